```python
import math
import jax, jax.numpy as jnp
from jax import lax
import numpy as np

D_MODEL = 2048
BATCH = 8
SEQ = 2048
DEPTH = 1

CHUNK = 64
LEFT_CHUNKS = 8
BAND = LEFT_CHUNKS + 1
N_HEADS = 16
HEAD_DIM = 64
ATTN_WIDTH = N_HEADS * HEAD_DIM
MAX_REL = 128
POOL_WINDOWS = (2, 4, 8, 16)
N_POOL_GROUPS = len(POOL_WINDOWS)
POOL_WIDTH = D_MODEL // 2
POOL_GROUP = POOL_WIDTH // N_POOL_GROUPS
IN_WIDTH = 3 * ATTN_WIDTH + POOL_WIDTH + 2 * D_MODEL
D_FF = ((8 * D_MODEL + 3 * 256 - 1) // (3 * 256)) * 256
EPS = 1e-6
NEG_INF = -1e30

kernel_name = "hybrid_chunk_attn_pool_gated_block"


def rms_norm(x, g):
    xf = x.astype(jnp.float32)
    y = xf * lax.rsqrt(jnp.mean(xf * xf, axis=-1, keepdims=True) + EPS)
    return (y * g.astype(jnp.float32)).astype(x.dtype)


def chunked_rel_attention(q, k, v, rel_bias):
    B, S, _ = q.shape
    nc = S // CHUNK
    qc = q.reshape(B, nc, CHUNK, N_HEADS, HEAD_DIM)
    pad = ((0, 0), (LEFT_CHUNKS * CHUNK, 0), (0, 0))
    kp = jnp.pad(k, pad).reshape(B, nc + LEFT_CHUNKS, CHUNK, N_HEADS, HEAD_DIM)
    vp = jnp.pad(v, pad).reshape(B, nc + LEFT_CHUNKS, CHUNK, N_HEADS, HEAD_DIM)
    band_idx = jnp.arange(nc)[:, None] + jnp.arange(BAND)[None, :]
    kb = kp[:, band_idx].reshape(B, nc, BAND * CHUNK, N_HEADS, HEAD_DIM)
    vb = vp[:, band_idx].reshape(B, nc, BAND * CHUNK, N_HEADS, HEAD_DIM)
    scale = 1.0 / math.sqrt(HEAD_DIM)
    s = jnp.einsum('bnihd,bnjhd->bhnij', qc, kb).astype(jnp.float32) * scale
    i = jnp.arange(CHUNK)
    j = jnp.arange(BAND * CHUNK)
    dist = LEFT_CHUNKS * CHUNK + i[:, None] - j[None, :]
    rel_idx = jnp.clip(dist, -MAX_REL, MAX_REL) + MAX_REL
    bias = rel_bias.astype(jnp.float32)[:, rel_idx]
    key_pos = (jnp.arange(nc)[:, None] - LEFT_CHUNKS) * CHUNK + j[None, :]
    valid = key_pos >= 0
    s = s + bias[None, :, None]
    s = jnp.where(valid[None, None, :, None, :], s, NEG_INF)
    p = jax.nn.softmax(s, axis=-1).astype(v.dtype)
    o = jnp.einsum('bhnij,bnjhd->bnihd', p, vb)
    return o.reshape(B, S, ATTN_WIDTH)


def multiscale_pool(u, pool_w, pool_scale):
    B, S, _ = u.shape
    uf = u.astype(jnp.float32)
    cs = jnp.concatenate([jnp.zeros((B, 1, POOL_WIDTH), jnp.float32),
                          jnp.cumsum(uf, axis=1)], axis=1)
    t = jnp.arange(S)
    outs = []
    for g, w in enumerate(POOL_WINDOWS):
        sl = slice(g * POOL_GROUP, (g + 1) * POOL_GROUP)
        lo = jnp.maximum(t + 1 - w, 0)
        cnt = (t + 1 - lo).astype(jnp.float32)[None, :, None]
        mean = (cs[:, 1:, sl] - cs[:, lo, sl]) / cnt
        outs.append(mean - uf[..., sl])
    d = jnp.stack(outs, axis=2).astype(u.dtype)
    y = jnp.einsum('bsgc,gce->bsge', d, pool_w).reshape(B, S, POOL_WIDTH)
    return y * pool_scale


def swiglu(h, w_gate_up, w_down):
    gu = h @ w_gate_up
    gate, up = gu[..., :D_FF], gu[..., D_FF:]
    return (jax.nn.silu(gate) * up) @ w_down


def setup_inputs(seed: int = 0) -> dict:
    key = jax.random.key(seed)
    ks = jax.random.split(key, 16)
    f32 = jnp.float32

    def nrm(k, shape, fan_in):
        return jax.random.normal(k, shape, f32) * (fan_in ** -0.5)

    def gain(k, shape):
        return 1.0 + 0.02 * jax.random.normal(k, shape, f32)

    return {
        "x": jax.random.normal(ks[0], (BATCH, SEQ, D_MODEL), f32),
        "norm_mix": gain(ks[1], (DEPTH, D_MODEL)),
        "w_in": nrm(ks[2], (DEPTH, D_MODEL, IN_WIDTH), D_MODEL),
        "rel_bias": 0.1 * jax.random.normal(ks[3], (DEPTH, N_HEADS, 2 * MAX_REL + 1), f32),
        "pool_w": nrm(ks[4], (DEPTH, N_POOL_GROUPS, POOL_GROUP, POOL_GROUP), POOL_GROUP),
        "pool_scale": gain(ks[5], (DEPTH, POOL_WIDTH)),
        "w_branch_a": nrm(ks[6], (DEPTH, ATTN_WIDTH, D_MODEL), ATTN_WIDTH),
        "w_branch_b": nrm(ks[7], (DEPTH, POOL_WIDTH, D_MODEL), POOL_WIDTH),
        "w_out": nrm(ks[8], (DEPTH, D_MODEL, D_MODEL), D_MODEL),
        "norm_ffn": gain(ks[9], (DEPTH, D_MODEL)),
        "w_gate_up": nrm(ks[10], (DEPTH, D_MODEL, 2 * D_FF), D_MODEL),
        "w_down": nrm(ks[11], (DEPTH, D_FF, D_MODEL), D_FF),
        "norm_final": gain(ks[12], (D_MODEL,)),
    }


def reference(x, norm_mix, w_in, rel_bias, pool_w, pool_scale, w_branch_a, w_branch_b,
              w_out, norm_ffn, w_gate_up, w_down, norm_final):
    a0 = 3 * ATTN_WIDTH
    a1 = a0 + POOL_WIDTH
    a2 = a1 + D_MODEL
    for l in range(DEPTH):
        h = rms_norm(x, norm_mix[l])
        z = h @ w_in[l]
        q = z[..., :ATTN_WIDTH]
        k = z[..., ATTN_WIDTH:2 * ATTN_WIDTH]
        v = z[..., 2 * ATTN_WIDTH:a0]
        u = z[..., a0:a1]
        gate_a = jax.nn.sigmoid(z[..., a1:a2])
        gate_b = jax.nn.sigmoid(z[..., a2:])
        y_a = chunked_rel_attention(q, k, v, rel_bias[l])
        y_b = multiscale_pool(u, pool_w[l], pool_scale[l])
        merged = gate_a * (y_a @ w_branch_a[l]) + gate_b * (y_b @ w_branch_b[l])
        x = x + merged @ w_out[l]
        x = x + swiglu(rms_norm(x, norm_ffn[l]), w_gate_up[l], w_down[l])
    return rms_norm(x, norm_final)
```

```python
import functools
import math

import jax
import jax.numpy as jnp
from jax import lax
from jax.experimental import pallas as pl
from jax.experimental.pallas import tpu as pltpu

F32 = jnp.float32
BF16 = jnp.bfloat16

CHUNK = 64
LEFT_CHUNKS = 8
N_HEADS = 16
HEAD_DIM = 64
ATTN_WIDTH = N_HEADS * HEAD_DIM
MAX_REL = 128
POOL_WINDOWS = (2, 4, 8, 16)
POOL_GROUP = 256
POOL_WIDTH = POOL_GROUP * len(POOL_WINDOWS)
EPS = 1e-6
NEG_INF = -1e30

Q_ROWS = 2 * CHUNK
BAND = (LEFT_CHUNKS + 2) * CHUNK
PAD_ROWS = LEFT_CHUNKS * CHUNK
BIAS_EXT = BAND + Q_ROWS
POOL_HALO = 16

VMEM_LIMIT_BYTES = 56 * 1024 * 1024


def _rms(x, g):
    ms = jnp.mean(x * x, axis=-1, keepdims=True)
    return x * lax.rsqrt(ms + EPS) * g


def _compiler_params(semantics):
    return pltpu.CompilerParams(
        dimension_semantics=semantics, vmem_limit_bytes=VMEM_LIMIT_BYTES)


def _in_proj_body(x_ref, g_ref, w_ref, qkv_ref, u_ref, gate_ref, h_ref, *,
                  n_qkv_tiles):
    j = pl.program_id(1)

    @pl.when(j == 0)
    def _():
        h_ref[...] = _rms(x_ref[...], g_ref[...]).astype(BF16)

    z = jnp.dot(h_ref[...], w_ref[...], preferred_element_type=F32)

    @pl.when(j < n_qkv_tiles)
    def _():
        qkv_ref[...] = z.astype(BF16)

    @pl.when(j == n_qkv_tiles)
    def _():
        u_ref[...] = z

    @pl.when(j > n_qkv_tiles)
    def _():
        gate_ref[...] = jax.nn.sigmoid(z).astype(BF16)


def _in_proj(x2d, g, w, *, tm):
    n, d = x2d.shape
    in_width = w.shape[1]
    tn = POOL_WIDTH
    n_qkv_tiles = 3 * ATTN_WIDTH // tn
    n_gate = in_width - 3 * ATTN_WIDTH - POOL_WIDTH
    grid = (n // tm, in_width // tn)
    return pl.pallas_call(
        functools.partial(_in_proj_body, n_qkv_tiles=n_qkv_tiles),
        grid=grid,
        in_specs=[
            pl.BlockSpec((tm, d), lambda i, j: (i, 0)),
            pl.BlockSpec((1, d), lambda i, j: (0, 0)),
            pl.BlockSpec((d, tn), lambda i, j: (0, j)),
        ],
        out_specs=[
            pl.BlockSpec((tm, tn), lambda i, j: (i, jnp.minimum(j, n_qkv_tiles - 1))),
            pl.BlockSpec((tm, tn), lambda i, j: (i, 0)),
            pl.BlockSpec((tm, tn), lambda i, j: (i, jnp.maximum(j - n_qkv_tiles - 1, 0))),
        ],
        out_shape=[
            jax.ShapeDtypeStruct((n, 3 * ATTN_WIDTH), BF16),
            jax.ShapeDtypeStruct((n, POOL_WIDTH), F32),
            jax.ShapeDtypeStruct((n, n_gate), BF16),
        ],
        scratch_shapes=[pltpu.VMEM((tm, d), BF16)],
        compiler_params=_compiler_params(("arbitrary", "arbitrary")),
        name="in_proj",
    )(x2d, g, w)


def _attn_body(ext_ref, q_ref, k_ref, v_ref, o_ref, kpad, vpad, bias_ref):
    b = pl.program_id(0)
    p = pl.program_id(1)

    @pl.when((b == 0) & (p == 0))
    def _():
        row = lax.broadcasted_iota(jnp.int32, (Q_ROWS, BAND), 0)
        col = lax.broadcasted_iota(jnp.int32, (Q_ROWS, BAND), 1)
        lo = (row // CHUNK) * CHUNK
        in_band = (col >= lo) & (col < lo + (LEFT_CHUNKS + 1) * CHUNK)
        for h in range(N_HEADS):
            base = jnp.broadcast_to(ext_ref[h:h + 1, :], (Q_ROWS, BIAS_EXT))
            toeplitz = pltpu.roll(base, 0, 1, stride=1, stride_axis=0)
            bias_ref[h] = jnp.where(in_band, toeplitz[:, :BAND], NEG_INF)
        kpad[0:PAD_ROWS, :] = jnp.zeros((PAD_ROWS, ATTN_WIDTH), BF16)
        vpad[0:PAD_ROWS, :] = jnp.zeros((PAD_ROWS, ATTN_WIDTH), BF16)

    @pl.when(p == 0)
    def _():
        kpad[PAD_ROWS:, :] = k_ref[...]
        vpad[PAD_ROWS:, :] = v_ref[...]

    start = pl.multiple_of(p * Q_ROWS, Q_ROWS)
    scale = 1.0 / math.sqrt(HEAD_DIM)
    col = lax.broadcasted_iota(jnp.int32, (Q_ROWS, BAND), 1)
    key_exists = col >= PAD_ROWS - p * Q_ROWS
    lane = lax.broadcasted_iota(jnp.int32, (Q_ROWS, 2 * HEAD_DIM), 1)
    first_head = lane < HEAD_DIM

    for hp in range(N_HEADS // 2):
        lanes = slice(hp * 2 * HEAD_DIM, (hp + 1) * 2 * HEAD_DIM)
        q2 = q_ref[:, lanes]
        k2 = kpad[pl.ds(start, BAND), lanes]
        v2 = vpad[pl.ds(start, BAND), lanes]
        outs = []
        for a in range(2):
            sel = first_head if a == 0 else jnp.logical_not(first_head)
            qa = jnp.where(sel, q2, jnp.zeros_like(q2))
            s = lax.dot_general(qa, k2, (((1,), (1,)), ((), ())),
                                preferred_element_type=F32)
            s = s * scale + bias_ref[2 * hp + a]
            s = jnp.where(key_exists, s, NEG_INF)
            m = jnp.max(s, axis=-1, keepdims=True)
            e = jnp.exp(s - m)
            l = jnp.sum(e, axis=-1, keepdims=True)
            pv = jnp.dot(e.astype(BF16), v2, preferred_element_type=F32)
            outs.append(pv / l)
        o_ref[:, lanes] = jnp.where(first_head, outs[0], outs[1]).astype(BF16)


def _attention(qkv, bias_ext, *, batch, seq):
    n = batch * seq
    steps = seq // Q_ROWS
    return pl.pallas_call(
        _attn_body,
        grid=(batch, steps),
        in_specs=[
            pl.BlockSpec((N_HEADS, BIAS_EXT), lambda b, p: (0, 0)),
            pl.BlockSpec((Q_ROWS, ATTN_WIDTH), lambda b, p: (b * steps + p, 0)),
            pl.BlockSpec((seq, ATTN_WIDTH), lambda b, p: (b, 1)),
            pl.BlockSpec((seq, ATTN_WIDTH), lambda b, p: (b, 2)),
        ],
        out_specs=pl.BlockSpec((Q_ROWS, ATTN_WIDTH), lambda b, p: (b * steps + p, 0)),
        out_shape=jax.ShapeDtypeStruct((n, ATTN_WIDTH), BF16),
        scratch_shapes=[
            pltpu.VMEM((PAD_ROWS + seq, ATTN_WIDTH), BF16),
            pltpu.VMEM((PAD_ROWS + seq, ATTN_WIDTH), BF16),
            pltpu.VMEM((N_HEADS, Q_ROWS, BAND), F32),
        ],
        compiler_params=_compiler_params(("arbitrary", "arbitrary")),
        name="attn",
    )(bias_ext, qkv, qkv, qkv)


def _bias_extended(rel_bias):
    far = jnp.broadcast_to(rel_bias[:, -1:], (N_HEADS, BAND - 2 * MAX_REL))
    wrap = jnp.broadcast_to(rel_bias[:, -1:], (N_HEADS, Q_ROWS - 1))
    return jnp.concatenate([far, rel_bias[:, ::-1], wrap], axis=1)


def _mix_body(ya_ref, u_ref, uprev_ref, ga_ref, gb_ref, x_ref, pw_ref, ps_ref,
              pa_ref, pb_ref, wo_ref, o_ref, ext_ref, yb_ref, *, tiles_per_seq):
    i = pl.program_id(0)
    tm = u_ref.shape[0]
    tile_in_seq = i % tiles_per_seq

    halo = jnp.where(tile_in_seq == 0, jnp.zeros_like(uprev_ref[...]), uprev_ref[...])
    ext_ref[0:POOL_HALO, :] = halo
    ext_ref[POOL_HALO:, :] = u_ref[...]

    pos = tile_in_seq * tm + lax.broadcasted_iota(jnp.int32, (tm, 1), 0)
    for g, w in enumerate(POOL_WINDOWS):
        cols = slice(g * POOL_GROUP, (g + 1) * POOL_GROUP)
        cur = ext_ref[POOL_HALO:POOL_HALO + tm, cols]
        acc = cur
        for k in range(1, w):
            acc = acc + ext_ref[POOL_HALO - k:POOL_HALO - k + tm, cols]
        cnt = jnp.minimum(pos + 1, w).astype(F32)
        d = (acc / cnt - cur).astype(BF16)
        y = jnp.dot(d, pw_ref[g], preferred_element_type=F32)
        yb_ref[:, cols] = (y * ps_ref[:, cols]).astype(BF16)

    a = jnp.dot(ya_ref[...], pa_ref[...], preferred_element_type=F32)
    bb = jnp.dot(yb_ref[...], pb_ref[...], preferred_element_type=F32)
    merged = ga_ref[...].astype(F32) * a + gb_ref[...].astype(F32) * bb
    y = jnp.dot(merged.astype(BF16), wo_ref[...], preferred_element_type=F32)
    o_ref[...] = x_ref[...] + y


def _mix(ya, u, gates, x2d, pool_w, pool_scale, w_a, w_b, w_o, *, seq, tm):
    n, d = x2d.shape
    tiles_per_seq = seq // tm
    halo_blocks = tm // POOL_HALO
    const = lambda *shape: pl.BlockSpec(shape, lambda i: (0,) * len(shape),
                                        pipeline_mode=pl.Buffered(1))
    return pl.pallas_call(
        functools.partial(_mix_body, tiles_per_seq=tiles_per_seq),
        grid=(n // tm,),
        in_specs=[
            pl.BlockSpec((tm, ATTN_WIDTH), lambda i: (i, 0)),
            pl.BlockSpec((tm, POOL_WIDTH), lambda i: (i, 0)),
            pl.BlockSpec((POOL_HALO, POOL_WIDTH),
                         lambda i: (jnp.maximum(i * halo_blocks - 1, 0), 0)),
            pl.BlockSpec((tm, d), lambda i: (i, 0)),
            pl.BlockSpec((tm, d), lambda i: (i, 1)),
            pl.BlockSpec((tm, d), lambda i: (i, 0)),
            const(len(POOL_WINDOWS), POOL_GROUP, POOL_GROUP),
            const(1, POOL_WIDTH),
            const(ATTN_WIDTH, d),
            const(POOL_WIDTH, d),
            const(d, d),
        ],
        out_specs=pl.BlockSpec((tm, d), lambda i: (i, 0)),
        out_shape=jax.ShapeDtypeStruct((n, d), F32),
        scratch_shapes=[
            pltpu.VMEM((POOL_HALO + tm, POOL_WIDTH), F32),
            pltpu.VMEM((tm, POOL_WIDTH), BF16),
        ],
        compiler_params=_compiler_params(("arbitrary",)),
        name="mix",
    )(ya, u, u, gates, gates, x2d, pool_w, pool_scale, w_a, w_b, w_o)


def _ffn_body(x_ref, g_ref, wg_ref, wu_ref, wd_ref, gf_ref, o_ref, h_ref, acc_ref):
    j = pl.program_id(1)

    @pl.when(j == 0)
    def _():
        h_ref[...] = _rms(x_ref[...], g_ref[...]).astype(BF16)
        acc_ref[...] = jnp.zeros_like(acc_ref)

    h = h_ref[...]
    gate = jnp.dot(h, wg_ref[...], preferred_element_type=F32)
    up = jnp.dot(h, wu_ref[...], preferred_element_type=F32)
    act = (jax.nn.silu(gate) * up).astype(BF16)
    acc_ref[...] += jnp.dot(act, wd_ref[...], preferred_element_type=F32)

    @pl.when(j == pl.num_programs(1) - 1)
    def _():
        o_ref[...] = _rms(x_ref[...] + acc_ref[...], gf_ref[...])


def _ffn(x2d, g, w_gate_up, w_down, g_final, *, tm, tf):
    n, d = x2d.shape
    d_ff = w_down.shape[0]
    n_f = d_ff // tf
    return pl.pallas_call(
        _ffn_body,
        grid=(n // tm, n_f),
        in_specs=[
            pl.BlockSpec((tm, d), lambda i, j: (i, 0)),
            pl.BlockSpec((1, d), lambda i, j: (0, 0)),
            pl.BlockSpec((d, tf), lambda i, j: (0, j)),
            pl.BlockSpec((d, tf), lambda i, j: (0, j + n_f)),
            pl.BlockSpec((tf, d), lambda i, j: (j, 0)),
            pl.BlockSpec((1, d), lambda i, j: (0, 0)),
        ],
        out_specs=pl.BlockSpec((tm, d), lambda i, j: (i, 0)),
        out_shape=jax.ShapeDtypeStruct((n, d), F32),
        scratch_shapes=[pltpu.VMEM((tm, d), BF16), pltpu.VMEM((tm, d), F32)],
        compiler_params=_compiler_params(("arbitrary", "arbitrary")),
        name="ffn",
    )(x2d, g, w_gate_up, w_gate_up, w_down, g_final)


def _layer(x2d, norm_mix, w_in, rel_bias, pool_w, pool_scale, w_a, w_b, w_o,
           norm_ffn, w_gate_up, w_down, norm_out, *, batch, seq):
    d = x2d.shape[1]
    qkv, u, gates = _in_proj(x2d, norm_mix.reshape(1, d), w_in.astype(BF16), tm=512)
    ya = _attention(qkv, _bias_extended(rel_bias), batch=batch, seq=seq)
    x1 = _mix(ya, u, gates, x2d, pool_w.astype(BF16), pool_scale.reshape(1, -1),
              w_a.astype(BF16), w_b.astype(BF16), w_o.astype(BF16), seq=seq, tm=256)
    return _ffn(x1, norm_ffn.reshape(1, d), w_gate_up.astype(BF16),
                w_down.astype(BF16), norm_out.reshape(1, d), tm=512, tf=512)


def kernel(x, norm_mix, w_in, rel_bias, pool_w, pool_scale, w_branch_a, w_branch_b,
           w_out, norm_ffn, w_gate_up, w_down, norm_final):
    batch, seq, d = x.shape
    depth = w_in.shape[0]
    assert depth == 1, "the fused ffn kernel applies the final norm to the only layer"
    out = _layer(x.reshape(batch * seq, d), norm_mix[0], w_in[0], rel_bias[0],
                 pool_w[0], pool_scale[0], w_branch_a[0], w_branch_b[0], w_out[0],
                 norm_ffn[0], w_gate_up[0], w_down[0], norm_final,
                 batch=batch, seq=seq)
    return out.reshape(batch, seq, d)
```

```python
import functools
import math

import jax
import jax.numpy as jnp
from jax import lax
from jax.experimental import pallas as pl
from jax.experimental.pallas import tpu as pltpu

F32 = jnp.float32
BF16 = jnp.bfloat16

CHUNK = 64
LEFT_CHUNKS = 8
N_HEADS = 16
HEAD_DIM = 64
ATTN_WIDTH = N_HEADS * HEAD_DIM
MAX_REL = 128
POOL_WINDOWS = (2, 4, 8, 16)
POOL_GROUP = 256
POOL_WIDTH = POOL_GROUP * len(POOL_WINDOWS)
EPS = 1e-6
NEG_INF = -1e30

Q_ROWS = 2 * CHUNK
BAND = (LEFT_CHUNKS + 2) * CHUNK
PAD_ROWS = LEFT_CHUNKS * CHUNK
BIAS_EXT = BAND + Q_ROWS
POOL_HALO = 16

VMEM_LIMIT_BYTES = 56 * 1024 * 1024


def _rms(x, g):
    ms = jnp.mean(x * x, axis=-1, keepdims=True)
    return x * lax.rsqrt(ms + EPS) * g


def _compiler_params(semantics):
    return pltpu.CompilerParams(
        dimension_semantics=semantics, vmem_limit_bytes=VMEM_LIMIT_BYTES)


def _in_proj_body(x_ref, g_ref, w_ref, qkv_ref, u_ref, gate_ref, h_ref, *,
                  n_qkv_tiles):
    j = pl.program_id(1)

    @pl.when(j == 0)
    def _():
        h_ref[...] = _rms(x_ref[...], g_ref[...]).astype(BF16)

    def z():
        return jnp.dot(h_ref[...], w_ref[...], preferred_element_type=F32)

    @pl.when(j < n_qkv_tiles)
    def _():
        qkv_ref[...] = z().astype(BF16)

    @pl.when(j == n_qkv_tiles)
    def _():
        u_ref[...] = z()

    @pl.when(j > n_qkv_tiles)
    def _():
        gate_ref[...] = jax.nn.sigmoid(z()).astype(BF16)


def _in_proj(x2d, g, w, *, tm):
    n, d = x2d.shape
    in_width = w.shape[1]
    tn = POOL_WIDTH
    n_qkv_tiles = 3 * ATTN_WIDTH // tn
    n_gate = in_width - 3 * ATTN_WIDTH - POOL_WIDTH
    grid = (n // tm, in_width // tn)
    return pl.pallas_call(
        functools.partial(_in_proj_body, n_qkv_tiles=n_qkv_tiles),
        grid=grid,
        in_specs=[
            pl.BlockSpec((tm, d), lambda i, j: (i, 0)),
            pl.BlockSpec((1, d), lambda i, j: (0, 0)),
            pl.BlockSpec((d, tn), lambda i, j: (0, j)),
        ],
        out_specs=[
            pl.BlockSpec((tm, tn), lambda i, j: (i, jnp.minimum(j, n_qkv_tiles - 1))),
            pl.BlockSpec((tm, tn), lambda i, j: (i, 0)),
            pl.BlockSpec((tm, tn), lambda i, j: (i, jnp.maximum(j - n_qkv_tiles - 1, 0))),
        ],
        out_shape=[
            jax.ShapeDtypeStruct((n, 3 * ATTN_WIDTH), BF16),
            jax.ShapeDtypeStruct((n, POOL_WIDTH), F32),
            jax.ShapeDtypeStruct((n, n_gate), BF16),
        ],
        scratch_shapes=[pltpu.VMEM((tm, d), BF16)],
        compiler_params=_compiler_params(("arbitrary", "arbitrary")),
        name="in_proj",
    )(x2d, g, w)


PAIR = 2 * HEAD_DIM
N_PAIRS = N_HEADS // 2
WIN_BLOCKS = BAND // Q_ROWS
PAD_BLOCKS = PAD_ROWS // Q_ROWS


def _attn_body(base_ref, q_ref, k_ref, v_ref, o_ref, kpad, vtpad, bias_ref):
    b = pl.program_id(0)
    p = pl.program_id(1)

    @pl.when((b == 0) & (p == 0))
    def _():
        key = lax.broadcasted_iota(jnp.int32, (BAND, 2 * Q_ROWS), 0)
        qry = lax.broadcasted_iota(jnp.int32, (BAND, 2 * Q_ROWS), 1) % Q_ROWS
        lo = (qry // CHUNK) * CHUNK
        in_band = (key >= lo) & (key < lo + (LEFT_CHUNKS + 1) * CHUNK)
        for hp in range(N_PAIRS):
            halves = []
            for a in range(2):
                h = 2 * hp + a
                base = jnp.broadcast_to(base_ref[h:h + 1, :], (BAND, BIAS_EXT))
                rot = pltpu.roll(base, 0, 1, stride=1, stride_axis=0)
                halves.append(rot[:, :Q_ROWS])
            bias_ref[hp] = jnp.where(in_band, jnp.concatenate(halves, axis=1), NEG_INF)
        kpad[0:PAD_ROWS, :] = jnp.zeros((PAD_ROWS, ATTN_WIDTH), BF16)
        vtpad[0:PAD_BLOCKS] = jnp.zeros((PAD_BLOCKS, ATTN_WIDTH, Q_ROWS), BF16)

    @pl.when(p == 0)
    def _():
        kpad[PAD_ROWS:, :] = k_ref[...]
        for c in range(v_ref.shape[0] // Q_ROWS):
            blk = v_ref[c * Q_ROWS:(c + 1) * Q_ROWS, :].astype(F32)
            vtpad[PAD_BLOCKS + c] = blk.T.astype(BF16)

    start = pl.multiple_of(p * Q_ROWS, Q_ROWS)
    lane = lax.broadcasted_iota(jnp.int32, (Q_ROWS, PAIR), 1)
    first_head = lane < HEAD_DIM
    row = lax.broadcasted_iota(jnp.int32, (PAIR, Q_ROWS), 0)
    first_head_rows = row < HEAD_DIM
    scale = jnp.asarray(1.0 / math.sqrt(HEAD_DIM), BF16)

    def step(mask_missing_keys):
        if mask_missing_keys:
            key = lax.broadcasted_iota(jnp.int32, (BAND, 2 * Q_ROWS), 0)
            key_exists = key >= PAD_ROWS - p * Q_ROWS
        for hp in range(N_PAIRS):
            lanes = slice(hp * PAIR, (hp + 1) * PAIR)
            q2 = q_ref[:, lanes] * scale
            zero = jnp.zeros_like(q2)
            qm = jnp.concatenate([jnp.where(first_head, q2, zero),
                                  jnp.where(first_head, zero, q2)], axis=0)
            k2 = kpad[pl.ds(start, BAND), lanes]
            s = lax.dot_general(k2, qm, (((1,), (1,)), ((), ())),
                                preferred_element_type=F32)
            s = s + bias_ref[hp]
            if mask_missing_keys:
                s = jnp.where(key_exists, s, NEG_INF)
            m = jnp.max(s, axis=0, keepdims=True)
            e = jnp.exp(s - m)
            l = jnp.sum(e, axis=0, keepdims=True)
            vt2 = jnp.concatenate(
                [vtpad[p + t, lanes, :] for t in range(WIN_BLOCKS)], axis=1)
            o_t = jnp.dot(vt2, e.astype(BF16), preferred_element_type=F32) / l
            own = jnp.where(first_head_rows, o_t[:, :Q_ROWS], o_t[:, Q_ROWS:])
            o_ref[:, lanes] = own.T.astype(BF16)

    @pl.when(p < PAD_BLOCKS)
    def _():
        step(True)

    @pl.when(p >= PAD_BLOCKS)
    def _():
        step(False)


def _attention(qkv, bias_base, *, batch, seq):
    n = batch * seq
    steps = seq // Q_ROWS
    return pl.pallas_call(
        _attn_body,
        grid=(batch, steps),
        in_specs=[
            pl.BlockSpec((N_HEADS, BIAS_EXT), lambda b, p: (0, 0)),
            pl.BlockSpec((Q_ROWS, ATTN_WIDTH), lambda b, p: (b * steps + p, 0)),
            pl.BlockSpec((seq, ATTN_WIDTH), lambda b, p: (b, 1)),
            pl.BlockSpec((seq, ATTN_WIDTH), lambda b, p: (b, 2)),
        ],
        out_specs=pl.BlockSpec((Q_ROWS, ATTN_WIDTH), lambda b, p: (b * steps + p, 0)),
        out_shape=jax.ShapeDtypeStruct((n, ATTN_WIDTH), BF16),
        scratch_shapes=[
            pltpu.VMEM((PAD_ROWS + seq, ATTN_WIDTH), BF16),
            pltpu.VMEM((PAD_BLOCKS + seq // Q_ROWS, ATTN_WIDTH, Q_ROWS), BF16),
            pltpu.VMEM((N_PAIRS, BAND, 2 * Q_ROWS), F32),
        ],
        compiler_params=_compiler_params(("arbitrary", "arbitrary")),
        name="attn",
    )(bias_base, qkv, qkv, qkv)


def _bias_base(rel_bias):
    far = rel_bias[:, -1:]
    head = jnp.broadcast_to(far, (N_HEADS, Q_ROWS))
    tail = jnp.broadcast_to(far, (N_HEADS, BIAS_EXT - Q_ROWS - 2 * MAX_REL - 1))
    return jnp.concatenate([head, rel_bias, tail], axis=1)


def _mix_body(ya_ref, u_ref, uprev_ref, ga_ref, gb_ref, x_ref, pw_ref, ps_ref,
              pa_ref, pb_ref, wo_ref, o_ref, ext_ref, yb_ref, *, tiles_per_seq):
    i = pl.program_id(0)
    tm = u_ref.shape[0]
    tile_in_seq = i % tiles_per_seq

    halo = jnp.where(tile_in_seq == 0, jnp.zeros_like(uprev_ref[...]), uprev_ref[...])
    ext_ref[0:POOL_HALO, :] = halo
    ext_ref[POOL_HALO:, :] = u_ref[...]

    pos = tile_in_seq * tm + lax.broadcasted_iota(jnp.int32, (tm, 1), 0)
    for g, w in enumerate(POOL_WINDOWS):
        cols = slice(g * POOL_GROUP, (g + 1) * POOL_GROUP)
        cur = ext_ref[POOL_HALO:POOL_HALO + tm, cols]
        acc = cur
        for k in range(1, w):
            acc = acc + ext_ref[POOL_HALO - k:POOL_HALO - k + tm, cols]
        cnt = jnp.minimum(pos + 1, w).astype(F32)
        d = (acc / cnt - cur).astype(BF16)
        y = jnp.dot(d, pw_ref[g], preferred_element_type=F32)
        yb_ref[:, cols] = (y * ps_ref[:, cols]).astype(BF16)

    a = jnp.dot(ya_ref[...], pa_ref[...], preferred_element_type=F32)
    bb = jnp.dot(yb_ref[...], pb_ref[...], preferred_element_type=F32)
    merged = ga_ref[...].astype(F32) * a + gb_ref[...].astype(F32) * bb
    y = jnp.dot(merged.astype(BF16), wo_ref[...], preferred_element_type=F32)
    o_ref[...] = x_ref[...] + y


def _mix(ya, u, gates, x2d, pool_w, pool_scale, w_a, w_b, w_o, *, seq, tm):
    n, d = x2d.shape
    tiles_per_seq = seq // tm
    halo_blocks = tm // POOL_HALO
    const = lambda *shape: pl.BlockSpec(shape, lambda i: (0,) * len(shape),
                                        pipeline_mode=pl.Buffered(1))
    return pl.pallas_call(
        functools.partial(_mix_body, tiles_per_seq=tiles_per_seq),
        grid=(n // tm,),
        in_specs=[
            pl.BlockSpec((tm, ATTN_WIDTH), lambda i: (i, 0)),
            pl.BlockSpec((tm, POOL_WIDTH), lambda i: (i, 0)),
            pl.BlockSpec((POOL_HALO, POOL_WIDTH),
                         lambda i: (jnp.maximum(i * halo_blocks - 1, 0), 0)),
            pl.BlockSpec((tm, d), lambda i: (i, 0)),
            pl.BlockSpec((tm, d), lambda i: (i, 1)),
            pl.BlockSpec((tm, d), lambda i: (i, 0)),
            const(len(POOL_WINDOWS), POOL_GROUP, POOL_GROUP),
            const(1, POOL_WIDTH),
            const(ATTN_WIDTH, d),
            const(POOL_WIDTH, d),
            const(d, d),
        ],
        out_specs=pl.BlockSpec((tm, d), lambda i: (i, 0)),
        out_shape=jax.ShapeDtypeStruct((n, d), F32),
        scratch_shapes=[
            pltpu.VMEM((POOL_HALO + tm, POOL_WIDTH), F32),
            pltpu.VMEM((tm, POOL_WIDTH), BF16),
        ],
        compiler_params=_compiler_params(("arbitrary",)),
        name="mix",
    )(ya, u, u, gates, gates, x2d, pool_w, pool_scale, w_a, w_b, w_o)


def _ffn_body(x_ref, g_ref, wg_ref, wu_ref, wd_ref, gf_ref, o_ref, h_ref, acc_ref):
    j = pl.program_id(1)

    @pl.when(j == 0)
    def _():
        h_ref[...] = _rms(x_ref[...], g_ref[...]).astype(BF16)
        acc_ref[...] = jnp.zeros_like(acc_ref)

    h = h_ref[...]
    gate = jnp.dot(h, wg_ref[...], preferred_element_type=F32)
    up = jnp.dot(h, wu_ref[...], preferred_element_type=F32)
    act = (jax.nn.silu(gate) * up).astype(BF16)
    acc_ref[...] += jnp.dot(act, wd_ref[...], preferred_element_type=F32)

    @pl.when(j == pl.num_programs(1) - 1)
    def _():
        o_ref[...] = _rms(x_ref[...] + acc_ref[...], gf_ref[...])


def _ffn(x2d, g, w_gate_up, w_down, g_final, *, tm, tf):
    n, d = x2d.shape
    d_ff = w_down.shape[0]
    n_f = d_ff // tf
    return pl.pallas_call(
        _ffn_body,
        grid=(n // tm, n_f),
        in_specs=[
            pl.BlockSpec((tm, d), lambda i, j: (i, 0)),
            pl.BlockSpec((1, d), lambda i, j: (0, 0)),
            pl.BlockSpec((d, tf), lambda i, j: (0, j)),
            pl.BlockSpec((d, tf), lambda i, j: (0, j + n_f)),
            pl.BlockSpec((tf, d), lambda i, j: (j, 0)),
            pl.BlockSpec((1, d), lambda i, j: (0, 0)),
        ],
        out_specs=pl.BlockSpec((tm, d), lambda i, j: (i, 0)),
        out_shape=jax.ShapeDtypeStruct((n, d), F32),
        scratch_shapes=[pltpu.VMEM((tm, d), BF16), pltpu.VMEM((tm, d), F32)],
        compiler_params=_compiler_params(("arbitrary", "arbitrary")),
        name="ffn",
    )(x2d, g, w_gate_up, w_gate_up, w_down, g_final)


def _layer(x2d, norm_mix, w_in, rel_bias, pool_w, pool_scale, w_a, w_b, w_o,
           norm_ffn, w_gate_up, w_down, norm_out, *, batch, seq):
    d = x2d.shape[1]
    qkv, u, gates = _in_proj(x2d, norm_mix.reshape(1, d), w_in.astype(BF16), tm=512)
    ya = _attention(qkv, _bias_base(rel_bias), batch=batch, seq=seq)
    x1 = _mix(ya, u, gates, x2d, pool_w.astype(BF16), pool_scale.reshape(1, -1),
              w_a.astype(BF16), w_b.astype(BF16), w_o.astype(BF16), seq=seq, tm=256)
    return _ffn(x1, norm_ffn.reshape(1, d), w_gate_up.astype(BF16),
                w_down.astype(BF16), norm_out.reshape(1, d), tm=512, tf=512)


def kernel(x, norm_mix, w_in, rel_bias, pool_w, pool_scale, w_branch_a, w_branch_b,
           w_out, norm_ffn, w_gate_up, w_down, norm_final):
    batch, seq, d = x.shape
    depth = w_in.shape[0]
    assert depth == 1, "the fused ffn kernel applies the final norm to the only layer"
    out = _layer(x.reshape(batch * seq, d), norm_mix[0], w_in[0], rel_bias[0],
                 pool_w[0], pool_scale[0], w_branch_a[0], w_branch_b[0], w_out[0],
                 norm_ffn[0], w_gate_up[0], w_down[0], norm_final,
                 batch=batch, seq=seq)
    return out.reshape(batch, seq, d)
```

```python
import functools
import math

import jax
import jax.numpy as jnp
from jax import lax
from jax.experimental import pallas as pl
from jax.experimental.pallas import tpu as pltpu

F32 = jnp.float32
BF16 = jnp.bfloat16

CHUNK = 64
LEFT_CHUNKS = 8
N_HEADS = 16
HEAD_DIM = 64
ATTN_WIDTH = N_HEADS * HEAD_DIM
MAX_REL = 128
POOL_WINDOWS = (2, 4, 8, 16)
POOL_GROUP = 256
POOL_WIDTH = POOL_GROUP * len(POOL_WINDOWS)
EPS = 1e-6
NEG_INF = -1e30

Q_ROWS = 2 * CHUNK
BAND = (LEFT_CHUNKS + 2) * CHUNK
PAD_ROWS = LEFT_CHUNKS * CHUNK
BIAS_EXT = BAND + Q_ROWS
POOL_HALO = 16

VMEM_LIMIT_BYTES = 56 * 1024 * 1024


def _rms(x, g):
    ms = jnp.mean(x * x, axis=-1, keepdims=True)
    return x * lax.rsqrt(ms + EPS) * g


def _compiler_params(semantics):
    return pltpu.CompilerParams(
        dimension_semantics=semantics, vmem_limit_bytes=VMEM_LIMIT_BYTES)


def _in_proj_body(x_ref, g_ref, w_ref, qkv_ref, u_ref, gate_ref, h_ref, *,
                  n_qkv_tiles, n_u_tiles):
    j = pl.program_id(1)

    @pl.when(j == 0)
    def _():
        h_ref[...] = _rms(x_ref[...], g_ref[...]).astype(BF16)

    def z():
        return jnp.dot(h_ref[...], w_ref[...], preferred_element_type=F32)

    @pl.when(j < n_qkv_tiles)
    def _():
        qkv_ref[...] = z().astype(BF16)

    @pl.when((j >= n_qkv_tiles) & (j < n_qkv_tiles + n_u_tiles))
    def _():
        u_ref[...] = z()

    @pl.when(j >= n_qkv_tiles + n_u_tiles)
    def _():
        gate_ref[...] = jax.nn.sigmoid(z()).astype(BF16)


def _in_proj(x2d, g, w, *, tm, tn):
    n, d = x2d.shape
    in_width = w.shape[1]
    n_qkv_tiles = 3 * ATTN_WIDTH // tn
    n_u_tiles = POOL_WIDTH // tn
    n_gate = in_width - 3 * ATTN_WIDTH - POOL_WIDTH
    grid = (n // tm, in_width // tn)
    return pl.pallas_call(
        functools.partial(_in_proj_body, n_qkv_tiles=n_qkv_tiles, n_u_tiles=n_u_tiles),
        grid=grid,
        in_specs=[
            pl.BlockSpec((tm, d), lambda i, j: (i, 0)),
            pl.BlockSpec((1, d), lambda i, j: (0, 0)),
            pl.BlockSpec((d, tn), lambda i, j: (0, j)),
        ],
        out_specs=[
            pl.BlockSpec((tm, tn), lambda i, j: (i, jnp.minimum(j, n_qkv_tiles - 1))),
            pl.BlockSpec((tm, tn),
                         lambda i, j: (i, jnp.clip(j - n_qkv_tiles, 0, n_u_tiles - 1))),
            pl.BlockSpec((tm, tn),
                         lambda i, j: (i, jnp.maximum(j - n_qkv_tiles - n_u_tiles, 0))),
        ],
        out_shape=[
            jax.ShapeDtypeStruct((n, 3 * ATTN_WIDTH), BF16),
            jax.ShapeDtypeStruct((n, POOL_WIDTH), F32),
            jax.ShapeDtypeStruct((n, n_gate), BF16),
        ],
        scratch_shapes=[pltpu.VMEM((tm, d), BF16)],
        compiler_params=_compiler_params(("arbitrary", "arbitrary")),
        name="in_proj",
    )(x2d, g, w)


PAIR = 2 * HEAD_DIM
N_PAIRS = N_HEADS // 2
WIN_BLOCKS = BAND // Q_ROWS
PAD_BLOCKS = PAD_ROWS // Q_ROWS
FLAT_END = PAD_ROWS - MAX_REL
BIAS_ROWS = CHUNK + BAND - FLAT_END
ONES_ROWS = 16


def _attn_body(base_ref, q_ref, k_ref, v_ref, o_ref, kpad, vtpad, bias_ref,
               s_ref, p_ref):
    b = pl.program_id(0)
    p = pl.program_id(1)

    @pl.when((b == 0) & (p == 0))
    def _():
        key = lax.broadcasted_iota(jnp.int32, (BAND, 2 * Q_ROWS), 0)
        qry = lax.broadcasted_iota(jnp.int32, (BAND, 2 * Q_ROWS), 1) % Q_ROWS
        lo = (qry // CHUNK) * CHUNK
        in_band = (key >= lo) & (key < lo + (LEFT_CHUNKS + 1) * CHUNK)
        for hp in range(N_PAIRS):
            halves = []
            for a in range(2):
                h = 2 * hp + a
                base = jnp.broadcast_to(base_ref[h:h + 1, :], (BAND, BIAS_EXT))
                rot = pltpu.roll(base, 0, 1, stride=1, stride_axis=0)
                halves.append(rot[:, :Q_ROWS] - base[:, 0:1])
            full = jnp.where(in_band, jnp.concatenate(halves, axis=1), NEG_INF)
            bias_ref[hp, 0:CHUNK] = full[0:CHUNK]
            bias_ref[hp, CHUNK:] = full[FLAT_END:]
        kpad[0:PAD_ROWS, :] = jnp.zeros((PAD_ROWS, ATTN_WIDTH), BF16)
        vtpad[0:PAD_BLOCKS, :, 0:PAIR, :] = jnp.zeros(
            (PAD_BLOCKS, N_PAIRS, PAIR, Q_ROWS), BF16)
        vtpad[:, :, PAIR:, :] = jnp.ones(
            (vtpad.shape[0], N_PAIRS, ONES_ROWS, Q_ROWS), BF16)

    @pl.when(p == 0)
    def _():
        kpad[PAD_ROWS:, :] = k_ref[...]
        for c in range(v_ref.shape[0] // Q_ROWS):
            blk_t = v_ref[c * Q_ROWS:(c + 1) * Q_ROWS, :].astype(F32).T.astype(BF16)
            for hp in range(N_PAIRS):
                vtpad[PAD_BLOCKS + c, hp, 0:PAIR, :] = blk_t[hp * PAIR:(hp + 1) * PAIR, :]

    start = pl.multiple_of(p * Q_ROWS, Q_ROWS)
    lane = lax.broadcasted_iota(jnp.int32, (Q_ROWS, PAIR), 1)
    first_head = lane < HEAD_DIM
    row = lax.broadcasted_iota(jnp.int32, (PAIR, Q_ROWS), 0)
    first_head_rows = row < HEAD_DIM
    scale = jnp.asarray(1.0 / math.sqrt(HEAD_DIM), BF16)

    def step(mask_missing_keys):
        first_key = PAD_ROWS - p * Q_ROWS

        def scores(hp):
            lanes = slice(hp * PAIR, (hp + 1) * PAIR)
            q2 = q_ref[:, lanes] * scale
            zero = jnp.zeros_like(q2)
            qm = jnp.concatenate([jnp.where(first_head, q2, zero),
                                  jnp.where(first_head, zero, q2)], axis=0)
            k2 = kpad[pl.ds(start, BAND), lanes]
            s = lax.dot_general(k2, qm, (((1,), (1,)), ((), ())),
                                preferred_element_type=F32)
            parts = [(0, CHUNK, s[0:CHUNK] + bias_ref[hp, 0:CHUNK]),
                     (CHUNK, FLAT_END, s[CHUNK:FLAT_END]),
                     (FLAT_END, BAND, s[FLAT_END:] + bias_ref[hp, CHUNK:])]
            for r0, r1, part in parts:
                if mask_missing_keys:
                    key = r0 + lax.broadcasted_iota(jnp.int32, part.shape, 0)
                    part = jnp.where(key >= first_key, part, NEG_INF)
                s_ref[hp % 2, r0:r1] = part

        def weights(hp):
            slot = hp % 2
            m = jnp.max(s_ref[slot], axis=0, keepdims=True)
            p_ref[slot] = jnp.exp(s_ref[slot] - m).astype(BF16)

        def values(hp):
            lanes = slice(hp * PAIR, (hp + 1) * PAIR)
            vt2 = jnp.concatenate(
                [vtpad[p + t, hp] for t in range(WIN_BLOCKS)], axis=1)
            o_t = jnp.dot(vt2, p_ref[hp % 2], preferred_element_type=F32)
            o_t = o_t[0:PAIR] / o_t[PAIR:PAIR + 1]
            own = jnp.where(first_head_rows, o_t[:, :Q_ROWS], o_t[:, Q_ROWS:])
            o_ref[:, lanes] = own.T.astype(BF16)

        scores(0)
        for hp in range(N_PAIRS + 1):
            if hp + 1 < N_PAIRS:
                scores(hp + 1)
            if hp >= 1:
                values(hp - 1)
            if hp < N_PAIRS:
                weights(hp)

    @pl.when(p < PAD_BLOCKS)
    def _():
        step(True)

    @pl.when(p >= PAD_BLOCKS)
    def _():
        step(False)


def _attention(qkv, bias_base, *, batch, seq):
    n = batch * seq
    steps = seq // Q_ROWS
    return pl.pallas_call(
        _attn_body,
        grid=(batch, steps),
        in_specs=[
            pl.BlockSpec((N_HEADS, BIAS_EXT), lambda b, p: (0, 0)),
            pl.BlockSpec((Q_ROWS, ATTN_WIDTH), lambda b, p: (b * steps + p, 0)),
            pl.BlockSpec((seq, ATTN_WIDTH), lambda b, p: (b, 1)),
            pl.BlockSpec((seq, ATTN_WIDTH), lambda b, p: (b, 2)),
        ],
        out_specs=pl.BlockSpec((Q_ROWS, ATTN_WIDTH), lambda b, p: (b * steps + p, 0)),
        out_shape=jax.ShapeDtypeStruct((n, ATTN_WIDTH), BF16),
        scratch_shapes=[
            pltpu.VMEM((PAD_ROWS + seq, ATTN_WIDTH), BF16),
            pltpu.VMEM((PAD_BLOCKS + seq // Q_ROWS, N_PAIRS, PAIR + ONES_ROWS, Q_ROWS), BF16),
            pltpu.VMEM((N_PAIRS, BIAS_ROWS, 2 * Q_ROWS), F32),
            pltpu.VMEM((2, BAND, 2 * Q_ROWS), F32),
            pltpu.VMEM((2, BAND, 2 * Q_ROWS), BF16),
        ],
        compiler_params=_compiler_params(("arbitrary", "arbitrary")),
        name="attn",
    )(bias_base, qkv, qkv, qkv)


def _bias_base(rel_bias):
    far = rel_bias[:, -1:]
    head = jnp.broadcast_to(far, (N_HEADS, Q_ROWS))
    tail = jnp.broadcast_to(far, (N_HEADS, BIAS_EXT - Q_ROWS - 2 * MAX_REL - 1))
    return jnp.concatenate([head, rel_bias, tail], axis=1)


def _mix_body(ya_ref, u_ref, uprev_ref, ga_ref, gb_ref, x_ref, pw_ref, ps_ref,
              pa_ref, pb_ref, wo_ref, o_ref, ext_ref, yb_ref, *, tiles_per_seq):
    i = pl.program_id(0)
    tm = u_ref.shape[0]
    tile_in_seq = i % tiles_per_seq

    halo = jnp.where(tile_in_seq == 0, jnp.zeros_like(uprev_ref[...]), uprev_ref[...])
    ext_ref[0:POOL_HALO, :] = halo
    ext_ref[POOL_HALO:, :] = u_ref[...]

    pos = tile_in_seq * tm + lax.broadcasted_iota(jnp.int32, (tm, 1), 0)
    for g, w in enumerate(POOL_WINDOWS):
        cols = slice(g * POOL_GROUP, (g + 1) * POOL_GROUP)
        cur = ext_ref[POOL_HALO:POOL_HALO + tm, cols]
        acc = cur
        for k in range(1, w):
            acc = acc + ext_ref[POOL_HALO - k:POOL_HALO - k + tm, cols]
        cnt = jnp.minimum(pos + 1, w).astype(F32)
        d = (acc / cnt - cur).astype(BF16)
        y = jnp.dot(d, pw_ref[g], preferred_element_type=F32)
        yb_ref[:, cols] = (y * ps_ref[:, cols]).astype(BF16)

    a = jnp.dot(ya_ref[...], pa_ref[...], preferred_element_type=F32)
    bb = jnp.dot(yb_ref[...], pb_ref[...], preferred_element_type=F32)
    merged = ga_ref[...].astype(F32) * a + gb_ref[...].astype(F32) * bb
    y = jnp.dot(merged.astype(BF16), wo_ref[...], preferred_element_type=F32)
    o_ref[...] = x_ref[...] + y


def _mix(ya, u, gates, x2d, pool_w, pool_scale, w_a, w_b, w_o, *, seq, tm):
    n, d = x2d.shape
    tiles_per_seq = seq // tm
    halo_blocks = tm // POOL_HALO
    const = lambda *shape: pl.BlockSpec(shape, lambda i: (0,) * len(shape),
                                        pipeline_mode=pl.Buffered(1))
    return pl.pallas_call(
        functools.partial(_mix_body, tiles_per_seq=tiles_per_seq),
        grid=(n // tm,),
        in_specs=[
            pl.BlockSpec((tm, ATTN_WIDTH), lambda i: (i, 0)),
            pl.BlockSpec((tm, POOL_WIDTH), lambda i: (i, 0)),
            pl.BlockSpec((POOL_HALO, POOL_WIDTH),
                         lambda i: (jnp.maximum(i * halo_blocks - 1, 0), 0)),
            pl.BlockSpec((tm, d), lambda i: (i, 0)),
            pl.BlockSpec((tm, d), lambda i: (i, 1)),
            pl.BlockSpec((tm, d), lambda i: (i, 0)),
            const(len(POOL_WINDOWS), POOL_GROUP, POOL_GROUP),
            const(1, POOL_WIDTH),
            const(ATTN_WIDTH, d),
            const(POOL_WIDTH, d),
            const(d, d),
        ],
        out_specs=pl.BlockSpec((tm, d), lambda i: (i, 0)),
        out_shape=jax.ShapeDtypeStruct((n, d), F32),
        scratch_shapes=[
            pltpu.VMEM((POOL_HALO + tm, POOL_WIDTH), F32),
            pltpu.VMEM((tm, POOL_WIDTH), BF16),
        ],
        compiler_params=_compiler_params(("arbitrary",)),
        name="mix",
    )(ya, u, u, gates, gates, x2d, pool_w, pool_scale, w_a, w_b, w_o)


def _ffn_body(x_ref, g_ref, wg_ref, wu_ref, wd_ref, gf_ref, o_ref, h_ref, acc_ref):
    j = pl.program_id(1)

    @pl.when(j == 0)
    def _():
        h_ref[...] = _rms(x_ref[...], g_ref[...]).astype(BF16)
        acc_ref[...] = jnp.zeros_like(acc_ref)

    h = h_ref[...]
    gate = jnp.dot(h, wg_ref[...], preferred_element_type=F32)
    up = jnp.dot(h, wu_ref[...], preferred_element_type=F32)
    act = (jax.nn.silu(gate) * up).astype(BF16)
    acc_ref[...] += jnp.dot(act, wd_ref[...], preferred_element_type=F32)

    @pl.when(j == pl.num_programs(1) - 1)
    def _():
        o_ref[...] = _rms(x_ref[...] + acc_ref[...], gf_ref[...])


def _ffn(x2d, g, w_gate_up, w_down, g_final, *, tm, tf):
    n, d = x2d.shape
    d_ff = w_down.shape[0]
    n_f = d_ff // tf
    return pl.pallas_call(
        _ffn_body,
        grid=(n // tm, n_f),
        in_specs=[
            pl.BlockSpec((tm, d), lambda i, j: (i, 0)),
            pl.BlockSpec((1, d), lambda i, j: (0, 0)),
            pl.BlockSpec((d, tf), lambda i, j: (0, j)),
            pl.BlockSpec((d, tf), lambda i, j: (0, j + n_f)),
            pl.BlockSpec((tf, d), lambda i, j: (j, 0)),
            pl.BlockSpec((1, d), lambda i, j: (0, 0)),
        ],
        out_specs=pl.BlockSpec((tm, d), lambda i, j: (i, 0)),
        out_shape=jax.ShapeDtypeStruct((n, d), F32),
        scratch_shapes=[pltpu.VMEM((tm, d), BF16), pltpu.VMEM((tm, d), F32)],
        compiler_params=_compiler_params(("arbitrary", "arbitrary")),
        name="ffn",
    )(x2d, g, w_gate_up, w_gate_up, w_down, g_final)


def _layer(x2d, norm_mix, w_in, rel_bias, pool_w, pool_scale, w_a, w_b, w_o,
           norm_ffn, w_gate_up, w_down, norm_out, *, batch, seq):
    d = x2d.shape[1]
    qkv, u, gates = _in_proj(x2d, norm_mix.reshape(1, d), w_in.astype(BF16), tm=1024, tn=512)
    ya = _attention(qkv, _bias_base(rel_bias), batch=batch, seq=seq)
    x1 = _mix(ya, u, gates, x2d, pool_w.astype(BF16), pool_scale.reshape(1, -1),
              w_a.astype(BF16), w_b.astype(BF16), w_o.astype(BF16), seq=seq, tm=256)
    return _ffn(x1, norm_ffn.reshape(1, d), w_gate_up.astype(BF16),
                w_down.astype(BF16), norm_out.reshape(1, d), tm=512, tf=512)


def kernel(x, norm_mix, w_in, rel_bias, pool_w, pool_scale, w_branch_a, w_branch_b,
           w_out, norm_ffn, w_gate_up, w_down, norm_final):
    batch, seq, d = x.shape
    depth = w_in.shape[0]
    assert depth == 1, "the fused ffn kernel applies the final norm to the only layer"
    out = _layer(x.reshape(batch * seq, d), norm_mix[0], w_in[0], rel_bias[0],
                 pool_w[0], pool_scale[0], w_branch_a[0], w_branch_b[0], w_out[0],
                 norm_ffn[0], w_gate_up[0], w_down[0], norm_final,
                 batch=batch, seq=seq)
    return out.reshape(batch, seq, d)
```

```python
import functools
import math

import jax
import jax.numpy as jnp
from jax import lax
from jax.experimental import pallas as pl
from jax.experimental.pallas import tpu as pltpu

F32 = jnp.float32
BF16 = jnp.bfloat16

CHUNK = 64
LEFT_CHUNKS = 8
N_HEADS = 16
HEAD_DIM = 64
ATTN_WIDTH = N_HEADS * HEAD_DIM
MAX_REL = 128
POOL_WINDOWS = (2, 4, 8, 16)
POOL_GROUP = 256
POOL_WIDTH = POOL_GROUP * len(POOL_WINDOWS)
EPS = 1e-6
NEG_INF = -1e30

Q_ROWS = 2 * CHUNK
BAND = (LEFT_CHUNKS + 2) * CHUNK
PAD_ROWS = LEFT_CHUNKS * CHUNK
BIAS_EXT = BAND + Q_ROWS
POOL_HALO = 16

VMEM_LIMIT_BYTES = 60 * 1024 * 1024


def _rms(x, g):
    ms = jnp.mean(x * x, axis=-1, keepdims=True)
    return x * lax.rsqrt(ms + EPS) * g


def _compiler_params(semantics):
    return pltpu.CompilerParams(
        dimension_semantics=semantics, vmem_limit_bytes=VMEM_LIMIT_BYTES)


def _in_proj_body(x_ref, g_ref, w_ref, qkv_ref, u_ref, gate_ref, h_ref, *,
                  n_qkv_tiles, n_u_tiles):
    j = pl.program_id(1)

    @pl.when(j == 0)
    def _():
        h_ref[...] = _rms(x_ref[...], g_ref[...]).astype(BF16)

    def z():
        return jnp.dot(h_ref[...], w_ref[...], preferred_element_type=F32)

    @pl.when(j < n_qkv_tiles)
    def _():
        qkv_ref[...] = z().astype(BF16)

    @pl.when((j >= n_qkv_tiles) & (j < n_qkv_tiles + n_u_tiles))
    def _():
        u_ref[...] = z()

    @pl.when(j >= n_qkv_tiles + n_u_tiles)
    def _():
        gate_ref[...] = z().astype(BF16)


def _in_proj(x2d, g, w, *, tm, tn):
    n, d = x2d.shape
    in_width = w.shape[1]
    n_qkv_tiles = 3 * ATTN_WIDTH // tn
    n_u_tiles = POOL_WIDTH // tn
    n_gate = in_width - 3 * ATTN_WIDTH - POOL_WIDTH
    grid = (n // tm, in_width // tn)
    return pl.pallas_call(
        functools.partial(_in_proj_body, n_qkv_tiles=n_qkv_tiles, n_u_tiles=n_u_tiles),
        grid=grid,
        in_specs=[
            pl.BlockSpec((tm, d), lambda i, j: (i, 0)),
            pl.BlockSpec((1, d), lambda i, j: (0, 0)),
            pl.BlockSpec((d, tn), lambda i, j: (0, j)),
        ],
        out_specs=[
            pl.BlockSpec((tm, tn), lambda i, j: (i, jnp.minimum(j, n_qkv_tiles - 1))),
            pl.BlockSpec((tm, tn),
                         lambda i, j: (i, jnp.clip(j - n_qkv_tiles, 0, n_u_tiles - 1))),
            pl.BlockSpec((tm, tn),
                         lambda i, j: (i, jnp.maximum(j - n_qkv_tiles - n_u_tiles, 0))),
        ],
        out_shape=[
            jax.ShapeDtypeStruct((n, 3 * ATTN_WIDTH), BF16),
            jax.ShapeDtypeStruct((n, POOL_WIDTH), F32),
            jax.ShapeDtypeStruct((n, n_gate), BF16),
        ],
        scratch_shapes=[pltpu.VMEM((tm, d), BF16)],
        compiler_params=_compiler_params(("arbitrary", "arbitrary")),
        name="in_proj",
    )(x2d, g, w)


PAIR = 2 * HEAD_DIM
N_PAIRS = N_HEADS // 2
WIN_BLOCKS = BAND // Q_ROWS
PAD_BLOCKS = PAD_ROWS // Q_ROWS
FLAT_END = PAD_ROWS - MAX_REL
BIAS_ROWS = CHUNK + BAND - FLAT_END
ONES_ROWS = 16


def _attn_body(base_ref, q_ref, k_ref, v_ref, o_ref, kpad, vtpad, bias_ref,
               s_ref, p_ref):
    b = pl.program_id(0)
    p = pl.program_id(1)

    @pl.when((b == 0) & (p == 0))
    def _():
        key = lax.broadcasted_iota(jnp.int32, (BAND, 2 * Q_ROWS), 0)
        qry = lax.broadcasted_iota(jnp.int32, (BAND, 2 * Q_ROWS), 1) % Q_ROWS
        lo = (qry // CHUNK) * CHUNK
        in_band = (key >= lo) & (key < lo + (LEFT_CHUNKS + 1) * CHUNK)
        for hp in range(N_PAIRS):
            halves = []
            for a in range(2):
                h = 2 * hp + a
                base = jnp.broadcast_to(base_ref[h:h + 1, :], (BAND, BIAS_EXT))
                rot = pltpu.roll(base, 0, 1, stride=1, stride_axis=0)
                halves.append(rot[:, :Q_ROWS] - base[:, 0:1])
            full = jnp.where(in_band, jnp.concatenate(halves, axis=1), NEG_INF)
            bias_ref[hp, 0:CHUNK] = full[0:CHUNK]
            bias_ref[hp, CHUNK:] = full[FLAT_END:]
        kpad[0:PAD_ROWS, :] = jnp.zeros((PAD_ROWS, ATTN_WIDTH), BF16)
        vtpad[0:PAD_BLOCKS, :, 0:PAIR, :] = jnp.zeros(
            (PAD_BLOCKS, N_PAIRS, PAIR, Q_ROWS), BF16)
        vtpad[:, :, PAIR:, :] = jnp.ones(
            (vtpad.shape[0], N_PAIRS, ONES_ROWS, Q_ROWS), BF16)

    @pl.when(p == 0)
    def _():
        kpad[PAD_ROWS:, :] = k_ref[...]
        for c in range(v_ref.shape[0] // Q_ROWS):
            blk_t = v_ref[c * Q_ROWS:(c + 1) * Q_ROWS, :].astype(F32).T.astype(BF16)
            for hp in range(N_PAIRS):
                vtpad[PAD_BLOCKS + c, hp, 0:PAIR, :] = blk_t[hp * PAIR:(hp + 1) * PAIR, :]

    start = pl.multiple_of(p * Q_ROWS, Q_ROWS)
    lane = lax.broadcasted_iota(jnp.int32, (Q_ROWS, PAIR), 1)
    first_head = lane < HEAD_DIM
    row = lax.broadcasted_iota(jnp.int32, (PAIR, Q_ROWS), 0)
    first_head_rows = row < HEAD_DIM
    scale = jnp.asarray(1.0 / math.sqrt(HEAD_DIM), BF16)

    def step(mask_missing_keys):
        first_key = PAD_ROWS - p * Q_ROWS

        def scores(hp):
            lanes = slice(hp * PAIR, (hp + 1) * PAIR)
            q2 = q_ref[:, lanes] * scale
            zero = jnp.zeros_like(q2)
            qm = jnp.concatenate([jnp.where(first_head, q2, zero),
                                  jnp.where(first_head, zero, q2)], axis=0)
            k2 = kpad[pl.ds(start, BAND), lanes]
            s = lax.dot_general(k2, qm, (((1,), (1,)), ((), ())),
                                preferred_element_type=F32)
            parts = [(0, CHUNK, s[0:CHUNK] + bias_ref[hp, 0:CHUNK]),
                     (CHUNK, FLAT_END, s[CHUNK:FLAT_END]),
                     (FLAT_END, BAND, s[FLAT_END:] + bias_ref[hp, CHUNK:])]
            for r0, r1, part in parts:
                if mask_missing_keys:
                    key = r0 + lax.broadcasted_iota(jnp.int32, part.shape, 0)
                    part = jnp.where(key >= first_key, part, NEG_INF)
                s_ref[hp % 2, r0:r1] = part

        def weights(hp):
            slot = hp % 2
            m = jnp.max(s_ref[slot], axis=0, keepdims=True)
            p_ref[slot] = jnp.exp(s_ref[slot] - m).astype(BF16)

        def values(hp):
            lanes = slice(hp * PAIR, (hp + 1) * PAIR)
            vt2 = jnp.concatenate(
                [vtpad[p + t, hp] for t in range(WIN_BLOCKS)], axis=1)
            o_t = jnp.dot(vt2, p_ref[hp % 2], preferred_element_type=F32)
            o_t = o_t[0:PAIR] / o_t[PAIR:PAIR + 1]
            own = jnp.where(first_head_rows, o_t[:, :Q_ROWS], o_t[:, Q_ROWS:])
            o_ref[:, lanes] = own.T.astype(BF16)

        scores(0)
        for hp in range(N_PAIRS + 1):
            if hp + 1 < N_PAIRS:
                scores(hp + 1)
            if hp >= 1:
                values(hp - 1)
            if hp < N_PAIRS:
                weights(hp)

    @pl.when(p < PAD_BLOCKS)
    def _():
        step(True)

    @pl.when(p >= PAD_BLOCKS)
    def _():
        step(False)


def _attention(qkv, bias_base, *, batch, seq):
    n = batch * seq
    steps = seq // Q_ROWS
    return pl.pallas_call(
        _attn_body,
        grid=(batch, steps),
        in_specs=[
            pl.BlockSpec((N_HEADS, BIAS_EXT), lambda b, p: (0, 0)),
            pl.BlockSpec((Q_ROWS, ATTN_WIDTH), lambda b, p: (b * steps + p, 0)),
            pl.BlockSpec((seq, ATTN_WIDTH), lambda b, p: (b, 1)),
            pl.BlockSpec((seq, ATTN_WIDTH), lambda b, p: (b, 2)),
        ],
        out_specs=pl.BlockSpec((Q_ROWS, ATTN_WIDTH), lambda b, p: (b * steps + p, 0)),
        out_shape=jax.ShapeDtypeStruct((n, ATTN_WIDTH), BF16),
        scratch_shapes=[
            pltpu.VMEM((PAD_ROWS + seq, ATTN_WIDTH), BF16),
            pltpu.VMEM((PAD_BLOCKS + seq // Q_ROWS, N_PAIRS, PAIR + ONES_ROWS, Q_ROWS), BF16),
            pltpu.VMEM((N_PAIRS, BIAS_ROWS, 2 * Q_ROWS), F32),
            pltpu.VMEM((2, BAND, 2 * Q_ROWS), F32),
            pltpu.VMEM((2, BAND, 2 * Q_ROWS), BF16),
        ],
        compiler_params=_compiler_params(("arbitrary", "arbitrary")),
        name="attn",
    )(bias_base, qkv, qkv, qkv)


def _bias_base(rel_bias):
    far = rel_bias[:, -1:]
    head = jnp.broadcast_to(far, (N_HEADS, Q_ROWS))
    tail = jnp.broadcast_to(far, (N_HEADS, BIAS_EXT - Q_ROWS - 2 * MAX_REL - 1))
    return jnp.concatenate([head, rel_bias, tail], axis=1)


def _mix_body(ya_ref, u_ref, uprev_ref, ga_ref, gb_ref, x_ref, pw_ref, ps_ref,
              pa_ref, pb_ref, wo_ref, o_ref, ext_ref, yb_ref, *, tiles_per_seq):
    i = pl.program_id(0)
    tm = u_ref.shape[0]
    tile_in_seq = i % tiles_per_seq

    halo = jnp.where(tile_in_seq == 0, jnp.zeros_like(uprev_ref[...]), uprev_ref[...])
    ext_ref[0:POOL_HALO, :] = halo
    ext_ref[POOL_HALO:, :] = u_ref[...]

    pos = tile_in_seq * tm + lax.broadcasted_iota(jnp.int32, (tm, 1), 0)
    for g, w in enumerate(POOL_WINDOWS):
        cols = slice(g * POOL_GROUP, (g + 1) * POOL_GROUP)
        cur = ext_ref[POOL_HALO:POOL_HALO + tm, cols]
        acc = cur
        for k in range(1, w):
            acc = acc + ext_ref[POOL_HALO - k:POOL_HALO - k + tm, cols]
        cnt = jnp.minimum(pos + 1, w).astype(F32)
        d = (acc / cnt - cur).astype(BF16)
        y = jnp.dot(d, pw_ref[g], preferred_element_type=F32)
        yb_ref[:, cols] = (y * ps_ref[:, cols]).astype(BF16)

    a = jnp.dot(ya_ref[...], pa_ref[...], preferred_element_type=F32)
    bb = jnp.dot(yb_ref[...], pb_ref[...], preferred_element_type=F32)
    gate_a = jax.nn.sigmoid(ga_ref[...].astype(F32))
    gate_b = jax.nn.sigmoid(gb_ref[...].astype(F32))
    merged = gate_a * a + gate_b * bb
    y = jnp.dot(merged.astype(BF16), wo_ref[...], preferred_element_type=F32)
    o_ref[...] = x_ref[...] + y


def _mix(ya, u, gates, x2d, pool_w, pool_scale, w_a, w_b, w_o, *, seq, tm):
    n, d = x2d.shape
    tiles_per_seq = seq // tm
    halo_blocks = tm // POOL_HALO
    const = lambda *shape: pl.BlockSpec(shape, lambda i: (0,) * len(shape),
                                        pipeline_mode=pl.Buffered(1))
    return pl.pallas_call(
        functools.partial(_mix_body, tiles_per_seq=tiles_per_seq),
        grid=(n // tm,),
        in_specs=[
            pl.BlockSpec((tm, ATTN_WIDTH), lambda i: (i, 0)),
            pl.BlockSpec((tm, POOL_WIDTH), lambda i: (i, 0)),
            pl.BlockSpec((POOL_HALO, POOL_WIDTH),
                         lambda i: (jnp.maximum(i * halo_blocks - 1, 0), 0)),
            pl.BlockSpec((tm, d), lambda i: (i, 0)),
            pl.BlockSpec((tm, d), lambda i: (i, 1)),
            pl.BlockSpec((tm, d), lambda i: (i, 0)),
            const(len(POOL_WINDOWS), POOL_GROUP, POOL_GROUP),
            const(1, POOL_WIDTH),
            const(ATTN_WIDTH, d),
            const(POOL_WIDTH, d),
            const(d, d),
        ],
        out_specs=pl.BlockSpec((tm, d), lambda i: (i, 0)),
        out_shape=jax.ShapeDtypeStruct((n, d), F32),
        scratch_shapes=[
            pltpu.VMEM((POOL_HALO + tm, POOL_WIDTH), F32),
            pltpu.VMEM((tm, POOL_WIDTH), BF16),
        ],
        compiler_params=_compiler_params(("arbitrary",)),
        name="mix",
    )(ya, u, u, gates, gates, x2d, pool_w, pool_scale, w_a, w_b, w_o)


def _ffn_body(x_ref, g_ref, wg_ref, wu_ref, wd_ref, gf_ref, o_ref, h_ref):
    j = pl.program_id(1)

    @pl.when(j == 0)
    def _():
        x = x_ref[...]
        h_ref[...] = _rms(x, g_ref[...]).astype(BF16)
        o_ref[...] = x

    h = h_ref[...]
    gate = jnp.dot(h, wg_ref[...], preferred_element_type=F32)
    up = jnp.dot(h, wu_ref[...], preferred_element_type=F32)
    act = (jax.nn.silu(gate) * up).astype(BF16)
    o_ref[...] += jnp.dot(act, wd_ref[...], preferred_element_type=F32)

    @pl.when(j == pl.num_programs(1) - 1)
    def _():
        o_ref[...] = _rms(o_ref[...], gf_ref[...])


def _ffn(x2d, g, w_gate_up, w_down, g_final, *, tm, tf):
    n, d = x2d.shape
    d_ff = w_down.shape[0]
    n_f = d_ff // tf
    return pl.pallas_call(
        _ffn_body,
        grid=(n // tm, n_f),
        in_specs=[
            pl.BlockSpec((tm, d), lambda i, j: (i, 0)),
            pl.BlockSpec((1, d), lambda i, j: (0, 0)),
            pl.BlockSpec((d, tf), lambda i, j: (0, j)),
            pl.BlockSpec((d, tf), lambda i, j: (0, j + n_f)),
            pl.BlockSpec((tf, d), lambda i, j: (j, 0)),
            pl.BlockSpec((1, d), lambda i, j: (0, 0)),
        ],
        out_specs=pl.BlockSpec((tm, d), lambda i, j: (i, 0)),
        out_shape=jax.ShapeDtypeStruct((n, d), F32),
        scratch_shapes=[pltpu.VMEM((tm, d), BF16)],
        compiler_params=_compiler_params(("arbitrary", "arbitrary")),
        name="ffn",
    )(x2d, g, w_gate_up, w_gate_up, w_down, g_final)


def _layer(x2d, norm_mix, w_in, rel_bias, pool_w, pool_scale, w_a, w_b, w_o,
           norm_ffn, w_gate_up, w_down, norm_out, *, batch, seq):
    d = x2d.shape[1]
    qkv, u, gates = _in_proj(x2d, norm_mix.reshape(1, d), w_in.astype(BF16), tm=1024, tn=1024)
    ya = _attention(qkv, _bias_base(rel_bias), batch=batch, seq=seq)
    x1 = _mix(ya, u, gates, x2d, pool_w.astype(BF16), pool_scale.reshape(1, -1),
              w_a.astype(BF16), w_b.astype(BF16), w_o.astype(BF16), seq=seq, tm=256)
    return _ffn(x1, norm_ffn.reshape(1, d), w_gate_up.astype(BF16),
                w_down.astype(BF16), norm_out.reshape(1, d), tm=1024, tf=512)


def kernel(x, norm_mix, w_in, rel_bias, pool_w, pool_scale, w_branch_a, w_branch_b,
           w_out, norm_ffn, w_gate_up, w_down, norm_final):
    batch, seq, d = x.shape
    depth = w_in.shape[0]
    assert depth == 1, "the fused ffn kernel applies the final norm to the only layer"
    out = _layer(x.reshape(batch * seq, d), norm_mix[0], w_in[0], rel_bias[0],
                 pool_w[0], pool_scale[0], w_branch_a[0], w_branch_b[0], w_out[0],
                 norm_ffn[0], w_gate_up[0], w_down[0], norm_final,
                 batch=batch, seq=seq)
    return out.reshape(batch, seq, d)
```

```python
import functools
import math

import jax
import jax.numpy as jnp
from jax import lax
from jax.experimental import pallas as pl
from jax.experimental.pallas import tpu as pltpu

F32 = jnp.float32
BF16 = jnp.bfloat16

CHUNK = 64
LEFT_CHUNKS = 8
N_HEADS = 16
HEAD_DIM = 64
ATTN_WIDTH = N_HEADS * HEAD_DIM
MAX_REL = 128
POOL_WINDOWS = (2, 4, 8, 16)
POOL_GROUP = 256
POOL_WIDTH = POOL_GROUP * len(POOL_WINDOWS)
EPS = 1e-6
NEG_INF = -1e30

Q_ROWS = 2 * CHUNK
BAND = (LEFT_CHUNKS + 2) * CHUNK
PAD_ROWS = LEFT_CHUNKS * CHUNK
BIAS_EXT = BAND + Q_ROWS
POOL_HALO = 16

VMEM_LIMIT_BYTES = 60 * 1024 * 1024


def _rms(x, g):
    ms = jnp.mean(x * x, axis=-1, keepdims=True)
    return x * lax.rsqrt(ms + EPS) * g


def _compiler_params(semantics):
    return pltpu.CompilerParams(
        dimension_semantics=semantics, vmem_limit_bytes=VMEM_LIMIT_BYTES)


def _in_proj_body(x_ref, g_ref, w_ref, qkv_ref, u_ref, gate_ref, h_ref, *,
                  n_qkv_tiles, n_u_tiles):
    j = pl.program_id(1)

    @pl.when(j == 0)
    def _():
        h_ref[...] = _rms(x_ref[...], g_ref[...]).astype(BF16)

    def z():
        return jnp.dot(h_ref[...], w_ref[...], preferred_element_type=F32)

    @pl.when(j < n_qkv_tiles)
    def _():
        qkv_ref[...] = z().astype(BF16)

    @pl.when((j >= n_qkv_tiles) & (j < n_qkv_tiles + n_u_tiles))
    def _():
        u_ref[...] = z()

    @pl.when(j >= n_qkv_tiles + n_u_tiles)
    def _():
        gate_ref[...] = z().astype(BF16)


def _in_proj(x2d, g, w, *, tm, tn):
    n, d = x2d.shape
    in_width = w.shape[1]
    n_qkv_tiles = 3 * ATTN_WIDTH // tn
    n_u_tiles = POOL_WIDTH // tn
    n_gate = in_width - 3 * ATTN_WIDTH - POOL_WIDTH
    grid = (n // tm, in_width // tn)
    return pl.pallas_call(
        functools.partial(_in_proj_body, n_qkv_tiles=n_qkv_tiles, n_u_tiles=n_u_tiles),
        grid=grid,
        in_specs=[
            pl.BlockSpec((tm, d), lambda i, j: (i, 0)),
            pl.BlockSpec((1, d), lambda i, j: (0, 0)),
            pl.BlockSpec((d, tn), lambda i, j: (0, j)),
        ],
        out_specs=[
            pl.BlockSpec((tm, tn), lambda i, j: (i, jnp.minimum(j, n_qkv_tiles - 1))),
            pl.BlockSpec((tm, tn),
                         lambda i, j: (i, jnp.clip(j - n_qkv_tiles, 0, n_u_tiles - 1))),
            pl.BlockSpec((tm, tn),
                         lambda i, j: (i, jnp.maximum(j - n_qkv_tiles - n_u_tiles, 0))),
        ],
        out_shape=[
            jax.ShapeDtypeStruct((n, 3 * ATTN_WIDTH), BF16),
            jax.ShapeDtypeStruct((n, POOL_WIDTH), F32),
            jax.ShapeDtypeStruct((n, n_gate), BF16),
        ],
        scratch_shapes=[pltpu.VMEM((tm, d), BF16)],
        compiler_params=_compiler_params(("arbitrary", "arbitrary")),
        name="in_proj",
    )(x2d, g, w)


PAIR = 2 * HEAD_DIM
N_PAIRS = N_HEADS // 2
WIN_BLOCKS = BAND // Q_ROWS
PAD_BLOCKS = PAD_ROWS // Q_ROWS
FLAT_END = PAD_ROWS - MAX_REL
BIAS_ROWS = CHUNK + BAND - FLAT_END
ONES_ROWS = 16
SUBS = 4


def _attn_body(base_ref, q_ref, k_ref, v_ref, o_ref, kpad, vtpad, bias_ref,
               s_ref, p_ref):
    b = pl.program_id(0)
    p = pl.program_id(1)

    @pl.when((b == 0) & (p == 0))
    def _():
        key = lax.broadcasted_iota(jnp.int32, (BAND, 2 * Q_ROWS), 0)
        qry = lax.broadcasted_iota(jnp.int32, (BAND, 2 * Q_ROWS), 1) % Q_ROWS
        lo = (qry // CHUNK) * CHUNK
        in_band = (key >= lo) & (key < lo + (LEFT_CHUNKS + 1) * CHUNK)
        for hp in range(N_PAIRS):
            halves = []
            for a in range(2):
                h = 2 * hp + a
                base = jnp.broadcast_to(base_ref[h:h + 1, :], (BAND, BIAS_EXT))
                rot = pltpu.roll(base, 0, 1, stride=1, stride_axis=0)
                halves.append(rot[:, :Q_ROWS] - base[:, 0:1])
            full = jnp.where(in_band, jnp.concatenate(halves, axis=1), NEG_INF)
            bias_ref[hp, 0:CHUNK] = full[0:CHUNK]
            bias_ref[hp, CHUNK:] = full[FLAT_END:]
        kpad[0:PAD_ROWS, :] = jnp.zeros((PAD_ROWS, ATTN_WIDTH), BF16)
        vtpad[0:PAD_BLOCKS, :, 0:PAIR, :] = jnp.zeros(
            (PAD_BLOCKS, N_PAIRS, PAIR, Q_ROWS), BF16)
        vtpad[:, :, PAIR:, :] = jnp.ones(
            (vtpad.shape[0], N_PAIRS, ONES_ROWS, Q_ROWS), BF16)

    @pl.when(p == 0)
    def _():
        kpad[PAD_ROWS:, :] = k_ref[...]
        for c in range(v_ref.shape[0] // Q_ROWS):
            blk_t = v_ref[c * Q_ROWS:(c + 1) * Q_ROWS, :].astype(F32).T.astype(BF16)
            for hp in range(N_PAIRS):
                vtpad[PAD_BLOCKS + c, hp, 0:PAIR, :] = blk_t[hp * PAIR:(hp + 1) * PAIR, :]

    lane = lax.broadcasted_iota(jnp.int32, (Q_ROWS, PAIR), 1)
    first_head = lane < HEAD_DIM
    row = lax.broadcasted_iota(jnp.int32, (PAIR, Q_ROWS), 0)
    first_head_rows = row < HEAD_DIM
    scale = jnp.asarray(1.0 / math.sqrt(HEAD_DIM), BF16)

    def step(mask_missing_keys):
        units = [(sub, hp) for sub in range(SUBS) for hp in range(N_PAIRS)]

        def scores(u):
            sub, hp = units[u]
            blk = p * SUBS + sub
            lanes = slice(hp * PAIR, (hp + 1) * PAIR)
            q2 = q_ref[sub * Q_ROWS:(sub + 1) * Q_ROWS, lanes] * scale
            zero = jnp.zeros_like(q2)
            qm = jnp.concatenate([jnp.where(first_head, q2, zero),
                                  jnp.where(first_head, zero, q2)], axis=0)
            k2 = kpad[pl.ds(pl.multiple_of(blk * Q_ROWS, Q_ROWS), BAND), lanes]
            s = lax.dot_general(k2, qm, (((1,), (1,)), ((), ())),
                                preferred_element_type=F32)
            parts = [(0, CHUNK, s[0:CHUNK] + bias_ref[hp, 0:CHUNK]),
                     (CHUNK, FLAT_END, s[CHUNK:FLAT_END]),
                     (FLAT_END, BAND, s[FLAT_END:] + bias_ref[hp, CHUNK:])]
            for r0, r1, part in parts:
                if mask_missing_keys:
                    key = r0 + lax.broadcasted_iota(jnp.int32, part.shape, 0)
                    part = jnp.where(key >= PAD_ROWS - blk * Q_ROWS, part, NEG_INF)
                s_ref[u % 2, r0:r1] = part

        def weights(u):
            slot = u % 2
            m = jnp.max(s_ref[slot], axis=0, keepdims=True)
            p_ref[slot] = jnp.exp(s_ref[slot] - m).astype(BF16)

        def values(u):
            sub, hp = units[u]
            blk = p * SUBS + sub
            vt2 = jnp.concatenate(
                [vtpad[blk + t, hp] for t in range(WIN_BLOCKS)], axis=1)
            o_t = jnp.dot(vt2, p_ref[u % 2], preferred_element_type=F32)
            o_t = o_t[0:PAIR] / o_t[PAIR:PAIR + 1]
            own = jnp.where(first_head_rows, o_t[:, :Q_ROWS], o_t[:, Q_ROWS:])
            o_ref[sub * Q_ROWS:(sub + 1) * Q_ROWS, hp * PAIR:(hp + 1) * PAIR] = (
                own.T.astype(BF16))

        scores(0)
        for u in range(len(units) + 1):
            if u + 1 < len(units):
                scores(u + 1)
            if u >= 1:
                values(u - 1)
            if u < len(units):
                weights(u)

    @pl.when(p < PAD_BLOCKS // SUBS)
    def _():
        step(True)

    @pl.when(p >= PAD_BLOCKS // SUBS)
    def _():
        step(False)


def _attention(qkv, bias_base, *, batch, seq):
    n = batch * seq
    rows = SUBS * Q_ROWS
    steps = seq // rows
    return pl.pallas_call(
        _attn_body,
        grid=(batch, steps),
        in_specs=[
            pl.BlockSpec((N_HEADS, BIAS_EXT), lambda b, p: (0, 0)),
            pl.BlockSpec((rows, ATTN_WIDTH), lambda b, p: (b * steps + p, 0)),
            pl.BlockSpec((seq, ATTN_WIDTH), lambda b, p: (b, 1)),
            pl.BlockSpec((seq, ATTN_WIDTH), lambda b, p: (b, 2)),
        ],
        out_specs=pl.BlockSpec((rows, ATTN_WIDTH), lambda b, p: (b * steps + p, 0)),
        out_shape=jax.ShapeDtypeStruct((n, ATTN_WIDTH), BF16),
        scratch_shapes=[
            pltpu.VMEM((PAD_ROWS + seq, ATTN_WIDTH), BF16),
            pltpu.VMEM((PAD_BLOCKS + seq // Q_ROWS, N_PAIRS, PAIR + ONES_ROWS, Q_ROWS), BF16),
            pltpu.VMEM((N_PAIRS, BIAS_ROWS, 2 * Q_ROWS), F32),
            pltpu.VMEM((2, BAND, 2 * Q_ROWS), F32),
            pltpu.VMEM((2, BAND, 2 * Q_ROWS), BF16),
        ],
        compiler_params=_compiler_params(("arbitrary", "arbitrary")),
        name="attn",
    )(bias_base, qkv, qkv, qkv)


def _bias_base(rel_bias):
    far = rel_bias[:, -1:]
    head = jnp.broadcast_to(far, (N_HEADS, Q_ROWS))
    tail = jnp.broadcast_to(far, (N_HEADS, BIAS_EXT - Q_ROWS - 2 * MAX_REL - 1))
    return jnp.concatenate([head, rel_bias, tail], axis=1)


def _mix_body(ya_ref, u_ref, uprev_ref, ga_ref, gb_ref, x_ref, pw_ref, ps_ref,
              pa_ref, pb_ref, wo_ref, o_ref, ext_ref, yb_ref, a_ref, *, tiles_per_seq):
    i = pl.program_id(0)
    tm, d = x_ref.shape
    tile_in_seq = i % tiles_per_seq
    n_groups = len(POOL_WINDOWS)
    a_cols = d // n_groups

    halo = jnp.where(tile_in_seq == 0, jnp.zeros_like(uprev_ref[...]), uprev_ref[...])
    ext_ref[0:POOL_HALO, :] = halo
    ext_ref[POOL_HALO:, :] = u_ref[...]

    pos = tile_in_seq * tm + lax.broadcasted_iota(jnp.int32, (tm, 1), 0)
    for g, w in enumerate(POOL_WINDOWS):
        acols = slice(g * a_cols, (g + 1) * a_cols)
        a_ref[:, acols] = jnp.dot(ya_ref[...], pa_ref[:, acols],
                                  preferred_element_type=F32)
        cols = slice(g * POOL_GROUP, (g + 1) * POOL_GROUP)
        acc = ext_ref[:, cols]
        shift = 1
        while shift < w:
            acc = acc + pltpu.roll(acc, shift, 0)
            shift *= 2
        acc = acc[POOL_HALO:]
        cur = ext_ref[POOL_HALO:, cols]
        cnt = jnp.minimum(pos + 1, w).astype(F32)
        dev = (acc / cnt - cur).astype(BF16)
        y = jnp.dot(dev, pw_ref[g], preferred_element_type=F32)
        yb_ref[:, cols] = (y * ps_ref[:, cols]).astype(BF16)

    bb = jnp.dot(yb_ref[...], pb_ref[...], preferred_element_type=F32)
    gate_a = jax.nn.sigmoid(ga_ref[...].astype(F32))
    gate_b = jax.nn.sigmoid(gb_ref[...].astype(F32))
    merged = gate_a * a_ref[...] + gate_b * bb
    y = jnp.dot(merged.astype(BF16), wo_ref[...], preferred_element_type=F32)
    o_ref[...] = x_ref[...] + y


def _mix(ya, u, gates, x2d, pool_w, pool_scale, w_a, w_b, w_o, *, seq, tm):
    n, d = x2d.shape
    tiles_per_seq = seq // tm
    halo_blocks = tm // POOL_HALO
    const = lambda *shape: pl.BlockSpec(shape, lambda i: (0,) * len(shape),
                                        pipeline_mode=pl.Buffered(1))
    return pl.pallas_call(
        functools.partial(_mix_body, tiles_per_seq=tiles_per_seq),
        grid=(n // tm,),
        in_specs=[
            pl.BlockSpec((tm, ATTN_WIDTH), lambda i: (i, 0)),
            pl.BlockSpec((tm, POOL_WIDTH), lambda i: (i, 0)),
            pl.BlockSpec((POOL_HALO, POOL_WIDTH),
                         lambda i: (jnp.maximum(i * halo_blocks - 1, 0), 0)),
            pl.BlockSpec((tm, d), lambda i: (i, 0)),
            pl.BlockSpec((tm, d), lambda i: (i, 1)),
            pl.BlockSpec((tm, d), lambda i: (i, 0)),
            const(len(POOL_WINDOWS), POOL_GROUP, POOL_GROUP),
            const(1, POOL_WIDTH),
            const(ATTN_WIDTH, d),
            const(POOL_WIDTH, d),
            const(d, d),
        ],
        out_specs=pl.BlockSpec((tm, d), lambda i: (i, 0)),
        out_shape=jax.ShapeDtypeStruct((n, d), F32),
        scratch_shapes=[
            pltpu.VMEM((POOL_HALO + tm, POOL_WIDTH), F32),
            pltpu.VMEM((tm, POOL_WIDTH), BF16),
            pltpu.VMEM((tm, d), F32),
        ],
        compiler_params=_compiler_params(("arbitrary",)),
        name="mix",
    )(ya, u, u, gates, gates, x2d, pool_w, pool_scale, w_a, w_b, w_o)


def _ffn_body(x_ref, g_ref, wg_ref, wu_ref, wd_ref, gf_ref, o_ref, h_ref):
    j = pl.program_id(1)

    @pl.when(j == 0)
    def _():
        x = x_ref[...]
        h_ref[...] = _rms(x, g_ref[...]).astype(BF16)
        o_ref[...] = x

    h = h_ref[...]
    gate = jnp.dot(h, wg_ref[...], preferred_element_type=F32)
    up = jnp.dot(h, wu_ref[...], preferred_element_type=F32)
    act = (jax.nn.silu(gate) * up).astype(BF16)
    o_ref[...] += jnp.dot(act, wd_ref[...], preferred_element_type=F32)

    @pl.when(j == pl.num_programs(1) - 1)
    def _():
        o_ref[...] = _rms(o_ref[...], gf_ref[...])


def _ffn(x2d, g, w_gate_up, w_down, g_final, *, tm, tf):
    n, d = x2d.shape
    d_ff = w_down.shape[0]
    n_f = d_ff // tf
    return pl.pallas_call(
        _ffn_body,
        grid=(n // tm, n_f),
        in_specs=[
            pl.BlockSpec((tm, d), lambda i, j: (i, 0)),
            pl.BlockSpec((1, d), lambda i, j: (0, 0)),
            pl.BlockSpec((d, tf), lambda i, j: (0, j)),
            pl.BlockSpec((d, tf), lambda i, j: (0, j + n_f)),
            pl.BlockSpec((tf, d), lambda i, j: (j, 0)),
            pl.BlockSpec((1, d), lambda i, j: (0, 0)),
        ],
        out_specs=pl.BlockSpec((tm, d), lambda i, j: (i, 0)),
        out_shape=jax.ShapeDtypeStruct((n, d), F32),
        scratch_shapes=[pltpu.VMEM((tm, d), BF16)],
        compiler_params=_compiler_params(("arbitrary", "arbitrary")),
        name="ffn",
    )(x2d, g, w_gate_up, w_gate_up, w_down, g_final)


def _layer(x2d, norm_mix, w_in, rel_bias, pool_w, pool_scale, w_a, w_b, w_o,
           norm_ffn, w_gate_up, w_down, norm_out, *, batch, seq):
    d = x2d.shape[1]
    qkv, u, gates = _in_proj(x2d, norm_mix.reshape(1, d), w_in.astype(BF16), tm=1024, tn=1024)
    ya = _attention(qkv, _bias_base(rel_bias), batch=batch, seq=seq)
    x1 = _mix(ya, u, gates, x2d, pool_w.astype(BF16), pool_scale.reshape(1, -1),
              w_a.astype(BF16), w_b.astype(BF16), w_o.astype(BF16), seq=seq, tm=256)
    return _ffn(x1, norm_ffn.reshape(1, d), w_gate_up.astype(BF16),
                w_down.astype(BF16), norm_out.reshape(1, d), tm=1024, tf=512)


def kernel(x, norm_mix, w_in, rel_bias, pool_w, pool_scale, w_branch_a, w_branch_b,
           w_out, norm_ffn, w_gate_up, w_down, norm_final):
    batch, seq, d = x.shape
    depth = w_in.shape[0]
    assert depth == 1, "the fused ffn kernel applies the final norm to the only layer"
    out = _layer(x.reshape(batch * seq, d), norm_mix[0], w_in[0], rel_bias[0],
                 pool_w[0], pool_scale[0], w_branch_a[0], w_branch_b[0], w_out[0],
                 norm_ffn[0], w_gate_up[0], w_down[0], norm_final,
                 batch=batch, seq=seq)
    return out.reshape(batch, seq, d)
```

```python
import functools
import math

import jax
import jax.numpy as jnp
from jax import lax
from jax.experimental import pallas as pl
from jax.experimental.pallas import tpu as pltpu

F32 = jnp.float32
BF16 = jnp.bfloat16

CHUNK = 64
LEFT_CHUNKS = 8
N_HEADS = 16
HEAD_DIM = 64
ATTN_WIDTH = N_HEADS * HEAD_DIM
MAX_REL = 128
POOL_WINDOWS = (2, 4, 8, 16)
POOL_GROUP = 256
POOL_WIDTH = POOL_GROUP * len(POOL_WINDOWS)
EPS = 1e-6
NEG_INF = -1e30

Q_ROWS = 2 * CHUNK
BAND = (LEFT_CHUNKS + 2) * CHUNK
PAD_ROWS = LEFT_CHUNKS * CHUNK
BIAS_EXT = BAND + Q_ROWS
POOL_HALO = 16

VMEM_LIMIT_BYTES = 60 * 1024 * 1024


def _rms(x, g):
    ms = jnp.mean(x * x, axis=-1, keepdims=True)
    return x * lax.rsqrt(ms + EPS) * g


def _compiler_params(semantics):
    return pltpu.CompilerParams(
        dimension_semantics=semantics, vmem_limit_bytes=VMEM_LIMIT_BYTES)


def _in_proj_body(x_ref, g_ref, w_ref, qkv_ref, u_ref, gate_ref, h_ref, *,
                  n_qkv_tiles, n_u_tiles):
    j = pl.program_id(1)

    @pl.when(j == 0)
    def _():
        h_ref[...] = _rms(x_ref[...], g_ref[...]).astype(BF16)

    def z():
        return jnp.dot(h_ref[...], w_ref[...], preferred_element_type=F32)

    @pl.when(j < n_qkv_tiles)
    def _():
        qkv_ref[...] = z().astype(BF16)

    @pl.when((j >= n_qkv_tiles) & (j < n_qkv_tiles + n_u_tiles))
    def _():
        u_ref[...] = z()

    @pl.when(j >= n_qkv_tiles + n_u_tiles)
    def _():
        gate_ref[...] = z().astype(BF16)


def _in_proj(x2d, g, w, *, tm, tn):
    n, d = x2d.shape
    in_width = w.shape[1]
    n_qkv_tiles = 3 * ATTN_WIDTH // tn
    n_u_tiles = POOL_WIDTH // tn
    n_gate = in_width - 3 * ATTN_WIDTH - POOL_WIDTH
    grid = (n // tm, in_width // tn)
    return pl.pallas_call(
        functools.partial(_in_proj_body, n_qkv_tiles=n_qkv_tiles, n_u_tiles=n_u_tiles),
        grid=grid,
        in_specs=[
            pl.BlockSpec((tm, d), lambda i, j: (i, 0)),
            pl.BlockSpec((1, d), lambda i, j: (0, 0)),
            pl.BlockSpec((d, tn), lambda i, j: (0, j)),
        ],
        out_specs=[
            pl.BlockSpec((tm, tn), lambda i, j: (i, jnp.minimum(j, n_qkv_tiles - 1))),
            pl.BlockSpec((tm, tn),
                         lambda i, j: (i, jnp.clip(j - n_qkv_tiles, 0, n_u_tiles - 1))),
            pl.BlockSpec((tm, tn),
                         lambda i, j: (i, jnp.maximum(j - n_qkv_tiles - n_u_tiles, 0))),
        ],
        out_shape=[
            jax.ShapeDtypeStruct((n, 3 * ATTN_WIDTH), BF16),
            jax.ShapeDtypeStruct((n, POOL_WIDTH), F32),
            jax.ShapeDtypeStruct((n, n_gate), BF16),
        ],
        scratch_shapes=[pltpu.VMEM((tm, d), BF16)],
        compiler_params=_compiler_params(("arbitrary", "arbitrary")),
        name="in_proj",
    )(x2d, g, w)


PAIR = 2 * HEAD_DIM
N_PAIRS = N_HEADS // 2
WIN_BLOCKS = BAND // Q_ROWS
PAD_BLOCKS = PAD_ROWS // Q_ROWS
FLAT_END = PAD_ROWS - MAX_REL
BIAS_ROWS = CHUNK + BAND - FLAT_END
ONES_ROWS = 16
SUBS = 4
CAST_ROWS = 16


def _attn_body(base_ref, q_ref, k_ref, v_ref, *rest, n_cast):
    cast_in = rest[:n_cast]
    o_ref = rest[n_cast]
    cast_out = rest[n_cast + 1:2 * n_cast + 1]
    kpad, vtpad, bias_ref, s_ref, p_ref = rest[2 * n_cast + 1:]
    b = pl.program_id(0)
    p = pl.program_id(1)

    @pl.when((b == 0) & (p == 0))
    def _():
        key = lax.broadcasted_iota(jnp.int32, (BAND, 2 * Q_ROWS), 0)
        qry = lax.broadcasted_iota(jnp.int32, (BAND, 2 * Q_ROWS), 1) % Q_ROWS
        lo = (qry // CHUNK) * CHUNK
        in_band = (key >= lo) & (key < lo + (LEFT_CHUNKS + 1) * CHUNK)
        for hp in range(N_PAIRS):
            halves = []
            for a in range(2):
                h = 2 * hp + a
                base = jnp.broadcast_to(base_ref[h:h + 1, :], (BAND, BIAS_EXT))
                rot = pltpu.roll(base, 0, 1, stride=1, stride_axis=0)
                halves.append(rot[:, :Q_ROWS] - base[:, 0:1])
            full = jnp.where(in_band, jnp.concatenate(halves, axis=1), NEG_INF)
            bias_ref[hp, 0:CHUNK] = full[0:CHUNK]
            bias_ref[hp, CHUNK:] = full[FLAT_END:]
        kpad[0:PAD_ROWS, :] = jnp.zeros((PAD_ROWS, ATTN_WIDTH), BF16)
        vtpad[0:PAD_BLOCKS, :, 0:PAIR, :] = jnp.zeros(
            (PAD_BLOCKS, N_PAIRS, PAIR, Q_ROWS), BF16)
        vtpad[:, :, PAIR:, :] = jnp.ones(
            (vtpad.shape[0], N_PAIRS, ONES_ROWS, Q_ROWS), BF16)

    @pl.when(p == 0)
    def _():
        kpad[PAD_ROWS:, :] = k_ref[...]
        for c in range(v_ref.shape[0] // Q_ROWS):
            blk_t = v_ref[c * Q_ROWS:(c + 1) * Q_ROWS, :].astype(F32).T.astype(BF16)
            for hp in range(N_PAIRS):
                vtpad[PAD_BLOCKS + c, hp, 0:PAIR, :] = blk_t[hp * PAIR:(hp + 1) * PAIR, :]

    lane = lax.broadcasted_iota(jnp.int32, (Q_ROWS, PAIR), 1)
    first_head = lane < HEAD_DIM
    row = lax.broadcasted_iota(jnp.int32, (PAIR, Q_ROWS), 0)
    first_head_rows = row < HEAD_DIM
    scale = jnp.asarray(1.0 / math.sqrt(HEAD_DIM), BF16)

    def step(mask_missing_keys):
        units = [(sub, hp) for sub in range(SUBS) for hp in range(N_PAIRS)]

        def scores(u):
            sub, hp = units[u]
            blk = p * SUBS + sub
            lanes = slice(hp * PAIR, (hp + 1) * PAIR)
            q2 = q_ref[sub * Q_ROWS:(sub + 1) * Q_ROWS, lanes] * scale
            zero = jnp.zeros_like(q2)
            qm = jnp.concatenate([jnp.where(first_head, q2, zero),
                                  jnp.where(first_head, zero, q2)], axis=0)
            k2 = kpad[pl.ds(pl.multiple_of(blk * Q_ROWS, Q_ROWS), BAND), lanes]
            s = lax.dot_general(k2, qm, (((1,), (1,)), ((), ())),
                                preferred_element_type=F32)
            parts = [(0, CHUNK, s[0:CHUNK] + bias_ref[hp, 0:CHUNK]),
                     (CHUNK, FLAT_END, s[CHUNK:FLAT_END]),
                     (FLAT_END, BAND, s[FLAT_END:] + bias_ref[hp, CHUNK:])]
            for r0, r1, part in parts:
                if mask_missing_keys:
                    key = r0 + lax.broadcasted_iota(jnp.int32, part.shape, 0)
                    part = jnp.where(key >= PAD_ROWS - blk * Q_ROWS, part, NEG_INF)
                s_ref[u % 2, r0:r1] = part

        def weights(u):
            slot = u % 2
            m = jnp.max(s_ref[slot], axis=0, keepdims=True)
            p_ref[slot] = jnp.exp(s_ref[slot] - m).astype(BF16)

        def values(u):
            sub, hp = units[u]
            blk = p * SUBS + sub
            vt2 = jnp.concatenate(
                [vtpad[blk + t, hp] for t in range(WIN_BLOCKS)], axis=1)
            o_t = jnp.dot(vt2, p_ref[u % 2], preferred_element_type=F32)
            o_t = o_t[0:PAIR] / o_t[PAIR:PAIR + 1]
            own = jnp.where(first_head_rows, o_t[:, :Q_ROWS], o_t[:, Q_ROWS:])
            o_ref[sub * Q_ROWS:(sub + 1) * Q_ROWS, hp * PAIR:(hp + 1) * PAIR] = (
                own.T.astype(BF16))

        casts = [(src, dst, r0) for src, dst in zip(cast_in, cast_out)
                 for r0 in range(0, src.shape[0], CAST_ROWS)]
        casts_per_unit = pl.cdiv(len(casts), len(units))

        scores(0)
        for u in range(len(units) + 1):
            if u + 1 < len(units):
                scores(u + 1)
            if u >= 1:
                values(u - 1)
            if u < len(units):
                weights(u)
                for src, dst, r0 in casts[u * casts_per_unit:(u + 1) * casts_per_unit]:
                    dst[r0:r0 + CAST_ROWS, :] = src[r0:r0 + CAST_ROWS, :].astype(BF16)

    @pl.when(p < PAD_BLOCKS // SUBS)
    def _():
        step(True)

    @pl.when(p >= PAD_BLOCKS // SUBS)
    def _():
        step(False)


def _attention(qkv, bias_base, cast_weights, *, batch, seq):
    n = batch * seq
    rows = SUBS * Q_ROWS
    steps = seq // rows
    n_steps = batch * steps
    cast_blocks = []
    for w in cast_weights:
        assert w.ndim == 2 and w.shape[0] % (n_steps * CAST_ROWS) == 0, w.shape
        cast_blocks.append(pl.BlockSpec((w.shape[0] // n_steps, w.shape[1]),
                                        lambda b, p: (b * steps + p, 0)))
    outs = pl.pallas_call(
        functools.partial(_attn_body, n_cast=len(cast_weights)),
        grid=(batch, steps),
        in_specs=[
            pl.BlockSpec((N_HEADS, BIAS_EXT), lambda b, p: (0, 0)),
            pl.BlockSpec((rows, ATTN_WIDTH), lambda b, p: (b * steps + p, 0)),
            pl.BlockSpec((seq, ATTN_WIDTH), lambda b, p: (b, 1)),
            pl.BlockSpec((seq, ATTN_WIDTH), lambda b, p: (b, 2)),
        ] + cast_blocks,
        out_specs=[pl.BlockSpec((rows, ATTN_WIDTH), lambda b, p: (b * steps + p, 0))]
        + cast_blocks,
        out_shape=[jax.ShapeDtypeStruct((n, ATTN_WIDTH), BF16)]
        + [jax.ShapeDtypeStruct(w.shape, BF16) for w in cast_weights],
        scratch_shapes=[
            pltpu.VMEM((PAD_ROWS + seq, ATTN_WIDTH), BF16),
            pltpu.VMEM((PAD_BLOCKS + seq // Q_ROWS, N_PAIRS, PAIR + ONES_ROWS, Q_ROWS), BF16),
            pltpu.VMEM((N_PAIRS, BIAS_ROWS, 2 * Q_ROWS), F32),
            pltpu.VMEM((2, BAND, 2 * Q_ROWS), F32),
            pltpu.VMEM((2, BAND, 2 * Q_ROWS), BF16),
        ],
        compiler_params=_compiler_params(("arbitrary", "arbitrary")),
        name="attn",
    )(bias_base, qkv, qkv, qkv, *cast_weights)
    return outs[0], outs[1:]


def _bias_base(rel_bias):
    far = rel_bias[:, -1:]
    head = jnp.broadcast_to(far, (N_HEADS, Q_ROWS))
    tail = jnp.broadcast_to(far, (N_HEADS, BIAS_EXT - Q_ROWS - 2 * MAX_REL - 1))
    return jnp.concatenate([head, rel_bias, tail], axis=1)


def _mix_body(ya_ref, u_ref, uprev_ref, ga_ref, gb_ref, x_ref, pw_ref, ps_ref,
              pa_ref, pb_ref, wo_ref, o_ref, ext_ref, yb_ref, a_ref, *, tiles_per_seq):
    i = pl.program_id(0)
    tm, d = x_ref.shape
    tile_in_seq = i % tiles_per_seq
    n_groups = len(POOL_WINDOWS)
    a_cols = d // n_groups

    halo = jnp.where(tile_in_seq == 0, jnp.zeros_like(uprev_ref[...]), uprev_ref[...])
    ext_ref[0:POOL_HALO, :] = halo
    ext_ref[POOL_HALO:, :] = u_ref[...]

    pos = tile_in_seq * tm + lax.broadcasted_iota(jnp.int32, (tm, 1), 0)
    for g, w in enumerate(POOL_WINDOWS):
        acols = slice(g * a_cols, (g + 1) * a_cols)
        a_ref[:, acols] = jnp.dot(ya_ref[...], pa_ref[:, acols],
                                  preferred_element_type=F32)
        cols = slice(g * POOL_GROUP, (g + 1) * POOL_GROUP)
        acc = ext_ref[:, cols]
        shift = 1
        while shift < w:
            acc = acc + pltpu.roll(acc, shift, 0)
            shift *= 2
        acc = acc[POOL_HALO:]
        cur = ext_ref[POOL_HALO:, cols]
        cnt = jnp.minimum(pos + 1, w).astype(F32)
        dev = (acc / cnt - cur).astype(BF16)
        y = jnp.dot(dev, pw_ref[g], preferred_element_type=F32)
        yb_ref[:, cols] = (y * ps_ref[:, cols]).astype(BF16)

    bb = jnp.dot(yb_ref[...], pb_ref[...], preferred_element_type=F32)
    gate_a = jax.nn.sigmoid(ga_ref[...].astype(F32))
    gate_b = jax.nn.sigmoid(gb_ref[...].astype(F32))
    merged = gate_a * a_ref[...] + gate_b * bb
    y = jnp.dot(merged.astype(BF16), wo_ref[...], preferred_element_type=F32)
    o_ref[...] = x_ref[...] + y


def _mix(ya, u, gates, x2d, pool_w, pool_scale, w_a, w_b, w_o, *, seq, tm):
    n, d = x2d.shape
    tiles_per_seq = seq // tm
    halo_blocks = tm // POOL_HALO
    const = lambda *shape: pl.BlockSpec(shape, lambda i: (0,) * len(shape),
                                        pipeline_mode=pl.Buffered(1))
    return pl.pallas_call(
        functools.partial(_mix_body, tiles_per_seq=tiles_per_seq),
        grid=(n // tm,),
        in_specs=[
            pl.BlockSpec((tm, ATTN_WIDTH), lambda i: (i, 0)),
            pl.BlockSpec((tm, POOL_WIDTH), lambda i: (i, 0)),
            pl.BlockSpec((POOL_HALO, POOL_WIDTH),
                         lambda i: (jnp.maximum(i * halo_blocks - 1, 0), 0)),
            pl.BlockSpec((tm, d), lambda i: (i, 0)),
            pl.BlockSpec((tm, d), lambda i: (i, 1)),
            pl.BlockSpec((tm, d), lambda i: (i, 0)),
            const(len(POOL_WINDOWS), POOL_GROUP, POOL_GROUP),
            const(1, POOL_WIDTH),
            const(ATTN_WIDTH, d),
            const(POOL_WIDTH, d),
            const(d, d),
        ],
        out_specs=pl.BlockSpec((tm, d), lambda i: (i, 0)),
        out_shape=jax.ShapeDtypeStruct((n, d), F32),
        scratch_shapes=[
            pltpu.VMEM((POOL_HALO + tm, POOL_WIDTH), F32),
            pltpu.VMEM((tm, POOL_WIDTH), BF16),
            pltpu.VMEM((tm, d), F32),
        ],
        compiler_params=_compiler_params(("arbitrary",)),
        name="mix",
    )(ya, u, u, gates, gates, x2d, pool_w, pool_scale, w_a, w_b, w_o)


def _ffn_body(x_ref, g_ref, wg_ref, wu_ref, wd_ref, gf_ref, o_ref, h_ref):
    j = pl.program_id(1)

    @pl.when(j == 0)
    def _():
        x = x_ref[...]
        h_ref[...] = _rms(x, g_ref[...]).astype(BF16)
        o_ref[...] = x

    h = h_ref[...]
    gate = jnp.dot(h, wg_ref[...], preferred_element_type=F32)
    up = jnp.dot(h, wu_ref[...], preferred_element_type=F32)
    act = (jax.nn.silu(gate) * up).astype(BF16)
    o_ref[...] += jnp.dot(act, wd_ref[...], preferred_element_type=F32)

    @pl.when(j == pl.num_programs(1) - 1)
    def _():
        o_ref[...] = _rms(o_ref[...], gf_ref[...])


def _ffn(x2d, g, w_gate_up, w_down, g_final, *, tm, tf):
    n, d = x2d.shape
    d_ff = w_down.shape[0]
    n_f = d_ff // tf
    return pl.pallas_call(
        _ffn_body,
        grid=(n // tm, n_f),
        in_specs=[
            pl.BlockSpec((tm, d), lambda i, j: (i, 0)),
            pl.BlockSpec((1, d), lambda i, j: (0, 0)),
            pl.BlockSpec((d, tf), lambda i, j: (0, j)),
            pl.BlockSpec((d, tf), lambda i, j: (0, j + n_f)),
            pl.BlockSpec((tf, d), lambda i, j: (j, 0)),
            pl.BlockSpec((1, d), lambda i, j: (0, 0)),
        ],
        out_specs=pl.BlockSpec((tm, d), lambda i, j: (i, 0)),
        out_shape=jax.ShapeDtypeStruct((n, d), F32),
        scratch_shapes=[pltpu.VMEM((tm, d), BF16)],
        compiler_params=_compiler_params(("arbitrary", "arbitrary")),
        name="ffn",
    )(x2d, g, w_gate_up, w_gate_up, w_down, g_final)


def _layer(x2d, norm_mix, w_in, rel_bias, pool_w, pool_scale, w_a, w_b, w_o,
           norm_ffn, w_gate_up, w_down, norm_out, *, batch, seq):
    d = x2d.shape[1]
    qkv, u, gates = _in_proj(x2d, norm_mix.reshape(1, d), w_in.astype(BF16), tm=1024, tn=1024)
    ya, (w_a, w_b, w_o, w_gate_up, w_down) = _attention(
        qkv, _bias_base(rel_bias), (w_a, w_b, w_o, w_gate_up, w_down),
        batch=batch, seq=seq)
    x1 = _mix(ya, u, gates, x2d, pool_w.astype(BF16), pool_scale.reshape(1, -1),
              w_a, w_b, w_o, seq=seq, tm=256)
    return _ffn(x1, norm_ffn.reshape(1, d), w_gate_up, w_down,
                norm_out.reshape(1, d), tm=1024, tf=512)


def kernel(x, norm_mix, w_in, rel_bias, pool_w, pool_scale, w_branch_a, w_branch_b,
           w_out, norm_ffn, w_gate_up, w_down, norm_final):
    batch, seq, d = x.shape
    depth = w_in.shape[0]
    assert depth == 1, "the fused ffn kernel applies the final norm to the only layer"
    out = _layer(x.reshape(batch * seq, d), norm_mix[0], w_in[0], rel_bias[0],
                 pool_w[0], pool_scale[0], w_branch_a[0], w_branch_b[0], w_out[0],
                 norm_ffn[0], w_gate_up[0], w_down[0], norm_final,
                 batch=batch, seq=seq)
    return out.reshape(batch, seq, d)
```

```python
import functools
import math

import jax
import jax.numpy as jnp
from jax import lax
from jax.experimental import pallas as pl
from jax.experimental.pallas import tpu as pltpu

F32 = jnp.float32
BF16 = jnp.bfloat16

CHUNK = 64
LEFT_CHUNKS = 8
N_HEADS = 16
HEAD_DIM = 64
ATTN_WIDTH = N_HEADS * HEAD_DIM
MAX_REL = 128
POOL_WINDOWS = (2, 4, 8, 16)
POOL_GROUP = 256
POOL_WIDTH = POOL_GROUP * len(POOL_WINDOWS)
EPS = 1e-6
NEG_INF = -1e30

Q_ROWS = 2 * CHUNK
BAND = (LEFT_CHUNKS + 2) * CHUNK
PAD_ROWS = LEFT_CHUNKS * CHUNK
BIAS_EXT = BAND + Q_ROWS
POOL_HALO = 16

VMEM_LIMIT_BYTES = 60 * 1024 * 1024


def _rms(x, g):
    ms = jnp.mean(x * x, axis=-1, keepdims=True)
    return x * lax.rsqrt(ms + EPS) * g


def _compiler_params(semantics):
    return pltpu.CompilerParams(
        dimension_semantics=semantics, vmem_limit_bytes=VMEM_LIMIT_BYTES)


def _in_proj_body(x_ref, g_ref, w_ref, qkv_ref, u_ref, gate_ref):
    h = _rms(x_ref[...], g_ref[...]).astype(BF16)
    q_end = qkv_ref.shape[1]
    u_end = q_end + u_ref.shape[1]
    qkv_ref[...] = jnp.dot(h, w_ref[:, :q_end], preferred_element_type=F32).astype(BF16)
    u_ref[...] = jnp.dot(h, w_ref[:, q_end:u_end], preferred_element_type=F32)
    gate_ref[...] = jnp.dot(h, w_ref[:, u_end:], preferred_element_type=F32).astype(BF16)


def _in_proj(x2d, g, w, *, tm):
    n, d = x2d.shape
    in_width = w.shape[1]
    n_gate = in_width - 3 * ATTN_WIDTH - POOL_WIDTH
    return pl.pallas_call(
        _in_proj_body,
        grid=(n // tm,),
        in_specs=[
            pl.BlockSpec((tm, d), lambda i: (i, 0)),
            pl.BlockSpec((1, d), lambda i: (0, 0)),
            pl.BlockSpec((d, in_width), lambda i: (0, 0), pipeline_mode=pl.Buffered(1)),
        ],
        out_specs=[
            pl.BlockSpec((tm, 3 * ATTN_WIDTH), lambda i: (i, 0)),
            pl.BlockSpec((tm, POOL_WIDTH), lambda i: (i, 0)),
            pl.BlockSpec((tm, n_gate), lambda i: (i, 0)),
        ],
        out_shape=[
            jax.ShapeDtypeStruct((n, 3 * ATTN_WIDTH), BF16),
            jax.ShapeDtypeStruct((n, POOL_WIDTH), F32),
            jax.ShapeDtypeStruct((n, n_gate), BF16),
        ],
        compiler_params=_compiler_params(("arbitrary",)),
        name="in_proj",
    )(x2d, g, w)


PAIR = 2 * HEAD_DIM
N_PAIRS = N_HEADS // 2
WIN_BLOCKS = BAND // Q_ROWS
PAD_BLOCKS = PAD_ROWS // Q_ROWS
FLAT_END = PAD_ROWS - MAX_REL
BIAS_ROWS = CHUNK + BAND - FLAT_END
ONES_ROWS = 16
SUBS = 4
CAST_ROWS = 16


def _attn_body(base_ref, q_ref, k_ref, v_ref, *rest, n_cast):
    cast_in = rest[:n_cast]
    o_ref = rest[n_cast]
    cast_out = rest[n_cast + 1:2 * n_cast + 1]
    kpad, vtpad, bias_ref, s_ref, p_ref = rest[2 * n_cast + 1:]
    b = pl.program_id(0)
    p = pl.program_id(1)

    @pl.when((b == 0) & (p == 0))
    def _():
        key = lax.broadcasted_iota(jnp.int32, (BAND, 2 * Q_ROWS), 0)
        qry = lax.broadcasted_iota(jnp.int32, (BAND, 2 * Q_ROWS), 1) % Q_ROWS
        lo = (qry // CHUNK) * CHUNK
        in_band = (key >= lo) & (key < lo + (LEFT_CHUNKS + 1) * CHUNK)
        for hp in range(N_PAIRS):
            halves = []
            for a in range(2):
                h = 2 * hp + a
                base = jnp.broadcast_to(base_ref[h:h + 1, :], (BAND, BIAS_EXT))
                rot = pltpu.roll(base, 0, 1, stride=1, stride_axis=0)
                halves.append(rot[:, :Q_ROWS] - base[:, 0:1])
            full = jnp.where(in_band, jnp.concatenate(halves, axis=1), NEG_INF)
            bias_ref[hp, 0:CHUNK] = full[0:CHUNK]
            bias_ref[hp, CHUNK:] = full[FLAT_END:]
        kpad[0:PAD_ROWS, :] = jnp.zeros((PAD_ROWS, ATTN_WIDTH), BF16)
        vtpad[0:PAD_BLOCKS, :, 0:PAIR, :] = jnp.zeros(
            (PAD_BLOCKS, N_PAIRS, PAIR, Q_ROWS), BF16)
        vtpad[:, :, PAIR:, :] = jnp.ones(
            (vtpad.shape[0], N_PAIRS, ONES_ROWS, Q_ROWS), BF16)

    @pl.when(p == 0)
    def _():
        kpad[PAD_ROWS:, :] = k_ref[...]
        for c in range(v_ref.shape[0] // Q_ROWS):
            blk_t = v_ref[c * Q_ROWS:(c + 1) * Q_ROWS, :].astype(F32).T.astype(BF16)
            for hp in range(N_PAIRS):
                vtpad[PAD_BLOCKS + c, hp, 0:PAIR, :] = blk_t[hp * PAIR:(hp + 1) * PAIR, :]

    lane = lax.broadcasted_iota(jnp.int32, (Q_ROWS, PAIR), 1)
    first_head = lane < HEAD_DIM
    row = lax.broadcasted_iota(jnp.int32, (PAIR, Q_ROWS), 0)
    first_head_rows = row < HEAD_DIM
    scale = jnp.asarray(1.0 / math.sqrt(HEAD_DIM), BF16)

    def step(mask_missing_keys):
        units = [(sub, hp) for sub in range(SUBS) for hp in range(N_PAIRS)]

        def scores(u):
            sub, hp = units[u]
            blk = p * SUBS + sub
            lanes = slice(hp * PAIR, (hp + 1) * PAIR)
            q2 = q_ref[sub * Q_ROWS:(sub + 1) * Q_ROWS, lanes] * scale
            zero = jnp.zeros_like(q2)
            qm = jnp.concatenate([jnp.where(first_head, q2, zero),
                                  jnp.where(first_head, zero, q2)], axis=0)
            k2 = kpad[pl.ds(pl.multiple_of(blk * Q_ROWS, Q_ROWS), BAND), lanes]
            s = lax.dot_general(k2, qm, (((1,), (1,)), ((), ())),
                                preferred_element_type=F32)
            parts = [(0, CHUNK, s[0:CHUNK] + bias_ref[hp, 0:CHUNK]),
                     (CHUNK, FLAT_END, s[CHUNK:FLAT_END]),
                     (FLAT_END, BAND, s[FLAT_END:] + bias_ref[hp, CHUNK:])]
            for r0, r1, part in parts:
                if mask_missing_keys:
                    key = r0 + lax.broadcasted_iota(jnp.int32, part.shape, 0)
                    part = jnp.where(key >= PAD_ROWS - blk * Q_ROWS, part, NEG_INF)
                s_ref[u % 2, r0:r1] = part

        def weights(u):
            slot = u % 2
            m = jnp.max(s_ref[slot], axis=0, keepdims=True)
            p_ref[slot] = jnp.exp(s_ref[slot] - m).astype(BF16)

        def values(u):
            sub, hp = units[u]
            blk = p * SUBS + sub
            vt2 = jnp.concatenate(
                [vtpad[blk + t, hp] for t in range(WIN_BLOCKS)], axis=1)
            o_t = jnp.dot(vt2, p_ref[u % 2], preferred_element_type=F32)
            o_t = o_t[0:PAIR] / o_t[PAIR:PAIR + 1]
            own = jnp.where(first_head_rows, o_t[:, :Q_ROWS], o_t[:, Q_ROWS:])
            o_ref[sub * Q_ROWS:(sub + 1) * Q_ROWS, hp * PAIR:(hp + 1) * PAIR] = (
                own.T.astype(BF16))

        casts = [(src, dst, r0) for src, dst in zip(cast_in, cast_out)
                 for r0 in range(0, src.shape[0], CAST_ROWS)]
        casts_per_unit = pl.cdiv(len(casts), len(units))

        scores(0)
        for u in range(len(units) + 1):
            if u + 1 < len(units):
                scores(u + 1)
            if u >= 1:
                values(u - 1)
            if u < len(units):
                weights(u)
                for src, dst, r0 in casts[u * casts_per_unit:(u + 1) * casts_per_unit]:
                    dst[r0:r0 + CAST_ROWS, :] = src[r0:r0 + CAST_ROWS, :].astype(BF16)

    @pl.when(p < PAD_BLOCKS // SUBS)
    def _():
        step(True)

    @pl.when(p >= PAD_BLOCKS // SUBS)
    def _():
        step(False)


def _attention(qkv, bias_base, cast_weights, *, batch, seq):
    n = batch * seq
    rows = SUBS * Q_ROWS
    steps = seq // rows
    n_steps = batch * steps
    cast_blocks = []
    for w in cast_weights:
        assert w.ndim == 2 and w.shape[0] % (n_steps * CAST_ROWS) == 0, w.shape
        cast_blocks.append(pl.BlockSpec((w.shape[0] // n_steps, w.shape[1]),
                                        lambda b, p: (b * steps + p, 0)))
    outs = pl.pallas_call(
        functools.partial(_attn_body, n_cast=len(cast_weights)),
        grid=(batch, steps),
        in_specs=[
            pl.BlockSpec((N_HEADS, BIAS_EXT), lambda b, p: (0, 0)),
            pl.BlockSpec((rows, ATTN_WIDTH), lambda b, p: (b * steps + p, 0)),
            pl.BlockSpec((seq, ATTN_WIDTH), lambda b, p: (b, 1)),
            pl.BlockSpec((seq, ATTN_WIDTH), lambda b, p: (b, 2)),
        ] + cast_blocks,
        out_specs=[pl.BlockSpec((rows, ATTN_WIDTH), lambda b, p: (b * steps + p, 0))]
        + cast_blocks,
        out_shape=[jax.ShapeDtypeStruct((n, ATTN_WIDTH), BF16)]
        + [jax.ShapeDtypeStruct(w.shape, BF16) for w in cast_weights],
        scratch_shapes=[
            pltpu.VMEM((PAD_ROWS + seq, ATTN_WIDTH), BF16),
            pltpu.VMEM((PAD_BLOCKS + seq // Q_ROWS, N_PAIRS, PAIR + ONES_ROWS, Q_ROWS), BF16),
            pltpu.VMEM((N_PAIRS, BIAS_ROWS, 2 * Q_ROWS), F32),
            pltpu.VMEM((2, BAND, 2 * Q_ROWS), F32),
            pltpu.VMEM((2, BAND, 2 * Q_ROWS), BF16),
        ],
        compiler_params=_compiler_params(("arbitrary", "arbitrary")),
        name="attn",
    )(bias_base, qkv, qkv, qkv, *cast_weights)
    return outs[0], outs[1:]


def _bias_base(rel_bias):
    far = rel_bias[:, -1:]
    head = jnp.broadcast_to(far, (N_HEADS, Q_ROWS))
    tail = jnp.broadcast_to(far, (N_HEADS, BIAS_EXT - Q_ROWS - 2 * MAX_REL - 1))
    return jnp.concatenate([head, rel_bias, tail], axis=1)


def _mix_body(ya_ref, u_ref, uprev_ref, ga_ref, gb_ref, x_ref, pw_ref, ps_ref,
              pa_ref, pb_ref, wo_ref, o_ref, ext_ref, yb_ref, a_ref, *, tiles_per_seq):
    i = pl.program_id(0)
    tm, d = x_ref.shape
    tile_in_seq = i % tiles_per_seq
    n_groups = len(POOL_WINDOWS)
    a_cols = d // n_groups

    halo = jnp.where(tile_in_seq == 0, jnp.zeros_like(uprev_ref[...]), uprev_ref[...])
    ext_ref[0:POOL_HALO, :] = halo
    ext_ref[POOL_HALO:, :] = u_ref[...]

    pos = tile_in_seq * tm + lax.broadcasted_iota(jnp.int32, (tm, 1), 0)
    for g, w in enumerate(POOL_WINDOWS):
        acols = slice(g * a_cols, (g + 1) * a_cols)
        a_ref[:, acols] = jnp.dot(ya_ref[...], pa_ref[:, acols],
                                  preferred_element_type=F32)
        cols = slice(g * POOL_GROUP, (g + 1) * POOL_GROUP)
        acc = ext_ref[:, cols]
        shift = 1
        while shift < w:
            acc = acc + pltpu.roll(acc, shift, 0)
            shift *= 2
        acc = acc[POOL_HALO:]
        cur = ext_ref[POOL_HALO:, cols]
        cnt = jnp.minimum(pos + 1, w).astype(F32)
        dev = (acc / cnt - cur).astype(BF16)
        y = jnp.dot(dev, pw_ref[g], preferred_element_type=F32)
        yb_ref[:, cols] = (y * ps_ref[:, cols]).astype(BF16)

    bb = jnp.dot(yb_ref[...], pb_ref[...], preferred_element_type=F32)
    gate_a = jax.nn.sigmoid(ga_ref[...].astype(F32))
    gate_b = jax.nn.sigmoid(gb_ref[...].astype(F32))
    merged = gate_a * a_ref[...] + gate_b * bb
    y = jnp.dot(merged.astype(BF16), wo_ref[...], preferred_element_type=F32)
    o_ref[...] = x_ref[...] + y


def _mix(ya, u, gates, x2d, pool_w, pool_scale, w_a, w_b, w_o, *, seq, tm):
    n, d = x2d.shape
    tiles_per_seq = seq // tm
    halo_blocks = tm // POOL_HALO
    const = lambda *shape: pl.BlockSpec(shape, lambda i: (0,) * len(shape),
                                        pipeline_mode=pl.Buffered(1))
    return pl.pallas_call(
        functools.partial(_mix_body, tiles_per_seq=tiles_per_seq),
        grid=(n // tm,),
        in_specs=[
            pl.BlockSpec((tm, ATTN_WIDTH), lambda i: (i, 0)),
            pl.BlockSpec((tm, POOL_WIDTH), lambda i: (i, 0)),
            pl.BlockSpec((POOL_HALO, POOL_WIDTH),
                         lambda i: (jnp.maximum(i * halo_blocks - 1, 0), 0)),
            pl.BlockSpec((tm, d), lambda i: (i, 0)),
            pl.BlockSpec((tm, d), lambda i: (i, 1)),
            pl.BlockSpec((tm, d), lambda i: (i, 0)),
            const(len(POOL_WINDOWS), POOL_GROUP, POOL_GROUP),
            const(1, POOL_WIDTH),
            const(ATTN_WIDTH, d),
            const(POOL_WIDTH, d),
            const(d, d),
        ],
        out_specs=pl.BlockSpec((tm, d), lambda i: (i, 0)),
        out_shape=jax.ShapeDtypeStruct((n, d), F32),
        scratch_shapes=[
            pltpu.VMEM((POOL_HALO + tm, POOL_WIDTH), F32),
            pltpu.VMEM((tm, POOL_WIDTH), BF16),
            pltpu.VMEM((tm, d), F32),
        ],
        compiler_params=_compiler_params(("arbitrary",)),
        name="mix",
    )(ya, u, u, gates, gates, x2d, pool_w, pool_scale, w_a, w_b, w_o)


def _ffn_body(x_ref, g_ref, wg_ref, wu_ref, wd_ref, gf_ref, o_ref, h_ref):
    j = pl.program_id(1)

    @pl.when(j == 0)
    def _():
        x = x_ref[...]
        h_ref[...] = _rms(x, g_ref[...]).astype(BF16)
        o_ref[...] = x

    h = h_ref[...]
    gate = jnp.dot(h, wg_ref[...], preferred_element_type=F32)
    up = jnp.dot(h, wu_ref[...], preferred_element_type=F32)
    act = (jax.nn.silu(gate) * up).astype(BF16)
    o_ref[...] += jnp.dot(act, wd_ref[...], preferred_element_type=F32)

    @pl.when(j == pl.num_programs(1) - 1)
    def _():
        o_ref[...] = _rms(o_ref[...], gf_ref[...])


def _ffn(x2d, g, w_gate_up, w_down, g_final, *, tm, tf):
    n, d = x2d.shape
    d_ff = w_down.shape[0]
    n_f = d_ff // tf
    return pl.pallas_call(
        _ffn_body,
        grid=(n // tm, n_f),
        in_specs=[
            pl.BlockSpec((tm, d), lambda i, j: (i, 0)),
            pl.BlockSpec((1, d), lambda i, j: (0, 0)),
            pl.BlockSpec((d, tf), lambda i, j: (0, j)),
            pl.BlockSpec((d, tf), lambda i, j: (0, j + n_f)),
            pl.BlockSpec((tf, d), lambda i, j: (j, 0)),
            pl.BlockSpec((1, d), lambda i, j: (0, 0)),
        ],
        out_specs=pl.BlockSpec((tm, d), lambda i, j: (i, 0)),
        out_shape=jax.ShapeDtypeStruct((n, d), F32),
        scratch_shapes=[pltpu.VMEM((tm, d), BF16)],
        compiler_params=_compiler_params(("arbitrary", "arbitrary")),
        name="ffn",
    )(x2d, g, w_gate_up, w_gate_up, w_down, g_final)


def _layer(x2d, norm_mix, w_in, rel_bias, pool_w, pool_scale, w_a, w_b, w_o,
           norm_ffn, w_gate_up, w_down, norm_out, *, batch, seq):
    d = x2d.shape[1]
    qkv, u, gates = _in_proj(x2d, norm_mix.reshape(1, d), w_in.astype(BF16), tm=256)
    ya, (w_a, w_b, w_o, w_gate_up, w_down) = _attention(
        qkv, _bias_base(rel_bias), (w_a, w_b, w_o, w_gate_up, w_down),
        batch=batch, seq=seq)
    x1 = _mix(ya, u, gates, x2d, pool_w.astype(BF16), pool_scale.reshape(1, -1),
              w_a, w_b, w_o, seq=seq, tm=256)
    return _ffn(x1, norm_ffn.reshape(1, d), w_gate_up, w_down,
                norm_out.reshape(1, d), tm=1024, tf=512)


def kernel(x, norm_mix, w_in, rel_bias, pool_w, pool_scale, w_branch_a, w_branch_b,
           w_out, norm_ffn, w_gate_up, w_down, norm_final):
    batch, seq, d = x.shape
    depth = w_in.shape[0]
    assert depth == 1, "the fused ffn kernel applies the final norm to the only layer"
    out = _layer(x.reshape(batch * seq, d), norm_mix[0], w_in[0], rel_bias[0],
                 pool_w[0], pool_scale[0], w_branch_a[0], w_branch_b[0], w_out[0],
                 norm_ffn[0], w_gate_up[0], w_down[0], norm_final,
                 batch=batch, seq=seq)
    return out.reshape(batch, seq, d)
```

```python
import functools
import math

import jax
import jax.numpy as jnp
from jax import lax
from jax.experimental import pallas as pl
from jax.experimental.pallas import tpu as pltpu

F32 = jnp.float32
BF16 = jnp.bfloat16

CHUNK = 64
LEFT_CHUNKS = 8
N_HEADS = 16
HEAD_DIM = 64
ATTN_WIDTH = N_HEADS * HEAD_DIM
MAX_REL = 128
POOL_WINDOWS = (2, 4, 8, 16)
POOL_GROUP = 256
POOL_WIDTH = POOL_GROUP * len(POOL_WINDOWS)
EPS = 1e-6
NEG_INF = -1e30

Q_ROWS = 2 * CHUNK
BAND = (LEFT_CHUNKS + 2) * CHUNK
PAD_ROWS = LEFT_CHUNKS * CHUNK
BIAS_EXT = BAND + Q_ROWS
POOL_HALO = 16
STAGE_ROWS = 64

VMEM_LIMIT_BYTES = 60 * 1024 * 1024


def _rms(x, g):
    ms = jnp.mean(x * x, axis=-1, keepdims=True)
    return x * lax.rsqrt(ms + EPS) * g


def _compiler_params(semantics):
    return pltpu.CompilerParams(
        dimension_semantics=semantics, vmem_limit_bytes=VMEM_LIMIT_BYTES)


def _in_proj_body(x_ref, g_ref, w_hbm, qkv_ref, u_ref, gate_ref, w_ref, stage_ref, sem):
    n_chunks = w_ref.shape[0] // STAGE_ROWS

    def chunk_copy(c, slot):
        rows = pl.ds(pl.multiple_of(c * STAGE_ROWS, STAGE_ROWS), STAGE_ROWS)
        return pltpu.make_async_copy(w_hbm.at[rows], stage_ref.at[slot], sem.at[slot])

    @pl.when(pl.program_id(0) == 0)
    def _():
        chunk_copy(0, 0).start()

        def two_chunks(cc, carry):
            for slot in range(2):
                c = 2 * cc + slot

                @pl.when(c + 1 < n_chunks)
                def _():
                    chunk_copy(c + 1, 1 - slot).start()

                chunk_copy(c, slot).wait()
                rows = pl.ds(pl.multiple_of(c * STAGE_ROWS, STAGE_ROWS), STAGE_ROWS)
                w_ref[rows, :] = stage_ref[slot].astype(BF16)
            return carry

        lax.fori_loop(0, n_chunks // 2, two_chunks, 0)

    h = _rms(x_ref[...], g_ref[...]).astype(BF16)
    q_end = qkv_ref.shape[1]
    u_end = q_end + u_ref.shape[1]
    qkv_ref[...] = jnp.dot(h, w_ref[:, :q_end], preferred_element_type=F32).astype(BF16)
    u_ref[...] = jnp.dot(h, w_ref[:, q_end:u_end], preferred_element_type=F32)
    gate_ref[...] = jnp.dot(h, w_ref[:, u_end:], preferred_element_type=F32).astype(BF16)


def _in_proj(x2d, g, w, *, tm):
    n, d = x2d.shape
    in_width = w.shape[1]
    n_gate = in_width - 3 * ATTN_WIDTH - POOL_WIDTH
    assert d % (2 * STAGE_ROWS) == 0
    return pl.pallas_call(
        _in_proj_body,
        grid=(n // tm,),
        in_specs=[
            pl.BlockSpec((tm, d), lambda i: (i, 0)),
            pl.BlockSpec((1, d), lambda i: (0, 0)),
            pl.BlockSpec(memory_space=pl.ANY),
        ],
        out_specs=[
            pl.BlockSpec((tm, 3 * ATTN_WIDTH), lambda i: (i, 0)),
            pl.BlockSpec((tm, POOL_WIDTH), lambda i: (i, 0)),
            pl.BlockSpec((tm, n_gate), lambda i: (i, 0)),
        ],
        out_shape=[
            jax.ShapeDtypeStruct((n, 3 * ATTN_WIDTH), BF16),
            jax.ShapeDtypeStruct((n, POOL_WIDTH), F32),
            jax.ShapeDtypeStruct((n, n_gate), BF16),
        ],
        scratch_shapes=[
            pltpu.VMEM((d, in_width), BF16),
            pltpu.VMEM((2, STAGE_ROWS, in_width), F32),
            pltpu.SemaphoreType.DMA((2,)),
        ],
        compiler_params=_compiler_params(("arbitrary",)),
        name="in_proj",
    )(x2d, g, w)


PAIR = 2 * HEAD_DIM
N_PAIRS = N_HEADS // 2
WIN_BLOCKS = BAND // Q_ROWS
PAD_BLOCKS = PAD_ROWS // Q_ROWS
FLAT_END = PAD_ROWS - MAX_REL
BIAS_ROWS = CHUNK + BAND - FLAT_END
ONES_ROWS = 16
SUBS = 4
CAST_ROWS = 16


def _attn_body(base_ref, q_ref, k_ref, v_ref, *rest, n_cast):
    cast_in = rest[:n_cast]
    o_ref = rest[n_cast]
    cast_out = rest[n_cast + 1:2 * n_cast + 1]
    kpad, vtpad, bias_ref, s_ref, p_ref = rest[2 * n_cast + 1:]
    b = pl.program_id(0)
    p = pl.program_id(1)

    @pl.when((b == 0) & (p == 0))
    def _():
        key = lax.broadcasted_iota(jnp.int32, (BAND, 2 * Q_ROWS), 0)
        qry = lax.broadcasted_iota(jnp.int32, (BAND, 2 * Q_ROWS), 1) % Q_ROWS
        lo = (qry // CHUNK) * CHUNK
        in_band = (key >= lo) & (key < lo + (LEFT_CHUNKS + 1) * CHUNK)
        for hp in range(N_PAIRS):
            halves = []
            for a in range(2):
                h = 2 * hp + a
                base = jnp.broadcast_to(base_ref[h:h + 1, :], (BAND, BIAS_EXT))
                rot = pltpu.roll(base, 0, 1, stride=1, stride_axis=0)
                halves.append(rot[:, :Q_ROWS] - base[:, 0:1])
            full = jnp.where(in_band, jnp.concatenate(halves, axis=1), NEG_INF)
            bias_ref[hp, 0:CHUNK] = full[0:CHUNK]
            bias_ref[hp, CHUNK:] = full[FLAT_END:]
        kpad[0:PAD_ROWS, :] = jnp.zeros((PAD_ROWS, ATTN_WIDTH), BF16)
        vtpad[0:PAD_BLOCKS, :, 0:PAIR, :] = jnp.zeros(
            (PAD_BLOCKS, N_PAIRS, PAIR, Q_ROWS), BF16)
        vtpad[:, :, PAIR:, :] = jnp.ones(
            (vtpad.shape[0], N_PAIRS, ONES_ROWS, Q_ROWS), BF16)

    @pl.when(p == 0)
    def _():
        kpad[PAD_ROWS:, :] = k_ref[...]
        for c in range(v_ref.shape[0] // Q_ROWS):
            blk_t = v_ref[c * Q_ROWS:(c + 1) * Q_ROWS, :].astype(F32).T.astype(BF16)
            for hp in range(N_PAIRS):
                vtpad[PAD_BLOCKS + c, hp, 0:PAIR, :] = blk_t[hp * PAIR:(hp + 1) * PAIR, :]

    lane = lax.broadcasted_iota(jnp.int32, (Q_ROWS, PAIR), 1)
    first_head = lane < HEAD_DIM
    row = lax.broadcasted_iota(jnp.int32, (PAIR, Q_ROWS), 0)
    first_head_rows = row < HEAD_DIM
    scale = jnp.asarray(1.0 / math.sqrt(HEAD_DIM), BF16)

    def step(mask_missing_keys):
        units = [(sub, hp) for sub in range(SUBS) for hp in range(N_PAIRS)]

        def scores(u):
            sub, hp = units[u]
            blk = p * SUBS + sub
            lanes = slice(hp * PAIR, (hp + 1) * PAIR)
            q2 = q_ref[sub * Q_ROWS:(sub + 1) * Q_ROWS, lanes] * scale
            zero = jnp.zeros_like(q2)
            qm = jnp.concatenate([jnp.where(first_head, q2, zero),
                                  jnp.where(first_head, zero, q2)], axis=0)
            k2 = kpad[pl.ds(pl.multiple_of(blk * Q_ROWS, Q_ROWS), BAND), lanes]
            s = lax.dot_general(k2, qm, (((1,), (1,)), ((), ())),
                                preferred_element_type=F32)
            parts = [(0, CHUNK, s[0:CHUNK] + bias_ref[hp, 0:CHUNK]),
                     (CHUNK, FLAT_END, s[CHUNK:FLAT_END]),
                     (FLAT_END, BAND, s[FLAT_END:] + bias_ref[hp, CHUNK:])]
            for r0, r1, part in parts:
                if mask_missing_keys:
                    key = r0 + lax.broadcasted_iota(jnp.int32, part.shape, 0)
                    part = jnp.where(key >= PAD_ROWS - blk * Q_ROWS, part, NEG_INF)
                s_ref[u % 2, r0:r1] = part

        def weights(u):
            slot = u % 2
            m = jnp.max(s_ref[slot], axis=0, keepdims=True)
            p_ref[slot] = jnp.exp(s_ref[slot] - m).astype(BF16)

        def values(u):
            sub, hp = units[u]
            blk = p * SUBS + sub
            vt2 = jnp.concatenate(
                [vtpad[blk + t, hp] for t in range(WIN_BLOCKS)], axis=1)
            o_t = jnp.dot(vt2, p_ref[u % 2], preferred_element_type=F32)
            o_t = o_t[0:PAIR] / o_t[PAIR:PAIR + 1]
            own = jnp.where(first_head_rows, o_t[:, :Q_ROWS], o_t[:, Q_ROWS:])
            o_ref[sub * Q_ROWS:(sub + 1) * Q_ROWS, hp * PAIR:(hp + 1) * PAIR] = (
                own.T.astype(BF16))

        casts = [(src, dst, r0) for src, dst in zip(cast_in, cast_out)
                 for r0 in range(0, src.shape[0], CAST_ROWS)]
        casts_per_unit = pl.cdiv(len(casts), len(units))

        scores(0)
        for u in range(len(units) + 1):
            if u + 1 < len(units):
                scores(u + 1)
            if u >= 1:
                values(u - 1)
            if u < len(units):
                weights(u)
                for src, dst, r0 in casts[u * casts_per_unit:(u + 1) * casts_per_unit]:
                    dst[r0:r0 + CAST_ROWS, :] = src[r0:r0 + CAST_ROWS, :].astype(BF16)

    @pl.when(p < PAD_BLOCKS // SUBS)
    def _():
        step(True)

    @pl.when(p >= PAD_BLOCKS // SUBS)
    def _():
        step(False)


def _attention(qkv, bias_base, cast_weights, *, batch, seq):
    n = batch * seq
    rows = SUBS * Q_ROWS
    steps = seq // rows
    n_steps = batch * steps
    cast_blocks = []
    for w in cast_weights:
        assert w.ndim == 2 and w.shape[0] % (n_steps * CAST_ROWS) == 0, w.shape
        cast_blocks.append(pl.BlockSpec((w.shape[0] // n_steps, w.shape[1]),
                                        lambda b, p: (b * steps + p, 0)))
    outs = pl.pallas_call(
        functools.partial(_attn_body, n_cast=len(cast_weights)),
        grid=(batch, steps),
        in_specs=[
            pl.BlockSpec((N_HEADS, BIAS_EXT), lambda b, p: (0, 0)),
            pl.BlockSpec((rows, ATTN_WIDTH), lambda b, p: (b * steps + p, 0)),
            pl.BlockSpec((seq, ATTN_WIDTH), lambda b, p: (b, 1)),
            pl.BlockSpec((seq, ATTN_WIDTH), lambda b, p: (b, 2)),
        ] + cast_blocks,
        out_specs=[pl.BlockSpec((rows, ATTN_WIDTH), lambda b, p: (b * steps + p, 0))]
        + cast_blocks,
        out_shape=[jax.ShapeDtypeStruct((n, ATTN_WIDTH), BF16)]
        + [jax.ShapeDtypeStruct(w.shape, BF16) for w in cast_weights],
        scratch_shapes=[
            pltpu.VMEM((PAD_ROWS + seq, ATTN_WIDTH), BF16),
            pltpu.VMEM((PAD_BLOCKS + seq // Q_ROWS, N_PAIRS, PAIR + ONES_ROWS, Q_ROWS), BF16),
            pltpu.VMEM((N_PAIRS, BIAS_ROWS, 2 * Q_ROWS), F32),
            pltpu.VMEM((2, BAND, 2 * Q_ROWS), F32),
            pltpu.VMEM((2, BAND, 2 * Q_ROWS), BF16),
        ],
        compiler_params=_compiler_params(("arbitrary", "arbitrary")),
        name="attn",
    )(bias_base, qkv, qkv, qkv, *cast_weights)
    return outs[0], outs[1:]


def _bias_base(rel_bias):
    far = rel_bias[:, -1:]
    head = jnp.broadcast_to(far, (N_HEADS, Q_ROWS))
    tail = jnp.broadcast_to(far, (N_HEADS, BIAS_EXT - Q_ROWS - 2 * MAX_REL - 1))
    return jnp.concatenate([head, rel_bias, tail], axis=1)


def _mix_body(ya_ref, u_ref, uprev_ref, ga_ref, gb_ref, x_ref, pw_ref, ps_ref,
              pa_ref, pb_ref, wo_ref, o_ref, ext_ref, yb_ref, a_ref, *, tiles_per_seq):
    i = pl.program_id(0)
    tm, d = x_ref.shape
    tile_in_seq = i % tiles_per_seq
    n_groups = len(POOL_WINDOWS)
    a_cols = d // n_groups

    halo = jnp.where(tile_in_seq == 0, jnp.zeros_like(uprev_ref[...]), uprev_ref[...])
    ext_ref[0:POOL_HALO, :] = halo
    ext_ref[POOL_HALO:, :] = u_ref[...]

    pos = tile_in_seq * tm + lax.broadcasted_iota(jnp.int32, (tm, 1), 0)
    for g, w in enumerate(POOL_WINDOWS):
        acols = slice(g * a_cols, (g + 1) * a_cols)
        a_ref[:, acols] = jnp.dot(ya_ref[...], pa_ref[:, acols],
                                  preferred_element_type=F32)
        cols = slice(g * POOL_GROUP, (g + 1) * POOL_GROUP)
        acc = ext_ref[:, cols]
        shift = 1
        while shift < w:
            acc = acc + pltpu.roll(acc, shift, 0)
            shift *= 2
        acc = acc[POOL_HALO:]
        cur = ext_ref[POOL_HALO:, cols]
        cnt = jnp.minimum(pos + 1, w).astype(F32)
        dev = (acc / cnt - cur).astype(BF16)
        y = jnp.dot(dev, pw_ref[g], preferred_element_type=F32)
        yb_ref[:, cols] = (y * ps_ref[:, cols]).astype(BF16)

    bb = jnp.dot(yb_ref[...], pb_ref[...], preferred_element_type=F32)
    gate_a = jax.nn.sigmoid(ga_ref[...].astype(F32))
    gate_b = jax.nn.sigmoid(gb_ref[...].astype(F32))
    merged = gate_a * a_ref[...] + gate_b * bb
    y = jnp.dot(merged.astype(BF16), wo_ref[...], preferred_element_type=F32)
    o_ref[...] = x_ref[...] + y


def _mix(ya, u, gates, x2d, pool_w, pool_scale, w_a, w_b, w_o, *, seq, tm):
    n, d = x2d.shape
    tiles_per_seq = seq // tm
    halo_blocks = tm // POOL_HALO
    const = lambda *shape: pl.BlockSpec(shape, lambda i: (0,) * len(shape),
                                        pipeline_mode=pl.Buffered(1))
    return pl.pallas_call(
        functools.partial(_mix_body, tiles_per_seq=tiles_per_seq),
        grid=(n // tm,),
        in_specs=[
            pl.BlockSpec((tm, ATTN_WIDTH), lambda i: (i, 0)),
            pl.BlockSpec((tm, POOL_WIDTH), lambda i: (i, 0)),
            pl.BlockSpec((POOL_HALO, POOL_WIDTH),
                         lambda i: (jnp.maximum(i * halo_blocks - 1, 0), 0)),
            pl.BlockSpec((tm, d), lambda i: (i, 0)),
            pl.BlockSpec((tm, d), lambda i: (i, 1)),
            pl.BlockSpec((tm, d), lambda i: (i, 0)),
            const(len(POOL_WINDOWS), POOL_GROUP, POOL_GROUP),
            const(1, POOL_WIDTH),
            const(ATTN_WIDTH, d),
            const(POOL_WIDTH, d),
            const(d, d),
        ],
        out_specs=pl.BlockSpec((tm, d), lambda i: (i, 0)),
        out_shape=jax.ShapeDtypeStruct((n, d), F32),
        scratch_shapes=[
            pltpu.VMEM((POOL_HALO + tm, POOL_WIDTH), F32),
            pltpu.VMEM((tm, POOL_WIDTH), BF16),
            pltpu.VMEM((tm, d), F32),
        ],
        compiler_params=_compiler_params(("arbitrary",)),
        name="mix",
    )(ya, u, u, gates, gates, x2d, pool_w, pool_scale, w_a, w_b, w_o)


def _ffn_body(x_ref, g_ref, wg_ref, wu_ref, wd_ref, gf_ref, o_ref, h_ref):
    j = pl.program_id(1)

    @pl.when(j == 0)
    def _():
        x = x_ref[...]
        h_ref[...] = _rms(x, g_ref[...]).astype(BF16)
        o_ref[...] = x

    h = h_ref[...]
    gate = jnp.dot(h, wg_ref[...], preferred_element_type=F32)
    up = jnp.dot(h, wu_ref[...], preferred_element_type=F32)
    act = (jax.nn.silu(gate) * up).astype(BF16)
    o_ref[...] += jnp.dot(act, wd_ref[...], preferred_element_type=F32)

    @pl.when(j == pl.num_programs(1) - 1)
    def _():
        o_ref[...] = _rms(o_ref[...], gf_ref[...])


def _ffn(x2d, g, w_gate_up, w_down, g_final, *, tm, tf):
    n, d = x2d.shape
    d_ff = w_down.shape[0]
    n_f = d_ff // tf
    return pl.pallas_call(
        _ffn_body,
        grid=(n // tm, n_f),
        in_specs=[
            pl.BlockSpec((tm, d), lambda i, j: (i, 0)),
            pl.BlockSpec((1, d), lambda i, j: (0, 0)),
            pl.BlockSpec((d, tf), lambda i, j: (0, j)),
            pl.BlockSpec((d, tf), lambda i, j: (0, j + n_f)),
            pl.BlockSpec((tf, d), lambda i, j: (j, 0)),
            pl.BlockSpec((1, d), lambda i, j: (0, 0)),
        ],
        out_specs=pl.BlockSpec((tm, d), lambda i, j: (i, 0)),
        out_shape=jax.ShapeDtypeStruct((n, d), F32),
        scratch_shapes=[pltpu.VMEM((tm, d), BF16)],
        compiler_params=_compiler_params(("arbitrary", "arbitrary")),
        name="ffn",
    )(x2d, g, w_gate_up, w_gate_up, w_down, g_final)


def _layer(x2d, norm_mix, w_in, rel_bias, pool_w, pool_scale, w_a, w_b, w_o,
           norm_ffn, w_gate_up, w_down, norm_out, *, batch, seq):
    d = x2d.shape[1]
    qkv, u, gates = _in_proj(x2d, norm_mix.reshape(1, d), w_in, tm=256)
    pool_w2d = pool_w.reshape(-1, pool_w.shape[-1])
    ya, (pool_w2d, w_a, w_b, w_o, w_gate_up, w_down) = _attention(
        qkv, _bias_base(rel_bias), (pool_w2d, w_a, w_b, w_o, w_gate_up, w_down),
        batch=batch, seq=seq)
    x1 = _mix(ya, u, gates, x2d, pool_w2d.reshape(pool_w.shape), pool_scale.reshape(1, -1),
              w_a, w_b, w_o, seq=seq, tm=256)
    return _ffn(x1, norm_ffn.reshape(1, d), w_gate_up, w_down,
                norm_out.reshape(1, d), tm=1024, tf=512)


def kernel(x, norm_mix, w_in, rel_bias, pool_w, pool_scale, w_branch_a, w_branch_b,
           w_out, norm_ffn, w_gate_up, w_down, norm_final):
    batch, seq, d = x.shape
    depth = w_in.shape[0]
    assert depth == 1, "the fused ffn kernel applies the final norm to the only layer"
    out = _layer(x.reshape(batch * seq, d), norm_mix[0], w_in[0], rel_bias[0],
                 pool_w[0], pool_scale[0], w_branch_a[0], w_branch_b[0], w_out[0],
                 norm_ffn[0], w_gate_up[0], w_down[0], norm_final,
                 batch=batch, seq=seq)
    return out.reshape(batch, seq, d)
```

```python
import functools
import math

import jax
import jax.numpy as jnp
from jax import lax
from jax.experimental import pallas as pl
from jax.experimental.pallas import tpu as pltpu

F32 = jnp.float32
BF16 = jnp.bfloat16

CHUNK = 64
LEFT_CHUNKS = 8
N_HEADS = 16
HEAD_DIM = 64
ATTN_WIDTH = N_HEADS * HEAD_DIM
MAX_REL = 128
POOL_WINDOWS = (2, 4, 8, 16)
POOL_GROUP = 256
POOL_WIDTH = POOL_GROUP * len(POOL_WINDOWS)
EPS = 1e-6
NEG_INF = -1e30

Q_ROWS = 2 * CHUNK
BAND = (LEFT_CHUNKS + 2) * CHUNK
PAD_ROWS = LEFT_CHUNKS * CHUNK
BIAS_EXT = BAND + Q_ROWS
POOL_HALO = 16

V7X_VMEM_BYTES = 64 * 1024 * 1024
VMEM_LIMIT_BYTES = V7X_VMEM_BYTES - 4 * 1024 * 1024
IN_PROJ_ROWS = 256
STAGE_ROWS = 128
MIX_ROWS = 256
FFN_ROWS = 1024
FFN_COLS = 512


def _rms(x, g):
    ms = jnp.mean(x * x, axis=-1, keepdims=True)
    return x * lax.rsqrt(ms + EPS) * g


def _compiler_params(semantics):
    return pltpu.CompilerParams(
        dimension_semantics=semantics, vmem_limit_bytes=VMEM_LIMIT_BYTES)


def _in_proj_body(x_ref, g_ref, w_hbm, qkv_ref, u_ref, gate_ref, w_ref, stage_ref, sem):
    n_chunks = w_ref.shape[0] // STAGE_ROWS

    def chunk_copy(c, slot):
        rows = pl.ds(pl.multiple_of(c * STAGE_ROWS, STAGE_ROWS), STAGE_ROWS)
        return pltpu.make_async_copy(w_hbm.at[rows], stage_ref.at[slot], sem.at[slot])

    @pl.when(pl.program_id(0) == 0)
    def _():
        chunk_copy(0, 0).start()

        def two_chunks(cc, carry):
            for slot in range(2):
                c = 2 * cc + slot

                @pl.when(c + 1 < n_chunks)
                def _():
                    chunk_copy(c + 1, 1 - slot).start()

                chunk_copy(c, slot).wait()
                rows = pl.ds(pl.multiple_of(c * STAGE_ROWS, STAGE_ROWS), STAGE_ROWS)
                w_ref[rows, :] = stage_ref[slot].astype(BF16)
            return carry

        lax.fori_loop(0, n_chunks // 2, two_chunks, 0)

    h = _rms(x_ref[...], g_ref[...]).astype(BF16)
    q_end = qkv_ref.shape[1]
    u_end = q_end + u_ref.shape[1]
    qkv_ref[...] = jnp.dot(h, w_ref[:, :q_end], preferred_element_type=F32).astype(BF16)
    u_ref[...] = jnp.dot(h, w_ref[:, q_end:u_end], preferred_element_type=F32)
    gate_ref[...] = jnp.dot(h, w_ref[:, u_end:], preferred_element_type=F32).astype(BF16)


def _in_proj(x2d, g, w, *, tm):
    n, d = x2d.shape
    in_width = w.shape[1]
    n_gate = in_width - 3 * ATTN_WIDTH - POOL_WIDTH
    assert d % (2 * STAGE_ROWS) == 0
    return pl.pallas_call(
        _in_proj_body,
        grid=(n // tm,),
        in_specs=[
            pl.BlockSpec((tm, d), lambda i: (i, 0)),
            pl.BlockSpec((1, d), lambda i: (0, 0)),
            pl.BlockSpec(memory_space=pl.ANY),
        ],
        out_specs=[
            pl.BlockSpec((tm, 3 * ATTN_WIDTH), lambda i: (i, 0)),
            pl.BlockSpec((tm, POOL_WIDTH), lambda i: (i, 0)),
            pl.BlockSpec((tm, n_gate), lambda i: (i, 0)),
        ],
        out_shape=[
            jax.ShapeDtypeStruct((n, 3 * ATTN_WIDTH), BF16),
            jax.ShapeDtypeStruct((n, POOL_WIDTH), F32),
            jax.ShapeDtypeStruct((n, n_gate), BF16),
        ],
        scratch_shapes=[
            pltpu.VMEM((d, in_width), BF16),
            pltpu.VMEM((2, STAGE_ROWS, in_width), F32),
            pltpu.SemaphoreType.DMA((2,)),
        ],
        compiler_params=_compiler_params(("arbitrary",)),
        name="in_proj",
    )(x2d, g, w)


PAIR = 2 * HEAD_DIM
N_PAIRS = N_HEADS // 2
WIN_BLOCKS = BAND // Q_ROWS
PAD_BLOCKS = PAD_ROWS // Q_ROWS
FLAT_END = PAD_ROWS - MAX_REL
BIAS_ROWS = CHUNK + BAND - FLAT_END
ONES_ROWS = 16
SUBS = 4
CAST_ROWS = 16


def _attn_body(base_ref, q_ref, k_ref, v_ref, *rest, n_cast):
    cast_in = rest[:n_cast]
    o_ref = rest[n_cast]
    cast_out = rest[n_cast + 1:2 * n_cast + 1]
    kpad, vtpad, bias_ref, s_ref, p_ref = rest[2 * n_cast + 1:]
    b = pl.program_id(0)
    p = pl.program_id(1)

    @pl.when((b == 0) & (p == 0))
    def _():
        key = lax.broadcasted_iota(jnp.int32, (BAND, 2 * Q_ROWS), 0)
        qry = lax.broadcasted_iota(jnp.int32, (BAND, 2 * Q_ROWS), 1) % Q_ROWS
        lo = (qry // CHUNK) * CHUNK
        in_band = (key >= lo) & (key < lo + (LEFT_CHUNKS + 1) * CHUNK)
        for hp in range(N_PAIRS):
            halves = []
            for a in range(2):
                h = 2 * hp + a
                base = jnp.broadcast_to(base_ref[h:h + 1, :], (BAND, BIAS_EXT))
                rot = pltpu.roll(base, 0, 1, stride=1, stride_axis=0)
                halves.append(rot[:, :Q_ROWS] - base[:, 0:1])
            full = jnp.where(in_band, jnp.concatenate(halves, axis=1), NEG_INF)
            bias_ref[hp, 0:CHUNK] = full[0:CHUNK]
            bias_ref[hp, CHUNK:] = full[FLAT_END:]
        kpad[0:PAD_ROWS, :] = jnp.zeros((PAD_ROWS, ATTN_WIDTH), BF16)
        vtpad[0:PAD_BLOCKS, :, 0:PAIR, :] = jnp.zeros(
            (PAD_BLOCKS, N_PAIRS, PAIR, Q_ROWS), BF16)
        vtpad[:, :, PAIR:, :] = jnp.ones(
            (vtpad.shape[0], N_PAIRS, ONES_ROWS, Q_ROWS), BF16)

    @pl.when(p == 0)
    def _():
        kpad[PAD_ROWS:, :] = k_ref[...]
        for c in range(v_ref.shape[0] // Q_ROWS):
            blk_t = v_ref[c * Q_ROWS:(c + 1) * Q_ROWS, :].astype(F32).T.astype(BF16)
            for hp in range(N_PAIRS):
                vtpad[PAD_BLOCKS + c, hp, 0:PAIR, :] = blk_t[hp * PAIR:(hp + 1) * PAIR, :]

    lane = lax.broadcasted_iota(jnp.int32, (Q_ROWS, PAIR), 1)
    first_head = lane < HEAD_DIM
    row = lax.broadcasted_iota(jnp.int32, (PAIR, Q_ROWS), 0)
    first_head_rows = row < HEAD_DIM
    scale = jnp.asarray(1.0 / math.sqrt(HEAD_DIM), BF16)

    def step(mask_missing_keys):
        units = [(sub, hp) for sub in range(SUBS) for hp in range(N_PAIRS)]

        def scores(u):
            sub, hp = units[u]
            blk = p * SUBS + sub
            lanes = slice(hp * PAIR, (hp + 1) * PAIR)
            q2 = q_ref[sub * Q_ROWS:(sub + 1) * Q_ROWS, lanes] * scale
            zero = jnp.zeros_like(q2)
            qm = jnp.concatenate([jnp.where(first_head, q2, zero),
                                  jnp.where(first_head, zero, q2)], axis=0)
            k2 = kpad[pl.ds(pl.multiple_of(blk * Q_ROWS, Q_ROWS), BAND), lanes]
            s = lax.dot_general(k2, qm, (((1,), (1,)), ((), ())),
                                preferred_element_type=F32)
            parts = [(0, CHUNK, s[0:CHUNK] + bias_ref[hp, 0:CHUNK]),
                     (CHUNK, FLAT_END, s[CHUNK:FLAT_END]),
                     (FLAT_END, BAND, s[FLAT_END:] + bias_ref[hp, CHUNK:])]
            for r0, r1, part in parts:
                if mask_missing_keys:
                    key = r0 + lax.broadcasted_iota(jnp.int32, part.shape, 0)
                    part = jnp.where(key >= PAD_ROWS - blk * Q_ROWS, part, NEG_INF)
                s_ref[u % 2, r0:r1] = part

        def weights(u):
            slot = u % 2
            m = jnp.max(s_ref[slot], axis=0, keepdims=True)
            p_ref[slot] = jnp.exp(s_ref[slot] - m).astype(BF16)

        def values(u):
            sub, hp = units[u]
            blk = p * SUBS + sub
            vt2 = jnp.concatenate(
                [vtpad[blk + t, hp] for t in range(WIN_BLOCKS)], axis=1)
            o_t = jnp.dot(vt2, p_ref[u % 2], preferred_element_type=F32)
            o_t = o_t[0:PAIR] / o_t[PAIR:PAIR + 1]
            own = jnp.where(first_head_rows, o_t[:, :Q_ROWS], o_t[:, Q_ROWS:])
            o_ref[sub * Q_ROWS:(sub + 1) * Q_ROWS, hp * PAIR:(hp + 1) * PAIR] = (
                own.T.astype(BF16))

        casts = [(src, dst, r0) for src, dst in zip(cast_in, cast_out)
                 for r0 in range(0, src.shape[0], CAST_ROWS)]
        casts_per_unit = pl.cdiv(len(casts), len(units))

        scores(0)
        for u in range(len(units) + 1):
            if u + 1 < len(units):
                scores(u + 1)
            if u >= 1:
                values(u - 1)
            if u < len(units):
                weights(u)
                for src, dst, r0 in casts[u * casts_per_unit:(u + 1) * casts_per_unit]:
                    dst[r0:r0 + CAST_ROWS, :] = src[r0:r0 + CAST_ROWS, :].astype(BF16)

    @pl.when(p < PAD_BLOCKS // SUBS)
    def _():
        step(True)

    @pl.when(p >= PAD_BLOCKS // SUBS)
    def _():
        step(False)


def _attention(qkv, bias_base, cast_weights, *, batch, seq):
    n = batch * seq
    rows = SUBS * Q_ROWS
    steps = seq // rows
    n_steps = batch * steps
    cast_blocks = []
    for w in cast_weights:
        assert w.ndim == 2 and w.shape[0] % (n_steps * CAST_ROWS) == 0, w.shape
        cast_blocks.append(pl.BlockSpec((w.shape[0] // n_steps, w.shape[1]),
                                        lambda b, p: (b * steps + p, 0)))
    outs = pl.pallas_call(
        functools.partial(_attn_body, n_cast=len(cast_weights)),
        grid=(batch, steps),
        in_specs=[
            pl.BlockSpec((N_HEADS, BIAS_EXT), lambda b, p: (0, 0)),
            pl.BlockSpec((rows, ATTN_WIDTH), lambda b, p: (b * steps + p, 0)),
            pl.BlockSpec((seq, ATTN_WIDTH), lambda b, p: (b, 1)),
            pl.BlockSpec((seq, ATTN_WIDTH), lambda b, p: (b, 2)),
        ] + cast_blocks,
        out_specs=[pl.BlockSpec((rows, ATTN_WIDTH), lambda b, p: (b * steps + p, 0))]
        + cast_blocks,
        out_shape=[jax.ShapeDtypeStruct((n, ATTN_WIDTH), BF16)]
        + [jax.ShapeDtypeStruct(w.shape, BF16) for w in cast_weights],
        scratch_shapes=[
            pltpu.VMEM((PAD_ROWS + seq, ATTN_WIDTH), BF16),
            pltpu.VMEM((PAD_BLOCKS + seq // Q_ROWS, N_PAIRS, PAIR + ONES_ROWS, Q_ROWS), BF16),
            pltpu.VMEM((N_PAIRS, BIAS_ROWS, 2 * Q_ROWS), F32),
            pltpu.VMEM((2, BAND, 2 * Q_ROWS), F32),
            pltpu.VMEM((2, BAND, 2 * Q_ROWS), BF16),
        ],
        compiler_params=_compiler_params(("arbitrary", "arbitrary")),
        name="attn",
    )(bias_base, qkv, qkv, qkv, *cast_weights)
    return outs[0], outs[1:]


def _bias_base(rel_bias):
    far = rel_bias[:, -1:]
    head = jnp.broadcast_to(far, (N_HEADS, Q_ROWS))
    tail = jnp.broadcast_to(far, (N_HEADS, BIAS_EXT - Q_ROWS - 2 * MAX_REL - 1))
    return jnp.concatenate([head, rel_bias, tail], axis=1)


def _mix_body(ya_ref, u_ref, uprev_ref, ga_ref, gb_ref, x_ref, pw_ref, ps_ref,
              pa_ref, pb_ref, wo_ref, o_ref, ext_ref, yb_ref, a_ref, *, tiles_per_seq):
    i = pl.program_id(0)
    tm, d = x_ref.shape
    tile_in_seq = i % tiles_per_seq
    n_groups = len(POOL_WINDOWS)
    a_cols = d // n_groups

    halo = jnp.where(tile_in_seq == 0, jnp.zeros_like(uprev_ref[...]), uprev_ref[...])
    ext_ref[0:POOL_HALO, :] = halo
    ext_ref[POOL_HALO:, :] = u_ref[...]

    pos = tile_in_seq * tm + lax.broadcasted_iota(jnp.int32, (tm, 1), 0)
    for g, w in enumerate(POOL_WINDOWS):
        acols = slice(g * a_cols, (g + 1) * a_cols)
        a_ref[:, acols] = jnp.dot(ya_ref[...], pa_ref[:, acols],
                                  preferred_element_type=F32)
        cols = slice(g * POOL_GROUP, (g + 1) * POOL_GROUP)
        acc = ext_ref[:, cols]
        shift = 1
        while shift < w:
            acc = acc + pltpu.roll(acc, shift, 0)
            shift *= 2
        acc = acc[POOL_HALO:]
        cur = ext_ref[POOL_HALO:, cols]
        cnt = jnp.minimum(pos + 1, w).astype(F32)
        dev = (acc / cnt - cur).astype(BF16)
        y = jnp.dot(dev, pw_ref[g], preferred_element_type=F32)
        yb_ref[:, cols] = (y * ps_ref[:, cols]).astype(BF16)

    bb = jnp.dot(yb_ref[...], pb_ref[...], preferred_element_type=F32)
    gate_a = jax.nn.sigmoid(ga_ref[...].astype(F32))
    gate_b = jax.nn.sigmoid(gb_ref[...].astype(F32))
    merged = gate_a * a_ref[...] + gate_b * bb
    y = jnp.dot(merged.astype(BF16), wo_ref[...], preferred_element_type=F32)
    o_ref[...] = x_ref[...] + y


def _mix(ya, u, gates, x2d, pool_w, pool_scale, w_a, w_b, w_o, *, seq, tm):
    n, d = x2d.shape
    tiles_per_seq = seq // tm
    halo_blocks = tm // POOL_HALO
    const = lambda *shape: pl.BlockSpec(shape, lambda i: (0,) * len(shape),
                                        pipeline_mode=pl.Buffered(1))
    return pl.pallas_call(
        functools.partial(_mix_body, tiles_per_seq=tiles_per_seq),
        grid=(n // tm,),
        in_specs=[
            pl.BlockSpec((tm, ATTN_WIDTH), lambda i: (i, 0)),
            pl.BlockSpec((tm, POOL_WIDTH), lambda i: (i, 0)),
            pl.BlockSpec((POOL_HALO, POOL_WIDTH),
                         lambda i: (jnp.maximum(i * halo_blocks - 1, 0), 0)),
            pl.BlockSpec((tm, d), lambda i: (i, 0)),
            pl.BlockSpec((tm, d), lambda i: (i, 1)),
            pl.BlockSpec((tm, d), lambda i: (i, 0)),
            const(len(POOL_WINDOWS), POOL_GROUP, POOL_GROUP),
            const(1, POOL_WIDTH),
            const(ATTN_WIDTH, d),
            const(POOL_WIDTH, d),
            const(d, d),
        ],
        out_specs=pl.BlockSpec((tm, d), lambda i: (i, 0)),
        out_shape=jax.ShapeDtypeStruct((n, d), F32),
        scratch_shapes=[
            pltpu.VMEM((POOL_HALO + tm, POOL_WIDTH), F32),
            pltpu.VMEM((tm, POOL_WIDTH), BF16),
            pltpu.VMEM((tm, d), F32),
        ],
        compiler_params=_compiler_params(("arbitrary",)),
        name="mix",
    )(ya, u, u, gates, gates, x2d, pool_w, pool_scale, w_a, w_b, w_o)


def _ffn_body(x_ref, g_ref, wg_ref, wu_ref, wd_ref, gf_ref, o_ref, h_ref):
    j = pl.program_id(1)

    def slab(first):
        if first:
            x = x_ref[...]
            h = _rms(x, g_ref[...]).astype(BF16)
            h_ref[...] = h
        else:
            h = h_ref[...]
        gate = jnp.dot(h, wg_ref[...], preferred_element_type=F32)
        up = jnp.dot(h, wu_ref[...], preferred_element_type=F32)
        act = (jax.nn.silu(gate) * up).astype(BF16)
        down = jnp.dot(act, wd_ref[...], preferred_element_type=F32)
        o_ref[...] = (x if first else o_ref[...]) + down

    @pl.when(j == 0)
    def _():
        slab(True)

    @pl.when(j > 0)
    def _():
        slab(False)

    @pl.when(j == pl.num_programs(1) - 1)
    def _():
        o_ref[...] = _rms(o_ref[...], gf_ref[...])


def _ffn(x2d, g, w_gate_up, w_down, g_final, *, tm, tf):
    n, d = x2d.shape
    d_ff = w_down.shape[0]
    n_f = d_ff // tf
    return pl.pallas_call(
        _ffn_body,
        grid=(n // tm, n_f),
        in_specs=[
            pl.BlockSpec((tm, d), lambda i, j: (i, 0)),
            pl.BlockSpec((1, d), lambda i, j: (0, 0)),
            pl.BlockSpec((d, tf), lambda i, j: (0, j)),
            pl.BlockSpec((d, tf), lambda i, j: (0, j + n_f)),
            pl.BlockSpec((tf, d), lambda i, j: (j, 0)),
            pl.BlockSpec((1, d), lambda i, j: (0, 0)),
        ],
        out_specs=pl.BlockSpec((tm, d), lambda i, j: (i, 0)),
        out_shape=jax.ShapeDtypeStruct((n, d), F32),
        scratch_shapes=[pltpu.VMEM((tm, d), BF16)],
        compiler_params=_compiler_params(("arbitrary", "arbitrary")),
        name="ffn",
    )(x2d, g, w_gate_up, w_gate_up, w_down, g_final)


def _layer(x2d, norm_mix, w_in, rel_bias, pool_w, pool_scale, w_a, w_b, w_o,
           norm_ffn, w_gate_up, w_down, norm_out, *, batch, seq):
    d = x2d.shape[1]
    qkv, u, gates = _in_proj(x2d, norm_mix.reshape(1, d), w_in, tm=IN_PROJ_ROWS)
    pool_w2d = pool_w.reshape(-1, pool_w.shape[-1])
    ya, (pool_w2d, w_a, w_b, w_o, w_gate_up, w_down) = _attention(
        qkv, _bias_base(rel_bias), (pool_w2d, w_a, w_b, w_o, w_gate_up, w_down),
        batch=batch, seq=seq)
    x1 = _mix(ya, u, gates, x2d, pool_w2d.reshape(pool_w.shape), pool_scale.reshape(1, -1),
              w_a, w_b, w_o, seq=seq, tm=MIX_ROWS)
    return _ffn(x1, norm_ffn.reshape(1, d), w_gate_up, w_down,
                norm_out.reshape(1, d), tm=FFN_ROWS, tf=FFN_COLS)


def kernel(x, norm_mix, w_in, rel_bias, pool_w, pool_scale, w_branch_a, w_branch_b,
           w_out, norm_ffn, w_gate_up, w_down, norm_final):
    batch, seq, d = x.shape
    depth = w_in.shape[0]
    assert depth == 1, "the fused ffn kernel applies the final norm to the only layer"
    out = _layer(x.reshape(batch * seq, d), norm_mix[0], w_in[0], rel_bias[0],
                 pool_w[0], pool_scale[0], w_branch_a[0], w_branch_b[0], w_out[0],
                 norm_ffn[0], w_gate_up[0], w_down[0], norm_final,
                 batch=batch, seq=seq)
    return out.reshape(batch, seq, d)
```

```python
import functools
import math

import jax
import jax.numpy as jnp
from jax import lax
from jax.experimental import pallas as pl
from jax.experimental.pallas import tpu as pltpu

F32 = jnp.float32
BF16 = jnp.bfloat16

CHUNK = 64
LEFT_CHUNKS = 8
N_HEADS = 16
HEAD_DIM = 64
ATTN_WIDTH = N_HEADS * HEAD_DIM
MAX_REL = 128
POOL_WINDOWS = (2, 4, 8, 16)
POOL_GROUP = 256
POOL_WIDTH = POOL_GROUP * len(POOL_WINDOWS)
EPS = 1e-6
NEG_INF = -1e30

Q_ROWS = 2 * CHUNK
BAND = (LEFT_CHUNKS + 2) * CHUNK
PAD_ROWS = LEFT_CHUNKS * CHUNK
BIAS_EXT = BAND + Q_ROWS
POOL_HALO = 16

V7X_VMEM_BYTES = 64 * 1024 * 1024
VMEM_LIMIT_BYTES = V7X_VMEM_BYTES - 4 * 1024 * 1024
IN_PROJ_ROWS = 256
STAGE_ROWS = 128
MIX_ROWS = 256
FFN_ROWS = 1024
FFN_COLS = 512


def _rms(x, g):
    ms = jnp.mean(x * x, axis=-1, keepdims=True)
    return x * lax.rsqrt(ms + EPS) * g


def _compiler_params(semantics):
    return pltpu.CompilerParams(
        dimension_semantics=semantics, vmem_limit_bytes=VMEM_LIMIT_BYTES)


def _in_proj_body(x_ref, g_ref, w_hbm, qkv_ref, u_ref, gate_ref, w_ref, stage_ref, sem):
    n_chunks = w_ref.shape[0] // STAGE_ROWS

    def chunk_copy(c, slot):
        rows = pl.ds(pl.multiple_of(c * STAGE_ROWS, STAGE_ROWS), STAGE_ROWS)
        return pltpu.make_async_copy(w_hbm.at[rows], stage_ref.at[slot], sem.at[slot])

    @pl.when(pl.program_id(0) == 0)
    def _():
        chunk_copy(0, 0).start()

        def two_chunks(cc, carry):
            for slot in range(2):
                c = 2 * cc + slot

                @pl.when(c + 1 < n_chunks)
                def _():
                    chunk_copy(c + 1, 1 - slot).start()

                chunk_copy(c, slot).wait()
                rows = pl.ds(pl.multiple_of(c * STAGE_ROWS, STAGE_ROWS), STAGE_ROWS)
                w_ref[rows, :] = stage_ref[slot].astype(BF16)
            return carry

        lax.fori_loop(0, n_chunks // 2, two_chunks, 0)

    h = _rms(x_ref[...], g_ref[...]).astype(BF16)
    q_end = qkv_ref.shape[1]
    u_end = q_end + u_ref.shape[1]
    qkv_ref[...] = jnp.dot(h, w_ref[:, :q_end], preferred_element_type=F32).astype(BF16)
    u_ref[...] = jnp.dot(h, w_ref[:, q_end:u_end], preferred_element_type=F32)
    gate_ref[...] = jnp.dot(h, w_ref[:, u_end:], preferred_element_type=F32).astype(BF16)


def _in_proj(x2d, g, w, *, tm):
    n, d = x2d.shape
    in_width = w.shape[1]
    n_gate = in_width - 3 * ATTN_WIDTH - POOL_WIDTH
    assert d % (2 * STAGE_ROWS) == 0
    return pl.pallas_call(
        _in_proj_body,
        grid=(n // tm,),
        in_specs=[
            pl.BlockSpec((tm, d), lambda i: (i, 0)),
            pl.BlockSpec((1, d), lambda i: (0, 0)),
            pl.BlockSpec(memory_space=pl.ANY),
        ],
        out_specs=[
            pl.BlockSpec((tm, 3 * ATTN_WIDTH), lambda i: (i, 0)),
            pl.BlockSpec((tm, POOL_WIDTH), lambda i: (i, 0)),
            pl.BlockSpec((tm, n_gate), lambda i: (i, 0)),
        ],
        out_shape=[
            jax.ShapeDtypeStruct((n, 3 * ATTN_WIDTH), BF16),
            jax.ShapeDtypeStruct((n, POOL_WIDTH), F32),
            jax.ShapeDtypeStruct((n, n_gate), BF16),
        ],
        scratch_shapes=[
            pltpu.VMEM((d, in_width), BF16),
            pltpu.VMEM((2, STAGE_ROWS, in_width), F32),
            pltpu.SemaphoreType.DMA((2,)),
        ],
        compiler_params=_compiler_params(("arbitrary",)),
        name="in_proj",
    )(x2d, g, w)


PAIR = 2 * HEAD_DIM
N_PAIRS = N_HEADS // 2
WIN_BLOCKS = BAND // Q_ROWS
PAD_BLOCKS = PAD_ROWS // Q_ROWS
FLAT_END = PAD_ROWS - MAX_REL
BIAS_ROWS = CHUNK + BAND - FLAT_END
ONES_ROWS = 16
SUBS = 4
CAST_ROWS = 16


def _attn_body(base_ref, q_ref, k_ref, v_ref, *rest, n_cast):
    cast_in = rest[:n_cast]
    o_ref = rest[n_cast]
    cast_out = rest[n_cast + 1:2 * n_cast + 1]
    kpad, vtpad, bias_ref, s_ref, p_ref = rest[2 * n_cast + 1:]
    b = pl.program_id(0)
    p = pl.program_id(1)

    @pl.when((b == 0) & (p == 0))
    def _():
        key = lax.broadcasted_iota(jnp.int32, (BAND, 2 * Q_ROWS), 0)
        qry = lax.broadcasted_iota(jnp.int32, (BAND, 2 * Q_ROWS), 1) % Q_ROWS
        lo = (qry // CHUNK) * CHUNK
        in_band = (key >= lo) & (key < lo + (LEFT_CHUNKS + 1) * CHUNK)
        for hp in range(N_PAIRS):
            halves = []
            for a in range(2):
                h = 2 * hp + a
                base = jnp.broadcast_to(base_ref[h:h + 1, :], (BAND, BIAS_EXT))
                rot = pltpu.roll(base, 0, 1, stride=1, stride_axis=0)
                halves.append(rot[:, :Q_ROWS] - base[:, 0:1])
            full = jnp.where(in_band, jnp.concatenate(halves, axis=1), NEG_INF)
            bias_ref[hp, 0:CHUNK] = full[0:CHUNK]
            bias_ref[hp, CHUNK:] = full[FLAT_END:]
        kpad[0:PAD_ROWS, :] = jnp.zeros((PAD_ROWS, ATTN_WIDTH), BF16)
        vtpad[0:PAD_BLOCKS, :, 0:PAIR, :] = jnp.zeros(
            (PAD_BLOCKS, N_PAIRS, PAIR, Q_ROWS), BF16)
        vtpad[:, :, PAIR:, :] = jnp.ones(
            (vtpad.shape[0], N_PAIRS, ONES_ROWS, Q_ROWS), BF16)

    @pl.when(p == 0)
    def _():
        kpad[PAD_ROWS:, :] = k_ref[...]
        for c in range(v_ref.shape[0] // Q_ROWS):
            blk_t = v_ref[c * Q_ROWS:(c + 1) * Q_ROWS, :].astype(F32).T.astype(BF16)
            for hp in range(N_PAIRS):
                vtpad[PAD_BLOCKS + c, hp, 0:PAIR, :] = blk_t[hp * PAIR:(hp + 1) * PAIR, :]

    lane = lax.broadcasted_iota(jnp.int32, (Q_ROWS, PAIR), 1)
    first_head = lane < HEAD_DIM
    row = lax.broadcasted_iota(jnp.int32, (PAIR, Q_ROWS), 0)
    first_head_rows = row < HEAD_DIM
    scale = jnp.asarray(1.0 / math.sqrt(HEAD_DIM), BF16)

    def step(mask_missing_keys):
        units = [(sub, hp) for sub in range(SUBS) for hp in range(N_PAIRS)]

        def scores(u):
            sub, hp = units[u]
            blk = p * SUBS + sub
            lanes = slice(hp * PAIR, (hp + 1) * PAIR)
            q2 = q_ref[sub * Q_ROWS:(sub + 1) * Q_ROWS, lanes] * scale
            q2t = q2.astype(F32).T.astype(BF16)
            zero = jnp.zeros_like(q2t)
            qm_t = jnp.concatenate([jnp.where(first_head_rows, q2t, zero),
                                    jnp.where(first_head_rows, zero, q2t)], axis=1)
            k2 = kpad[pl.ds(pl.multiple_of(blk * Q_ROWS, Q_ROWS), BAND), lanes]
            s = jnp.dot(k2, qm_t, preferred_element_type=F32)
            parts = [(0, CHUNK, s[0:CHUNK] + bias_ref[hp, 0:CHUNK]),
                     (CHUNK, FLAT_END, s[CHUNK:FLAT_END]),
                     (FLAT_END, BAND, s[FLAT_END:] + bias_ref[hp, CHUNK:])]
            for r0, r1, part in parts:
                if mask_missing_keys:
                    key = r0 + lax.broadcasted_iota(jnp.int32, part.shape, 0)
                    part = jnp.where(key >= PAD_ROWS - blk * Q_ROWS, part, NEG_INF)
                s_ref[u % 2, r0:r1] = part

        def weights(u):
            slot = u % 2
            m = jnp.max(s_ref[slot], axis=0, keepdims=True)
            p_ref[slot] = jnp.exp(s_ref[slot] - m).astype(BF16)

        def values(u):
            sub, hp = units[u]
            blk = p * SUBS + sub
            vt2 = jnp.concatenate(
                [vtpad[blk + t, hp] for t in range(WIN_BLOCKS)], axis=1)
            o_t = jnp.dot(vt2, p_ref[u % 2], preferred_element_type=F32)
            o_t = o_t[0:PAIR] / o_t[PAIR:PAIR + 1]
            own = jnp.where(first_head_rows, o_t[:, :Q_ROWS], o_t[:, Q_ROWS:])
            o_ref[sub * Q_ROWS:(sub + 1) * Q_ROWS, hp * PAIR:(hp + 1) * PAIR] = (
                own.T.astype(BF16))

        casts = [(src, dst, r0) for src, dst in zip(cast_in, cast_out)
                 for r0 in range(0, src.shape[0], CAST_ROWS)]
        casts_per_unit = pl.cdiv(len(casts), len(units))

        scores(0)
        for u in range(len(units) + 1):
            if u + 1 < len(units):
                scores(u + 1)
            if u >= 1:
                values(u - 1)
            if u < len(units):
                weights(u)
                for src, dst, r0 in casts[u * casts_per_unit:(u + 1) * casts_per_unit]:
                    dst[r0:r0 + CAST_ROWS, :] = src[r0:r0 + CAST_ROWS, :].astype(BF16)

    @pl.when(p < PAD_BLOCKS // SUBS)
    def _():
        step(True)

    @pl.when(p >= PAD_BLOCKS // SUBS)
    def _():
        step(False)


def _attention(qkv, bias_base, cast_weights, *, batch, seq):
    n = batch * seq
    rows = SUBS * Q_ROWS
    steps = seq // rows
    n_steps = batch * steps
    cast_blocks = []
    for w in cast_weights:
        assert w.ndim == 2 and w.shape[0] % (n_steps * CAST_ROWS) == 0, w.shape
        cast_blocks.append(pl.BlockSpec((w.shape[0] // n_steps, w.shape[1]),
                                        lambda b, p: (b * steps + p, 0)))
    outs = pl.pallas_call(
        functools.partial(_attn_body, n_cast=len(cast_weights)),
        grid=(batch, steps),
        in_specs=[
            pl.BlockSpec((N_HEADS, BIAS_EXT), lambda b, p: (0, 0)),
            pl.BlockSpec((rows, ATTN_WIDTH), lambda b, p: (b * steps + p, 0)),
            pl.BlockSpec((seq, ATTN_WIDTH), lambda b, p: (b, 1)),
            pl.BlockSpec((seq, ATTN_WIDTH), lambda b, p: (b, 2)),
        ] + cast_blocks,
        out_specs=[pl.BlockSpec((rows, ATTN_WIDTH), lambda b, p: (b * steps + p, 0))]
        + cast_blocks,
        out_shape=[jax.ShapeDtypeStruct((n, ATTN_WIDTH), BF16)]
        + [jax.ShapeDtypeStruct(w.shape, BF16) for w in cast_weights],
        scratch_shapes=[
            pltpu.VMEM((PAD_ROWS + seq, ATTN_WIDTH), BF16),
            pltpu.VMEM((PAD_BLOCKS + seq // Q_ROWS, N_PAIRS, PAIR + ONES_ROWS, Q_ROWS), BF16),
            pltpu.VMEM((N_PAIRS, BIAS_ROWS, 2 * Q_ROWS), F32),
            pltpu.VMEM((2, BAND, 2 * Q_ROWS), F32),
            pltpu.VMEM((2, BAND, 2 * Q_ROWS), BF16),
        ],
        compiler_params=_compiler_params(("arbitrary", "arbitrary")),
        name="attn",
    )(bias_base, qkv, qkv, qkv, *cast_weights)
    return outs[0], outs[1:]


def _bias_base(rel_bias):
    far = rel_bias[:, -1:]
    head = jnp.broadcast_to(far, (N_HEADS, Q_ROWS))
    tail = jnp.broadcast_to(far, (N_HEADS, BIAS_EXT - Q_ROWS - 2 * MAX_REL - 1))
    return jnp.concatenate([head, rel_bias, tail], axis=1)


def _mix_body(ya_ref, u_ref, uprev_ref, ga_ref, gb_ref, x_ref, pw_ref, ps_ref,
              pa_ref, pb_ref, wo_ref, o_ref, ext_ref, yb_ref, a_ref, *, tiles_per_seq):
    i = pl.program_id(0)
    tm, d = x_ref.shape
    tile_in_seq = i % tiles_per_seq
    n_groups = len(POOL_WINDOWS)
    a_cols = d // n_groups

    halo = jnp.where(tile_in_seq == 0, jnp.zeros_like(uprev_ref[...]), uprev_ref[...])
    ext_ref[0:POOL_HALO, :] = halo
    ext_ref[POOL_HALO:, :] = u_ref[...]

    pos = tile_in_seq * tm + lax.broadcasted_iota(jnp.int32, (tm, 1), 0)
    for g, w in enumerate(POOL_WINDOWS):
        acols = slice(g * a_cols, (g + 1) * a_cols)
        a_ref[:, acols] = jnp.dot(ya_ref[...], pa_ref[:, acols],
                                  preferred_element_type=F32)
        cols = slice(g * POOL_GROUP, (g + 1) * POOL_GROUP)
        acc = ext_ref[:, cols]
        shift = 1
        while shift < w:
            acc = acc + pltpu.roll(acc, shift, 0)
            shift *= 2
        acc = acc[POOL_HALO:]
        cur = ext_ref[POOL_HALO:, cols]
        cnt = jnp.minimum(pos + 1, w).astype(F32)
        dev = (acc / cnt - cur).astype(BF16)
        y = jnp.dot(dev, pw_ref[g], preferred_element_type=F32)
        yb_ref[:, cols] = (y * ps_ref[:, cols]).astype(BF16)

    bb = jnp.dot(yb_ref[...], pb_ref[...], preferred_element_type=F32)
    gate_a = jax.nn.sigmoid(ga_ref[...].astype(F32))
    gate_b = jax.nn.sigmoid(gb_ref[...].astype(F32))
    merged = gate_a * a_ref[...] + gate_b * bb
    y = jnp.dot(merged.astype(BF16), wo_ref[...], preferred_element_type=F32)
    o_ref[...] = x_ref[...] + y


def _mix(ya, u, gates, x2d, pool_w, pool_scale, w_a, w_b, w_o, *, seq, tm):
    n, d = x2d.shape
    tiles_per_seq = seq // tm
    halo_blocks = tm // POOL_HALO
    const = lambda *shape: pl.BlockSpec(shape, lambda i: (0,) * len(shape),
                                        pipeline_mode=pl.Buffered(1))
    return pl.pallas_call(
        functools.partial(_mix_body, tiles_per_seq=tiles_per_seq),
        grid=(n // tm,),
        in_specs=[
            pl.BlockSpec((tm, ATTN_WIDTH), lambda i: (i, 0)),
            pl.BlockSpec((tm, POOL_WIDTH), lambda i: (i, 0)),
            pl.BlockSpec((POOL_HALO, POOL_WIDTH),
                         lambda i: (jnp.maximum(i * halo_blocks - 1, 0), 0)),
            pl.BlockSpec((tm, d), lambda i: (i, 0)),
            pl.BlockSpec((tm, d), lambda i: (i, 1)),
            pl.BlockSpec((tm, d), lambda i: (i, 0)),
            const(len(POOL_WINDOWS), POOL_GROUP, POOL_GROUP),
            const(1, POOL_WIDTH),
            const(ATTN_WIDTH, d),
            const(POOL_WIDTH, d),
            const(d, d),
        ],
        out_specs=pl.BlockSpec((tm, d), lambda i: (i, 0)),
        out_shape=jax.ShapeDtypeStruct((n, d), F32),
        scratch_shapes=[
            pltpu.VMEM((POOL_HALO + tm, POOL_WIDTH), F32),
            pltpu.VMEM((tm, POOL_WIDTH), BF16),
            pltpu.VMEM((tm, d), F32),
        ],
        compiler_params=_compiler_params(("arbitrary",)),
        name="mix",
    )(ya, u, u, gates, gates, x2d, pool_w, pool_scale, w_a, w_b, w_o)


def _ffn_body(x_ref, g_ref, wg_ref, wu_ref, wd_ref, gf_ref, o_ref, h_ref):
    j = pl.program_id(1)

    def slab(first):
        if first:
            x = x_ref[...]
            h = _rms(x, g_ref[...]).astype(BF16)
            h_ref[...] = h
        else:
            h = h_ref[...]
        gate = jnp.dot(h, wg_ref[...], preferred_element_type=F32)
        up = jnp.dot(h, wu_ref[...], preferred_element_type=F32)
        act = (jax.nn.silu(gate) * up).astype(BF16)
        down = jnp.dot(act, wd_ref[...], preferred_element_type=F32)
        o_ref[...] = (x if first else o_ref[...]) + down

    @pl.when(j == 0)
    def _():
        slab(True)

    @pl.when(j > 0)
    def _():
        slab(False)

    @pl.when(j == pl.num_programs(1) - 1)
    def _():
        o_ref[...] = _rms(o_ref[...], gf_ref[...])


def _ffn(x2d, g, w_gate_up, w_down, g_final, *, tm, tf):
    n, d = x2d.shape
    d_ff = w_down.shape[0]
    n_f = d_ff // tf
    return pl.pallas_call(
        _ffn_body,
        grid=(n // tm, n_f),
        in_specs=[
            pl.BlockSpec((tm, d), lambda i, j: (i, 0)),
            pl.BlockSpec((1, d), lambda i, j: (0, 0)),
            pl.BlockSpec((d, tf), lambda i, j: (0, j)),
            pl.BlockSpec((d, tf), lambda i, j: (0, j + n_f)),
            pl.BlockSpec((tf, d), lambda i, j: (j, 0)),
            pl.BlockSpec((1, d), lambda i, j: (0, 0)),
        ],
        out_specs=pl.BlockSpec((tm, d), lambda i, j: (i, 0)),
        out_shape=jax.ShapeDtypeStruct((n, d), F32),
        scratch_shapes=[pltpu.VMEM((tm, d), BF16)],
        compiler_params=_compiler_params(("arbitrary", "arbitrary")),
        name="ffn",
    )(x2d, g, w_gate_up, w_gate_up, w_down, g_final)


def _layer(x2d, norm_mix, w_in, rel_bias, pool_w, pool_scale, w_a, w_b, w_o,
           norm_ffn, w_gate_up, w_down, norm_out, *, batch, seq):
    d = x2d.shape[1]
    qkv, u, gates = _in_proj(x2d, norm_mix.reshape(1, d), w_in, tm=IN_PROJ_ROWS)
    pool_w2d = pool_w.reshape(-1, pool_w.shape[-1])
    ya, (pool_w2d, w_a, w_b, w_o, w_gate_up, w_down) = _attention(
        qkv, _bias_base(rel_bias), (pool_w2d, w_a, w_b, w_o, w_gate_up, w_down),
        batch=batch, seq=seq)
    x1 = _mix(ya, u, gates, x2d, pool_w2d.reshape(pool_w.shape), pool_scale.reshape(1, -1),
              w_a, w_b, w_o, seq=seq, tm=MIX_ROWS)
    return _ffn(x1, norm_ffn.reshape(1, d), w_gate_up, w_down,
                norm_out.reshape(1, d), tm=FFN_ROWS, tf=FFN_COLS)


def kernel(x, norm_mix, w_in, rel_bias, pool_w, pool_scale, w_branch_a, w_branch_b,
           w_out, norm_ffn, w_gate_up, w_down, norm_final):
    batch, seq, d = x.shape
    depth = w_in.shape[0]
    assert depth == 1, "the fused ffn kernel applies the final norm to the only layer"
    out = _layer(x.reshape(batch * seq, d), norm_mix[0], w_in[0], rel_bias[0],
                 pool_w[0], pool_scale[0], w_branch_a[0], w_branch_b[0], w_out[0],
                 norm_ffn[0], w_gate_up[0], w_down[0], norm_final,
                 batch=batch, seq=seq)
    return out.reshape(batch, seq, d)
```

```python
import functools
import math

import jax
import jax.numpy as jnp
from jax import lax
from jax.experimental import pallas as pl
from jax.experimental.pallas import tpu as pltpu

F32 = jnp.float32
BF16 = jnp.bfloat16

CHUNK = 64
LEFT_CHUNKS = 8
N_HEADS = 16
HEAD_DIM = 64
ATTN_WIDTH = N_HEADS * HEAD_DIM
MAX_REL = 128
POOL_WINDOWS = (2, 4, 8, 16)
POOL_GROUP = 256
POOL_WIDTH = POOL_GROUP * len(POOL_WINDOWS)
EPS = 1e-6
NEG_INF = -1e30

Q_ROWS = 2 * CHUNK
BAND = (LEFT_CHUNKS + 2) * CHUNK
PAD_ROWS = LEFT_CHUNKS * CHUNK
BIAS_EXT = BAND + Q_ROWS
POOL_HALO = 16

V7X_VMEM_BYTES = 64 * 1024 * 1024
VMEM_LIMIT_BYTES = V7X_VMEM_BYTES - 4 * 1024 * 1024
IN_PROJ_ROWS = 256
STAGE_ROWS = 128
MIX_ROWS = 256
FFN_ROWS = 1024
FFN_COLS = 512


def _rms(x, g):
    ms = jnp.mean(x * x, axis=-1, keepdims=True)
    return x * lax.rsqrt(ms + EPS) * g


def _compiler_params(semantics):
    return pltpu.CompilerParams(
        dimension_semantics=semantics, vmem_limit_bytes=VMEM_LIMIT_BYTES)


def _in_proj_body(x_ref, g_ref, w_hbm, qkv_ref, u_ref, gate_ref, w_ref, stage_ref, sem):
    n_chunks = w_ref.shape[0] // STAGE_ROWS

    def chunk_copy(c, slot):
        rows = pl.ds(pl.multiple_of(c * STAGE_ROWS, STAGE_ROWS), STAGE_ROWS)
        return pltpu.make_async_copy(w_hbm.at[rows], stage_ref.at[slot], sem.at[slot])

    @pl.when(pl.program_id(0) == 0)
    def _():
        chunk_copy(0, 0).start()

        def two_chunks(cc, carry):
            for slot in range(2):
                c = 2 * cc + slot

                @pl.when(c + 1 < n_chunks)
                def _():
                    chunk_copy(c + 1, 1 - slot).start()

                chunk_copy(c, slot).wait()
                rows = pl.ds(pl.multiple_of(c * STAGE_ROWS, STAGE_ROWS), STAGE_ROWS)
                w_ref[rows, :] = stage_ref[slot].astype(BF16)
            return carry

        lax.fori_loop(0, n_chunks // 2, two_chunks, 0)

    h = _rms(x_ref[...], g_ref[...]).astype(BF16)
    q_end = qkv_ref.shape[1]
    u_end = q_end + u_ref.shape[1]
    qkv_ref[...] = jnp.dot(h, w_ref[:, :q_end], preferred_element_type=F32).astype(BF16)
    u_ref[...] = jnp.dot(h, w_ref[:, q_end:u_end], preferred_element_type=F32)
    gate_ref[...] = jnp.dot(h, w_ref[:, u_end:], preferred_element_type=F32).astype(BF16)


def _in_proj(x2d, g, w, *, tm):
    n, d = x2d.shape
    in_width = w.shape[1]
    n_gate = in_width - 3 * ATTN_WIDTH - POOL_WIDTH
    assert d % (2 * STAGE_ROWS) == 0
    return pl.pallas_call(
        _in_proj_body,
        grid=(n // tm,),
        in_specs=[
            pl.BlockSpec((tm, d), lambda i: (i, 0)),
            pl.BlockSpec((1, d), lambda i: (0, 0)),
            pl.BlockSpec(memory_space=pl.ANY),
        ],
        out_specs=[
            pl.BlockSpec((tm, 3 * ATTN_WIDTH), lambda i: (i, 0)),
            pl.BlockSpec((tm, POOL_WIDTH), lambda i: (i, 0)),
            pl.BlockSpec((tm, n_gate), lambda i: (i, 0)),
        ],
        out_shape=[
            jax.ShapeDtypeStruct((n, 3 * ATTN_WIDTH), BF16),
            jax.ShapeDtypeStruct((n, POOL_WIDTH), F32),
            jax.ShapeDtypeStruct((n, n_gate), BF16),
        ],
        scratch_shapes=[
            pltpu.VMEM((d, in_width), BF16),
            pltpu.VMEM((2, STAGE_ROWS, in_width), F32),
            pltpu.SemaphoreType.DMA((2,)),
        ],
        compiler_params=_compiler_params(("arbitrary",)),
        name="in_proj",
    )(x2d, g, w)


PAIR = 2 * HEAD_DIM
N_PAIRS = N_HEADS // 2
WIN_BLOCKS = BAND // Q_ROWS
PAD_BLOCKS = PAD_ROWS // Q_ROWS
FLAT_END = PAD_ROWS - MAX_REL
BIAS_ROWS = CHUNK + BAND - FLAT_END
ONES_ROWS = 16
SUBS = 4
CAST_ROWS = 16


def _attn_body(base_ref, q_ref, k_ref, v_ref, *rest, n_cast, paired_cast):
    cast_in = rest[:n_cast]
    o_ref = rest[n_cast]
    cast_out = rest[n_cast + 1:2 * n_cast + 1]
    kpad, vtpad, bias_ref, s_ref, p_ref = rest[2 * n_cast + 1:]
    b = pl.program_id(0)
    p = pl.program_id(1)

    @pl.when((b == 0) & (p == 0))
    def _():
        key = lax.broadcasted_iota(jnp.int32, (BAND, 2 * Q_ROWS), 0)
        qry = lax.broadcasted_iota(jnp.int32, (BAND, 2 * Q_ROWS), 1) % Q_ROWS
        lo = (qry // CHUNK) * CHUNK
        in_band = (key >= lo) & (key < lo + (LEFT_CHUNKS + 1) * CHUNK)
        for hp in range(N_PAIRS):
            halves = []
            for a in range(2):
                h = 2 * hp + a
                base = jnp.broadcast_to(base_ref[h:h + 1, :], (BAND, BIAS_EXT))
                rot = pltpu.roll(base, 0, 1, stride=1, stride_axis=0)
                halves.append(rot[:, :Q_ROWS] - base[:, 0:1])
            full = jnp.where(in_band, jnp.concatenate(halves, axis=1), NEG_INF)
            bias_ref[hp, 0:CHUNK] = full[0:CHUNK]
            bias_ref[hp, CHUNK:] = full[FLAT_END:]
        kpad[0:PAD_ROWS, :] = jnp.zeros((PAD_ROWS, ATTN_WIDTH), BF16)
        vtpad[0:PAD_BLOCKS, :, 0:PAIR, :] = jnp.zeros(
            (PAD_BLOCKS, N_PAIRS, PAIR, Q_ROWS), BF16)
        vtpad[:, :, PAIR:, :] = jnp.ones(
            (vtpad.shape[0], N_PAIRS, ONES_ROWS, Q_ROWS), BF16)

    @pl.when(p == 0)
    def _():
        kpad[PAD_ROWS:, :] = k_ref[...]
        for c in range(v_ref.shape[0] // Q_ROWS):
            blk_t = v_ref[c * Q_ROWS:(c + 1) * Q_ROWS, :].astype(F32).T.astype(BF16)
            for hp in range(N_PAIRS):
                vtpad[PAD_BLOCKS + c, hp, 0:PAIR, :] = blk_t[hp * PAIR:(hp + 1) * PAIR, :]

    lane = lax.broadcasted_iota(jnp.int32, (Q_ROWS, PAIR), 1)
    first_head = lane < HEAD_DIM
    row = lax.broadcasted_iota(jnp.int32, (PAIR, Q_ROWS), 0)
    first_head_rows = row < HEAD_DIM
    scale = jnp.asarray(1.0 / math.sqrt(HEAD_DIM), BF16)

    def step(mask_missing_keys):
        units = [(sub, hp) for sub in range(SUBS) for hp in range(N_PAIRS)]

        def scores(u):
            sub, hp = units[u]
            blk = p * SUBS + sub
            lanes = slice(hp * PAIR, (hp + 1) * PAIR)
            q2 = q_ref[sub * Q_ROWS:(sub + 1) * Q_ROWS, lanes] * scale
            q2t = q2.astype(F32).T.astype(BF16)
            zero = jnp.zeros_like(q2t)
            qm_t = jnp.concatenate([jnp.where(first_head_rows, q2t, zero),
                                    jnp.where(first_head_rows, zero, q2t)], axis=1)
            k2 = kpad[pl.ds(pl.multiple_of(blk * Q_ROWS, Q_ROWS), BAND), lanes]
            s = jnp.dot(k2, qm_t, preferred_element_type=F32)
            parts = [(0, CHUNK, s[0:CHUNK] + bias_ref[hp, 0:CHUNK]),
                     (CHUNK, FLAT_END, s[CHUNK:FLAT_END]),
                     (FLAT_END, BAND, s[FLAT_END:] + bias_ref[hp, CHUNK:])]
            for r0, r1, part in parts:
                if mask_missing_keys:
                    key = r0 + lax.broadcasted_iota(jnp.int32, part.shape, 0)
                    part = jnp.where(key >= PAD_ROWS - blk * Q_ROWS, part, NEG_INF)
                s_ref[u % 2, r0:r1] = part

        def weights(u):
            slot = u % 2
            m = jnp.max(s_ref[slot], axis=0, keepdims=True)
            p_ref[slot] = jnp.exp(s_ref[slot] - m).astype(BF16)

        def values(u):
            sub, hp = units[u]
            blk = p * SUBS + sub
            vt2 = jnp.concatenate(
                [vtpad[blk + t, hp] for t in range(WIN_BLOCKS)], axis=1)
            o_t = jnp.dot(vt2, p_ref[u % 2], preferred_element_type=F32)
            o_t = o_t[0:PAIR] / o_t[PAIR:PAIR + 1]
            own = jnp.where(first_head_rows, o_t[:, :Q_ROWS], o_t[:, Q_ROWS:])
            o_ref[sub * Q_ROWS:(sub + 1) * Q_ROWS, hp * PAIR:(hp + 1) * PAIR] = (
                own.T.astype(BF16))

        casts = [(k, r0) for k, src in enumerate(cast_in)
                 for r0 in range(0, src.shape[0], CAST_ROWS)]
        casts_per_unit = pl.cdiv(len(casts), len(units))

        def cast_slab(k, r0):
            src, dst = cast_in[k], cast_out[k]
            rows = slice(r0, r0 + CAST_ROWS)
            if k != paired_cast:
                dst[rows, :] = src[rows, :].astype(BF16)
                return
            half = src.shape[1] // 2
            for j in range(half // FFN_COLS):
                for part in range(2):
                    c_src = part * half + j * FFN_COLS
                    c_dst = (2 * j + part) * FFN_COLS
                    dst[rows, c_dst:c_dst + FFN_COLS] = (
                        src[rows, c_src:c_src + FFN_COLS].astype(BF16))

        scores(0)
        for u in range(len(units) + 1):
            if u + 1 < len(units):
                scores(u + 1)
            if u >= 1:
                values(u - 1)
            if u < len(units):
                weights(u)
                for k, r0 in casts[u * casts_per_unit:(u + 1) * casts_per_unit]:
                    cast_slab(k, r0)

    @pl.when(p < PAD_BLOCKS // SUBS)
    def _():
        step(True)

    @pl.when(p >= PAD_BLOCKS // SUBS)
    def _():
        step(False)


def _attention(qkv, bias_base, cast_weights, *, paired_cast, batch, seq):
    n = batch * seq
    rows = SUBS * Q_ROWS
    steps = seq // rows
    n_steps = batch * steps
    cast_blocks = []
    for w in cast_weights:
        assert w.ndim == 2 and w.shape[0] % (n_steps * CAST_ROWS) == 0, w.shape
        cast_blocks.append(pl.BlockSpec((w.shape[0] // n_steps, w.shape[1]),
                                        lambda b, p: (b * steps + p, 0)))
    outs = pl.pallas_call(
        functools.partial(_attn_body, n_cast=len(cast_weights), paired_cast=paired_cast),
        grid=(batch, steps),
        in_specs=[
            pl.BlockSpec((N_HEADS, BIAS_EXT), lambda b, p: (0, 0)),
            pl.BlockSpec((rows, ATTN_WIDTH), lambda b, p: (b * steps + p, 0)),
            pl.BlockSpec((seq, ATTN_WIDTH), lambda b, p: (b, 1)),
            pl.BlockSpec((seq, ATTN_WIDTH), lambda b, p: (b, 2)),
        ] + cast_blocks,
        out_specs=[pl.BlockSpec((rows, ATTN_WIDTH), lambda b, p: (b * steps + p, 0))]
        + cast_blocks,
        out_shape=[jax.ShapeDtypeStruct((n, ATTN_WIDTH), BF16)]
        + [jax.ShapeDtypeStruct(w.shape, BF16) for w in cast_weights],
        scratch_shapes=[
            pltpu.VMEM((PAD_ROWS + seq, ATTN_WIDTH), BF16),
            pltpu.VMEM((PAD_BLOCKS + seq // Q_ROWS, N_PAIRS, PAIR + ONES_ROWS, Q_ROWS), BF16),
            pltpu.VMEM((N_PAIRS, BIAS_ROWS, 2 * Q_ROWS), F32),
            pltpu.VMEM((2, BAND, 2 * Q_ROWS), F32),
            pltpu.VMEM((2, BAND, 2 * Q_ROWS), BF16),
        ],
        compiler_params=_compiler_params(("arbitrary", "arbitrary")),
        name="attn",
    )(bias_base, qkv, qkv, qkv, *cast_weights)
    return outs[0], outs[1:]


def _bias_base(rel_bias):
    far = rel_bias[:, -1:]
    head = jnp.broadcast_to(far, (N_HEADS, Q_ROWS))
    tail = jnp.broadcast_to(far, (N_HEADS, BIAS_EXT - Q_ROWS - 2 * MAX_REL - 1))
    return jnp.concatenate([head, rel_bias, tail], axis=1)


def _mix_body(ya_ref, u_ref, gates_ref, x_ref, pw_ref, ps_ref,
              pa_ref, pb_ref, wo_ref, o_ref, ext_ref, yb_ref, a_ref, *, tiles_per_seq):
    i = pl.program_id(0)
    tm, d = x_ref.shape
    tile_in_seq = i % tiles_per_seq
    n_groups = len(POOL_WINDOWS)
    a_cols = d // n_groups

    @pl.when(tile_in_seq == 0)
    def _():
        ext_ref[0:POOL_HALO, :] = jnp.zeros((POOL_HALO, POOL_WIDTH), F32)

    @pl.when(tile_in_seq != 0)
    def _():
        ext_ref[0:POOL_HALO, :] = ext_ref[tm:, :]

    ext_ref[POOL_HALO:, :] = u_ref[...]

    pos = tile_in_seq * tm + lax.broadcasted_iota(jnp.int32, (tm, 1), 0)
    for g, w in enumerate(POOL_WINDOWS):
        acols = slice(g * a_cols, (g + 1) * a_cols)
        a_ref[:, acols] = jnp.dot(ya_ref[...], pa_ref[:, acols],
                                  preferred_element_type=F32)
        cols = slice(g * POOL_GROUP, (g + 1) * POOL_GROUP)
        acc = ext_ref[:, cols]
        shift = 1
        while shift < w:
            acc = acc + pltpu.roll(acc, shift, 0)
            shift *= 2
        acc = acc[POOL_HALO:]
        cur = ext_ref[POOL_HALO:, cols]
        cnt = jnp.minimum(pos + 1, w).astype(F32)
        dev = (acc / cnt - cur).astype(BF16)
        y = jnp.dot(dev, pw_ref[g], preferred_element_type=F32)
        yb_ref[:, cols] = (y * ps_ref[:, cols]).astype(BF16)

    bb = jnp.dot(yb_ref[...], pb_ref[...], preferred_element_type=F32)
    gate_a = jax.nn.sigmoid(gates_ref[:, :d].astype(F32))
    gate_b = jax.nn.sigmoid(gates_ref[:, d:].astype(F32))
    merged = gate_a * a_ref[...] + gate_b * bb
    y = jnp.dot(merged.astype(BF16), wo_ref[...], preferred_element_type=F32)
    o_ref[...] = x_ref[...] + y


def _mix(ya, u, gates, x2d, pool_w, pool_scale, w_a, w_b, w_o, *, seq, tm):
    n, d = x2d.shape
    tiles_per_seq = seq // tm
    const = lambda *shape: pl.BlockSpec(shape, lambda i: (0,) * len(shape),
                                        pipeline_mode=pl.Buffered(1))
    return pl.pallas_call(
        functools.partial(_mix_body, tiles_per_seq=tiles_per_seq),
        grid=(n // tm,),
        in_specs=[
            pl.BlockSpec((tm, ATTN_WIDTH), lambda i: (i, 0)),
            pl.BlockSpec((tm, POOL_WIDTH), lambda i: (i, 0)),
            pl.BlockSpec((tm, 2 * d), lambda i: (i, 0)),
            pl.BlockSpec((tm, d), lambda i: (i, 0)),
            const(len(POOL_WINDOWS), POOL_GROUP, POOL_GROUP),
            const(1, POOL_WIDTH),
            const(ATTN_WIDTH, d),
            const(POOL_WIDTH, d),
            const(d, d),
        ],
        out_specs=pl.BlockSpec((tm, d), lambda i: (i, 0)),
        out_shape=jax.ShapeDtypeStruct((n, d), F32),
        scratch_shapes=[
            pltpu.VMEM((POOL_HALO + tm, POOL_WIDTH), F32),
            pltpu.VMEM((tm, POOL_WIDTH), BF16),
            pltpu.VMEM((tm, d), F32),
        ],
        compiler_params=_compiler_params(("arbitrary",)),
        name="mix",
    )(ya, u, gates, x2d, pool_w, pool_scale, w_a, w_b, w_o)


def _ffn_body(x_ref, g_ref, wgu_ref, wd_ref, gf_ref, o_ref, h_ref):
    j = pl.program_id(1)

    def slab(first):
        if first:
            x = x_ref[...]
            h = _rms(x, g_ref[...]).astype(BF16)
            h_ref[...] = h
        else:
            h = h_ref[...]
        tf = wd_ref.shape[0]
        gate = jnp.dot(h, wgu_ref[:, :tf], preferred_element_type=F32)
        up = jnp.dot(h, wgu_ref[:, tf:], preferred_element_type=F32)
        act = (jax.nn.silu(gate) * up).astype(BF16)
        down = jnp.dot(act, wd_ref[...], preferred_element_type=F32)
        o_ref[...] = (x if first else o_ref[...]) + down

    @pl.when(j == 0)
    def _():
        slab(True)

    @pl.when(j > 0)
    def _():
        slab(False)

    @pl.when(j == pl.num_programs(1) - 1)
    def _():
        o_ref[...] = _rms(o_ref[...], gf_ref[...])


def _ffn(x2d, g, w_gate_up, w_down, g_final, *, tm, tf):
    n, d = x2d.shape
    d_ff = w_down.shape[0]
    n_f = d_ff // tf
    return pl.pallas_call(
        _ffn_body,
        grid=(n // tm, n_f),
        in_specs=[
            pl.BlockSpec((tm, d), lambda i, j: (i, 0)),
            pl.BlockSpec((1, d), lambda i, j: (0, 0)),
            pl.BlockSpec((d, 2 * tf), lambda i, j: (0, j)),
            pl.BlockSpec((tf, d), lambda i, j: (j, 0)),
            pl.BlockSpec((1, d), lambda i, j: (0, 0)),
        ],
        out_specs=pl.BlockSpec((tm, d), lambda i, j: (i, 0)),
        out_shape=jax.ShapeDtypeStruct((n, d), F32),
        scratch_shapes=[pltpu.VMEM((tm, d), BF16)],
        compiler_params=_compiler_params(("arbitrary", "arbitrary")),
        name="ffn",
    )(x2d, g, w_gate_up, w_down, g_final)


def _layer(x2d, norm_mix, w_in, rel_bias, pool_w, pool_scale, w_a, w_b, w_o,
           norm_ffn, w_gate_up, w_down, norm_out, *, batch, seq):
    d = x2d.shape[1]
    qkv, u, gates = _in_proj(x2d, norm_mix.reshape(1, d), w_in, tm=IN_PROJ_ROWS)
    pool_w2d = pool_w.reshape(-1, pool_w.shape[-1])
    ya, (pool_w2d, w_a, w_b, w_o, w_gate_up, w_down) = _attention(
        qkv, _bias_base(rel_bias), (pool_w2d, w_a, w_b, w_o, w_gate_up, w_down),
        paired_cast=4, batch=batch, seq=seq)
    x1 = _mix(ya, u, gates, x2d, pool_w2d.reshape(pool_w.shape), pool_scale.reshape(1, -1),
              w_a, w_b, w_o, seq=seq, tm=MIX_ROWS)
    return _ffn(x1, norm_ffn.reshape(1, d), w_gate_up, w_down,
                norm_out.reshape(1, d), tm=FFN_ROWS, tf=FFN_COLS)


def kernel(x, norm_mix, w_in, rel_bias, pool_w, pool_scale, w_branch_a, w_branch_b,
           w_out, norm_ffn, w_gate_up, w_down, norm_final):
    batch, seq, d = x.shape
    depth = w_in.shape[0]
    assert depth == 1, "the fused ffn kernel applies the final norm to the only layer"
    out = _layer(x.reshape(batch * seq, d), norm_mix[0], w_in[0], rel_bias[0],
                 pool_w[0], pool_scale[0], w_branch_a[0], w_branch_b[0], w_out[0],
                 norm_ffn[0], w_gate_up[0], w_down[0], norm_final,
                 batch=batch, seq=seq)
    return out.reshape(batch, seq, d)
```

```python
import functools
import math

import jax
import jax.numpy as jnp
from jax import lax
from jax.experimental import pallas as pl
from jax.experimental.pallas import tpu as pltpu

F32 = jnp.float32
BF16 = jnp.bfloat16

CHUNK = 64
LEFT_CHUNKS = 8
N_HEADS = 16
HEAD_DIM = 64
ATTN_WIDTH = N_HEADS * HEAD_DIM
MAX_REL = 128
POOL_WINDOWS = (2, 4, 8, 16)
POOL_GROUP = 256
POOL_WIDTH = POOL_GROUP * len(POOL_WINDOWS)
EPS = 1e-6
NEG_INF = -1e30

Q_ROWS = 2 * CHUNK
BAND = (LEFT_CHUNKS + 2) * CHUNK
PAD_ROWS = LEFT_CHUNKS * CHUNK
BIAS_EXT = BAND + Q_ROWS
POOL_HALO = 16

V7X_VMEM_BYTES = 64 * 1024 * 1024
VMEM_LIMIT_BYTES = V7X_VMEM_BYTES - 4 * 1024 * 1024
IN_PROJ_ROWS = 256
STAGE_ROWS = 128
MIX_ROWS = 256
FFN_ROWS = 1024
FFN_COLS = 512


def _rms(x, g):
    ms = jnp.mean(x * x, axis=-1, keepdims=True)
    return x * lax.rsqrt(ms + EPS) * g


def _sigmoid(x):
    return 0.5 * jnp.tanh(0.5 * x) + 0.5


def _compiler_params(semantics):
    return pltpu.CompilerParams(
        dimension_semantics=semantics, vmem_limit_bytes=VMEM_LIMIT_BYTES)


def _in_proj_body(x_ref, g_ref, w_hbm, qkv_ref, u_ref, gate_ref, w_ref, stage_ref, sem):
    n_chunks = w_ref.shape[0] // STAGE_ROWS

    def chunk_copy(c, slot):
        rows = pl.ds(pl.multiple_of(c * STAGE_ROWS, STAGE_ROWS), STAGE_ROWS)
        return pltpu.make_async_copy(w_hbm.at[rows], stage_ref.at[slot], sem.at[slot])

    @pl.when(pl.program_id(0) == 0)
    def _():
        chunk_copy(0, 0).start()

        def two_chunks(cc, carry):
            for slot in range(2):
                c = 2 * cc + slot

                @pl.when(c + 1 < n_chunks)
                def _():
                    chunk_copy(c + 1, 1 - slot).start()

                chunk_copy(c, slot).wait()
                rows = pl.ds(pl.multiple_of(c * STAGE_ROWS, STAGE_ROWS), STAGE_ROWS)
                w_ref[rows, :] = stage_ref[slot].astype(BF16)
            return carry

        lax.fori_loop(0, n_chunks // 2, two_chunks, 0)

    h = _rms(x_ref[...], g_ref[...]).astype(BF16)
    q_end = qkv_ref.shape[1]
    u_end = q_end + u_ref.shape[1]
    qkv_ref[...] = jnp.dot(h, w_ref[:, :q_end], preferred_element_type=F32).astype(BF16)
    u_ref[...] = jnp.dot(h, w_ref[:, q_end:u_end], preferred_element_type=F32)
    gate_ref[...] = jnp.dot(h, w_ref[:, u_end:], preferred_element_type=F32).astype(BF16)


def _in_proj(x2d, g, w, *, tm):
    n, d = x2d.shape
    in_width = w.shape[1]
    n_gate = in_width - 3 * ATTN_WIDTH - POOL_WIDTH
    assert d % (2 * STAGE_ROWS) == 0
    return pl.pallas_call(
        _in_proj_body,
        grid=(n // tm,),
        in_specs=[
            pl.BlockSpec((tm, d), lambda i: (i, 0)),
            pl.BlockSpec((1, d), lambda i: (0, 0)),
            pl.BlockSpec(memory_space=pl.ANY),
        ],
        out_specs=[
            pl.BlockSpec((tm, 3 * ATTN_WIDTH), lambda i: (i, 0)),
            pl.BlockSpec((tm, POOL_WIDTH), lambda i: (i, 0)),
            pl.BlockSpec((tm, n_gate), lambda i: (i, 0)),
        ],
        out_shape=[
            jax.ShapeDtypeStruct((n, 3 * ATTN_WIDTH), BF16),
            jax.ShapeDtypeStruct((n, POOL_WIDTH), F32),
            jax.ShapeDtypeStruct((n, n_gate), BF16),
        ],
        scratch_shapes=[
            pltpu.VMEM((d, in_width), BF16),
            pltpu.VMEM((2, STAGE_ROWS, in_width), F32),
            pltpu.SemaphoreType.DMA((2,)),
        ],
        compiler_params=_compiler_params(("arbitrary",)),
        name="in_proj",
    )(x2d, g, w)


PAIR = 2 * HEAD_DIM
N_PAIRS = N_HEADS // 2
WIN_BLOCKS = BAND // Q_ROWS
PAD_BLOCKS = PAD_ROWS // Q_ROWS
FLAT_END = PAD_ROWS - MAX_REL
BIAS_ROWS = CHUNK + BAND - FLAT_END
ONES_ROWS = 16
SUBS = 4
CAST_ROWS = 16


def _attn_body(base_ref, q_ref, k_ref, v_ref, *rest, n_cast, paired_cast):
    cast_in = rest[:n_cast]
    o_ref = rest[n_cast]
    cast_out = rest[n_cast + 1:2 * n_cast + 1]
    kpad, vtpad, bias_ref, s_ref, p_ref = rest[2 * n_cast + 1:]
    b = pl.program_id(0)
    p = pl.program_id(1)

    @pl.when((b == 0) & (p == 0))
    def _():
        key = lax.broadcasted_iota(jnp.int32, (BAND, 2 * Q_ROWS), 0)
        qry = lax.broadcasted_iota(jnp.int32, (BAND, 2 * Q_ROWS), 1) % Q_ROWS
        lo = (qry // CHUNK) * CHUNK
        in_band = (key >= lo) & (key < lo + (LEFT_CHUNKS + 1) * CHUNK)
        for hp in range(N_PAIRS):
            halves = []
            for a in range(2):
                h = 2 * hp + a
                base = jnp.broadcast_to(base_ref[h:h + 1, :], (BAND, BIAS_EXT))
                rot = pltpu.roll(base, 0, 1, stride=1, stride_axis=0)
                halves.append(rot[:, :Q_ROWS] - base[:, 0:1])
            full = jnp.where(in_band, jnp.concatenate(halves, axis=1), NEG_INF)
            bias_ref[hp, 0:CHUNK] = full[0:CHUNK]
            bias_ref[hp, CHUNK:] = full[FLAT_END:]
        kpad[0:PAD_ROWS, :] = jnp.zeros((PAD_ROWS, ATTN_WIDTH), BF16)
        vtpad[0:PAD_BLOCKS, :, 0:PAIR, :] = jnp.zeros(
            (PAD_BLOCKS, N_PAIRS, PAIR, Q_ROWS), BF16)
        vtpad[:, :, PAIR:, :] = jnp.ones(
            (vtpad.shape[0], N_PAIRS, ONES_ROWS, Q_ROWS), BF16)

    @pl.when(p == 0)
    def _():
        kpad[PAD_ROWS:, :] = k_ref[...]
        for c in range(v_ref.shape[0] // Q_ROWS):
            blk_t = v_ref[c * Q_ROWS:(c + 1) * Q_ROWS, :].astype(F32).T.astype(BF16)
            for hp in range(N_PAIRS):
                vtpad[PAD_BLOCKS + c, hp, 0:PAIR, :] = blk_t[hp * PAIR:(hp + 1) * PAIR, :]

    lane = lax.broadcasted_iota(jnp.int32, (Q_ROWS, PAIR), 1)
    first_head = lane < HEAD_DIM
    row = lax.broadcasted_iota(jnp.int32, (PAIR, Q_ROWS), 0)
    first_head_rows = row < HEAD_DIM
    scale = jnp.asarray(1.0 / math.sqrt(HEAD_DIM), BF16)

    def step(mask_missing_keys):
        units = [(sub, hp) for sub in range(SUBS) for hp in range(N_PAIRS)]

        def scores(u):
            sub, hp = units[u]
            blk = p * SUBS + sub
            lanes = slice(hp * PAIR, (hp + 1) * PAIR)
            q2 = q_ref[sub * Q_ROWS:(sub + 1) * Q_ROWS, lanes] * scale
            q2t = q2.astype(F32).T.astype(BF16)
            zero = jnp.zeros_like(q2t)
            qm_t = jnp.concatenate([jnp.where(first_head_rows, q2t, zero),
                                    jnp.where(first_head_rows, zero, q2t)], axis=1)
            k2 = kpad[pl.ds(pl.multiple_of(blk * Q_ROWS, Q_ROWS), BAND), lanes]
            s = jnp.dot(k2, qm_t, preferred_element_type=F32)
            parts = [(0, CHUNK, s[0:CHUNK] + bias_ref[hp, 0:CHUNK]),
                     (CHUNK, FLAT_END, s[CHUNK:FLAT_END]),
                     (FLAT_END, BAND, s[FLAT_END:] + bias_ref[hp, CHUNK:])]
            for r0, r1, part in parts:
                if mask_missing_keys:
                    key = r0 + lax.broadcasted_iota(jnp.int32, part.shape, 0)
                    part = jnp.where(key >= PAD_ROWS - blk * Q_ROWS, part, NEG_INF)
                s_ref[u % 2, r0:r1] = part

        def weights(u):
            slot = u % 2
            m = jnp.max(s_ref[slot], axis=0, keepdims=True)
            p_ref[slot] = jnp.exp(s_ref[slot] - m).astype(BF16)

        def values(u):
            sub, hp = units[u]
            blk = p * SUBS + sub
            vt2 = jnp.concatenate(
                [vtpad[blk + t, hp] for t in range(WIN_BLOCKS)], axis=1)
            o_t = jnp.dot(vt2, p_ref[u % 2], preferred_element_type=F32)
            o_t = o_t[0:PAIR] / o_t[PAIR:PAIR + 1]
            own = jnp.where(first_head_rows, o_t[:, :Q_ROWS], o_t[:, Q_ROWS:])
            o_ref[sub * Q_ROWS:(sub + 1) * Q_ROWS, hp * PAIR:(hp + 1) * PAIR] = (
                own.T.astype(BF16))

        casts = [(k, r0) for k, src in enumerate(cast_in)
                 for r0 in range(0, src.shape[0], CAST_ROWS)]
        casts_per_unit = pl.cdiv(len(casts), len(units))

        def cast_slab(k, r0):
            src, dst = cast_in[k], cast_out[k]
            rows = slice(r0, r0 + CAST_ROWS)
            if k != paired_cast:
                dst[rows, :] = src[rows, :].astype(BF16)
                return
            half = src.shape[1] // 2
            for j in range(half // FFN_COLS):
                for part in range(2):
                    c_src = part * half + j * FFN_COLS
                    c_dst = (2 * j + part) * FFN_COLS
                    dst[rows, c_dst:c_dst + FFN_COLS] = (
                        src[rows, c_src:c_src + FFN_COLS].astype(BF16))

        scores(0)
        for u in range(len(units) + 1):
            if u + 1 < len(units):
                scores(u + 1)
            if u >= 1:
                values(u - 1)
            if u < len(units):
                weights(u)
                for k, r0 in casts[u * casts_per_unit:(u + 1) * casts_per_unit]:
                    cast_slab(k, r0)

    @pl.when(p < PAD_BLOCKS // SUBS)
    def _():
        step(True)

    @pl.when(p >= PAD_BLOCKS // SUBS)
    def _():
        step(False)


def _attention(qkv, bias_base, cast_weights, *, paired_cast, batch, seq):
    n = batch * seq
    rows = SUBS * Q_ROWS
    steps = seq // rows
    n_steps = batch * steps
    cast_blocks = []
    for w in cast_weights:
        assert w.ndim == 2 and w.shape[0] % (n_steps * CAST_ROWS) == 0, w.shape
        cast_blocks.append(pl.BlockSpec((w.shape[0] // n_steps, w.shape[1]),
                                        lambda b, p: (b * steps + p, 0)))
    outs = pl.pallas_call(
        functools.partial(_attn_body, n_cast=len(cast_weights), paired_cast=paired_cast),
        grid=(batch, steps),
        in_specs=[
            pl.BlockSpec((N_HEADS, BIAS_EXT), lambda b, p: (0, 0)),
            pl.BlockSpec((rows, ATTN_WIDTH), lambda b, p: (b * steps + p, 0)),
            pl.BlockSpec((seq, ATTN_WIDTH), lambda b, p: (b, 1)),
            pl.BlockSpec((seq, ATTN_WIDTH), lambda b, p: (b, 2)),
        ] + cast_blocks,
        out_specs=[pl.BlockSpec((rows, ATTN_WIDTH), lambda b, p: (b * steps + p, 0))]
        + cast_blocks,
        out_shape=[jax.ShapeDtypeStruct((n, ATTN_WIDTH), BF16)]
        + [jax.ShapeDtypeStruct(w.shape, BF16) for w in cast_weights],
        scratch_shapes=[
            pltpu.VMEM((PAD_ROWS + seq, ATTN_WIDTH), BF16),
            pltpu.VMEM((PAD_BLOCKS + seq // Q_ROWS, N_PAIRS, PAIR + ONES_ROWS, Q_ROWS), BF16),
            pltpu.VMEM((N_PAIRS, BIAS_ROWS, 2 * Q_ROWS), F32),
            pltpu.VMEM((2, BAND, 2 * Q_ROWS), F32),
            pltpu.VMEM((2, BAND, 2 * Q_ROWS), BF16),
        ],
        compiler_params=_compiler_params(("arbitrary", "arbitrary")),
        name="attn",
    )(bias_base, qkv, qkv, qkv, *cast_weights)
    return outs[0], outs[1:]


def _bias_base(rel_bias):
    far = rel_bias[:, -1:]
    head = jnp.broadcast_to(far, (N_HEADS, Q_ROWS))
    tail = jnp.broadcast_to(far, (N_HEADS, BIAS_EXT - Q_ROWS - 2 * MAX_REL - 1))
    return jnp.concatenate([head, rel_bias, tail], axis=1)


def _mix_body(ya_ref, u_ref, gates_ref, x_ref, pw_ref, ps_ref,
              pa_ref, pb_ref, wo_ref, o_ref, ext_ref, yb_ref, a_ref, *, tiles_per_seq):
    i = pl.program_id(0)
    tm, d = x_ref.shape
    tile_in_seq = i % tiles_per_seq
    n_groups = len(POOL_WINDOWS)
    a_cols = d // n_groups

    @pl.when(tile_in_seq == 0)
    def _():
        ext_ref[0:POOL_HALO, :] = jnp.zeros((POOL_HALO, POOL_WIDTH), F32)

    @pl.when(tile_in_seq != 0)
    def _():
        ext_ref[0:POOL_HALO, :] = ext_ref[tm:, :]

    ext_ref[POOL_HALO:, :] = u_ref[...]

    pos = tile_in_seq * tm + lax.broadcasted_iota(jnp.int32, (tm, 1), 0)
    for g, w in enumerate(POOL_WINDOWS):
        acols = slice(g * a_cols, (g + 1) * a_cols)
        a_ref[:, acols] = jnp.dot(ya_ref[...], pa_ref[:, acols],
                                  preferred_element_type=F32)
        cols = slice(g * POOL_GROUP, (g + 1) * POOL_GROUP)
        acc = ext_ref[:, cols]
        shift = 1
        while shift < w:
            acc = acc + pltpu.roll(acc, shift, 0)
            shift *= 2
        acc = acc[POOL_HALO:]
        cur = ext_ref[POOL_HALO:, cols]
        cnt = jnp.minimum(pos + 1, w).astype(F32)
        dev = (acc / cnt - cur).astype(BF16)
        y = jnp.dot(dev, pw_ref[g], preferred_element_type=F32)
        yb_ref[:, cols] = (y * ps_ref[:, cols]).astype(BF16)

    bb = jnp.dot(yb_ref[...], pb_ref[...], preferred_element_type=F32)
    gate_a = _sigmoid(gates_ref[:, :d].astype(F32))
    gate_b = _sigmoid(gates_ref[:, d:].astype(F32))
    merged = gate_a * a_ref[...] + gate_b * bb
    y = jnp.dot(merged.astype(BF16), wo_ref[...], preferred_element_type=F32)
    o_ref[...] = x_ref[...] + y


def _mix(ya, u, gates, x2d, pool_w, pool_scale, w_a, w_b, w_o, *, seq, tm):
    n, d = x2d.shape
    tiles_per_seq = seq // tm
    const = lambda *shape: pl.BlockSpec(shape, lambda i: (0,) * len(shape),
                                        pipeline_mode=pl.Buffered(1))
    return pl.pallas_call(
        functools.partial(_mix_body, tiles_per_seq=tiles_per_seq),
        grid=(n // tm,),
        in_specs=[
            pl.BlockSpec((tm, ATTN_WIDTH), lambda i: (i, 0)),
            pl.BlockSpec((tm, POOL_WIDTH), lambda i: (i, 0)),
            pl.BlockSpec((tm, 2 * d), lambda i: (i, 0)),
            pl.BlockSpec((tm, d), lambda i: (i, 0)),
            const(len(POOL_WINDOWS), POOL_GROUP, POOL_GROUP),
            const(1, POOL_WIDTH),
            const(ATTN_WIDTH, d),
            const(POOL_WIDTH, d),
            const(d, d),
        ],
        out_specs=pl.BlockSpec((tm, d), lambda i: (i, 0)),
        out_shape=jax.ShapeDtypeStruct((n, d), F32),
        scratch_shapes=[
            pltpu.VMEM((POOL_HALO + tm, POOL_WIDTH), F32),
            pltpu.VMEM((tm, POOL_WIDTH), BF16),
            pltpu.VMEM((tm, d), F32),
        ],
        compiler_params=_compiler_params(("arbitrary",)),
        name="mix",
    )(ya, u, gates, x2d, pool_w, pool_scale, w_a, w_b, w_o)


def _ffn_body(x_ref, g_ref, wgu_ref, wd_ref, gf_ref, o_ref, h_ref):
    j = pl.program_id(1)

    def slab(first):
        if first:
            x = x_ref[...]
            h = _rms(x, g_ref[...]).astype(BF16)
            h_ref[...] = h
        else:
            h = h_ref[...]
        tf = wd_ref.shape[0]
        gate = jnp.dot(h, wgu_ref[:, :tf], preferred_element_type=F32)
        up = jnp.dot(h, wgu_ref[:, tf:], preferred_element_type=F32)
        act = (gate * _sigmoid(gate) * up).astype(BF16)
        down = jnp.dot(act, wd_ref[...], preferred_element_type=F32)
        o_ref[...] = (x if first else o_ref[...]) + down

    @pl.when(j == 0)
    def _():
        slab(True)

    @pl.when(j > 0)
    def _():
        slab(False)

    @pl.when(j == pl.num_programs(1) - 1)
    def _():
        o_ref[...] = _rms(o_ref[...], gf_ref[...])


def _ffn(x2d, g, w_gate_up, w_down, g_final, *, tm, tf):
    n, d = x2d.shape
    d_ff = w_down.shape[0]
    n_f = d_ff // tf
    return pl.pallas_call(
        _ffn_body,
        grid=(n // tm, n_f),
        in_specs=[
            pl.BlockSpec((tm, d), lambda i, j: (i, 0)),
            pl.BlockSpec((1, d), lambda i, j: (0, 0)),
            pl.BlockSpec((d, 2 * tf), lambda i, j: (0, j)),
            pl.BlockSpec((tf, d), lambda i, j: (j, 0)),
            pl.BlockSpec((1, d), lambda i, j: (0, 0)),
        ],
        out_specs=pl.BlockSpec((tm, d), lambda i, j: (i, 0)),
        out_shape=jax.ShapeDtypeStruct((n, d), F32),
        scratch_shapes=[pltpu.VMEM((tm, d), BF16)],
        compiler_params=_compiler_params(("arbitrary", "arbitrary")),
        name="ffn",
    )(x2d, g, w_gate_up, w_down, g_final)


def _layer(x2d, norm_mix, w_in, rel_bias, pool_w, pool_scale, w_a, w_b, w_o,
           norm_ffn, w_gate_up, w_down, norm_out, *, batch, seq):
    d = x2d.shape[1]
    qkv, u, gates = _in_proj(x2d, norm_mix.reshape(1, d), w_in, tm=IN_PROJ_ROWS)
    pool_w2d = pool_w.reshape(-1, pool_w.shape[-1])
    ya, (pool_w2d, w_a, w_b, w_o, w_gate_up, w_down) = _attention(
        qkv, _bias_base(rel_bias), (pool_w2d, w_a, w_b, w_o, w_gate_up, w_down),
        paired_cast=4, batch=batch, seq=seq)
    x1 = _mix(ya, u, gates, x2d, pool_w2d.reshape(pool_w.shape), pool_scale.reshape(1, -1),
              w_a, w_b, w_o, seq=seq, tm=MIX_ROWS)
    return _ffn(x1, norm_ffn.reshape(1, d), w_gate_up, w_down,
                norm_out.reshape(1, d), tm=FFN_ROWS, tf=FFN_COLS)


def kernel(x, norm_mix, w_in, rel_bias, pool_w, pool_scale, w_branch_a, w_branch_b,
           w_out, norm_ffn, w_gate_up, w_down, norm_final):
    batch, seq, d = x.shape
    depth = w_in.shape[0]
    assert depth == 1, "the fused ffn kernel applies the final norm to the only layer"
    out = _layer(x.reshape(batch * seq, d), norm_mix[0], w_in[0], rel_bias[0],
                 pool_w[0], pool_scale[0], w_branch_a[0], w_branch_b[0], w_out[0],
                 norm_ffn[0], w_gate_up[0], w_down[0], norm_final,
                 batch=batch, seq=seq)
    return out.reshape(batch, seq, d)
```

```python
import functools
import math

import jax
import jax.numpy as jnp
from jax import lax
from jax.experimental import pallas as pl
from jax.experimental.pallas import tpu as pltpu

F32 = jnp.float32
BF16 = jnp.bfloat16

CHUNK = 64
LEFT_CHUNKS = 8
N_HEADS = 16
HEAD_DIM = 64
ATTN_WIDTH = N_HEADS * HEAD_DIM
MAX_REL = 128
POOL_WINDOWS = (2, 4, 8, 16)
POOL_GROUP = 256
POOL_WIDTH = POOL_GROUP * len(POOL_WINDOWS)
EPS = 1e-6
NEG_INF = -1e30

Q_ROWS = 2 * CHUNK
BAND = (LEFT_CHUNKS + 2) * CHUNK
PAD_ROWS = LEFT_CHUNKS * CHUNK
BIAS_EXT = BAND + Q_ROWS
POOL_HALO = 16

V7X_VMEM_BYTES = 64 * 1024 * 1024
VMEM_LIMIT_BYTES = V7X_VMEM_BYTES - 4 * 1024 * 1024
IN_PROJ_ROWS = 256
STAGE_ROWS = 128
MIX_ROWS = 256
FFN_ROWS = 1024
FFN_COLS = 512


def _rms(x, g):
    ms = jnp.mean(x * x, axis=-1, keepdims=True)
    return x * lax.rsqrt(ms + EPS) * g


def _sigmoid(x):
    return 0.5 * jnp.tanh(0.5 * x) + 0.5


def _compiler_params(semantics):
    return pltpu.CompilerParams(
        dimension_semantics=semantics, vmem_limit_bytes=VMEM_LIMIT_BYTES)


def _in_proj_body(x_ref, g_ref, w_hbm, qkv_ref, u_ref, gate_ref, w_ref, stage_ref, sem):
    n_chunks = w_ref.shape[0] // STAGE_ROWS

    def chunk_copy(c, slot):
        rows = pl.ds(pl.multiple_of(c * STAGE_ROWS, STAGE_ROWS), STAGE_ROWS)
        return pltpu.make_async_copy(w_hbm.at[rows], stage_ref.at[slot], sem.at[slot])

    @pl.when(pl.program_id(0) == 0)
    def _():
        chunk_copy(0, 0).start()

        def two_chunks(cc, carry):
            for slot in range(2):
                c = 2 * cc + slot

                @pl.when(c + 1 < n_chunks)
                def _():
                    chunk_copy(c + 1, 1 - slot).start()

                chunk_copy(c, slot).wait()
                rows = pl.ds(pl.multiple_of(c * STAGE_ROWS, STAGE_ROWS), STAGE_ROWS)
                w_ref[rows, :] = stage_ref[slot].astype(BF16)
            return carry

        lax.fori_loop(0, n_chunks // 2, two_chunks, 0)

    h = _rms(x_ref[...], g_ref[...]).astype(BF16)
    q_end = qkv_ref.shape[1]
    u_end = q_end + u_ref.shape[1]
    qkv_ref[...] = jnp.dot(h, w_ref[:, :q_end], preferred_element_type=F32).astype(BF16)
    u_ref[...] = jnp.dot(h, w_ref[:, q_end:u_end], preferred_element_type=F32)
    gate_ref[...] = jnp.dot(h, w_ref[:, u_end:], preferred_element_type=F32).astype(BF16)


def _in_proj(x2d, g, w, *, tm):
    n, d = x2d.shape
    in_width = w.shape[1]
    n_gate = in_width - 3 * ATTN_WIDTH - POOL_WIDTH
    assert d % (2 * STAGE_ROWS) == 0
    return pl.pallas_call(
        _in_proj_body,
        grid=(n // tm,),
        in_specs=[
            pl.BlockSpec((tm, d), lambda i: (i, 0)),
            pl.BlockSpec((1, d), lambda i: (0, 0)),
            pl.BlockSpec(memory_space=pl.ANY),
        ],
        out_specs=[
            pl.BlockSpec((tm, 3 * ATTN_WIDTH), lambda i: (i, 0)),
            pl.BlockSpec((tm, POOL_WIDTH), lambda i: (i, 0)),
            pl.BlockSpec((tm, n_gate), lambda i: (i, 0)),
        ],
        out_shape=[
            jax.ShapeDtypeStruct((n, 3 * ATTN_WIDTH), BF16),
            jax.ShapeDtypeStruct((n, POOL_WIDTH), F32),
            jax.ShapeDtypeStruct((n, n_gate), BF16),
        ],
        scratch_shapes=[
            pltpu.VMEM((d, in_width), BF16),
            pltpu.VMEM((2, STAGE_ROWS, in_width), F32),
            pltpu.SemaphoreType.DMA((2,)),
        ],
        compiler_params=_compiler_params(("arbitrary",)),
        name="in_proj",
    )(x2d, g, w)


PAIR = 2 * HEAD_DIM
N_PAIRS = N_HEADS // 2
WIN_BLOCKS = BAND // Q_ROWS
PAD_BLOCKS = PAD_ROWS // Q_ROWS
FLAT_END = PAD_ROWS - MAX_REL
BIAS_ROWS = BAND - FLAT_END
SEEN = BAND - CHUNK
ONES_ROWS = 16
SUBS = 4
CAST_ROWS = 16


def _attn_body(base_ref, q_ref, k_ref, v_ref, *rest, n_cast, paired_cast):
    cast_in = rest[:n_cast]
    o_ref = rest[n_cast]
    cast_out = rest[n_cast + 1:2 * n_cast + 1]
    kpad, vtpad, bias_ref, s_ref, p_ref = rest[2 * n_cast + 1:]
    b = pl.program_id(0)
    p = pl.program_id(1)

    @pl.when((b == 0) & (p == 0))
    def _():
        for hp in range(N_PAIRS):
            heads = []
            for a in range(2):
                h = 2 * hp + a
                base = jnp.broadcast_to(base_ref[h:h + 1, :], (BAND, BIAS_EXT))
                rot = pltpu.roll(base, 0, 1, stride=1, stride_axis=0)
                heads.append(rot[FLAT_END:, :Q_ROWS] - base[FLAT_END:, 0:1])
            bias_ref[hp] = jnp.concatenate(
                [heads[a][:, c * CHUNK:(c + 1) * CHUNK] for c in range(2) for a in range(2)],
                axis=1)
        kpad[0:PAD_ROWS, :] = jnp.zeros((PAD_ROWS, ATTN_WIDTH), BF16)
        vtpad[0:PAD_BLOCKS, :, 0:PAIR, :] = jnp.zeros(
            (PAD_BLOCKS, N_PAIRS, PAIR, Q_ROWS), BF16)
        vtpad[:, :, PAIR:, :] = jnp.ones(
            (vtpad.shape[0], N_PAIRS, ONES_ROWS, Q_ROWS), BF16)
        p_ref[:, SEEN:, :Q_ROWS] = jnp.zeros((2, CHUNK, Q_ROWS), BF16)
        p_ref[:, :CHUNK, Q_ROWS:] = jnp.zeros((2, CHUNK, Q_ROWS), BF16)

    @pl.when(p == 0)
    def _():
        kpad[PAD_ROWS:, :] = k_ref[...]
        for c in range(v_ref.shape[0] // Q_ROWS):
            blk_t = v_ref[c * Q_ROWS:(c + 1) * Q_ROWS, :].astype(F32).T.astype(BF16)
            for hp in range(N_PAIRS):
                vtpad[PAD_BLOCKS + c, hp, 0:PAIR, :] = blk_t[hp * PAIR:(hp + 1) * PAIR, :]

    lane = lax.broadcasted_iota(jnp.int32, (CHUNK, PAIR), 1)
    first_head = lane < HEAD_DIM
    row = lax.broadcasted_iota(jnp.int32, (PAIR, Q_ROWS), 0)
    first_head_rows = row < HEAD_DIM
    first_chunk_lanes = lax.broadcasted_iota(jnp.int32, (PAIR, Q_ROWS), 1) < CHUNK
    scale = jnp.asarray(1.0 / math.sqrt(HEAD_DIM), BF16)

    def step(mask_missing_keys):
        units = [(sub, hp) for sub in range(SUBS) for hp in range(N_PAIRS)]

        def scores(u):
            sub, hp = units[u]
            blk = p * SUBS + sub
            lanes = slice(hp * PAIR, (hp + 1) * PAIR)
            q2 = q_ref[sub * Q_ROWS:(sub + 1) * Q_ROWS, lanes] * scale
            zero = jnp.zeros((CHUNK, PAIR), BF16)
            qm = jnp.concatenate(
                [jnp.where(first_head if a == 0 else jnp.logical_not(first_head),
                           q2[c * CHUNK:(c + 1) * CHUNK], zero)
                 for c in range(2) for a in range(2)], axis=0)
            qm_t = qm.astype(F32).T.astype(BF16)
            k2 = kpad[pl.ds(pl.multiple_of(blk * Q_ROWS, Q_ROWS), BAND), lanes]
            s = jnp.dot(k2, qm_t, preferred_element_type=F32)
            pieces = [(0, FLAT_END, 0, None), (FLAT_END, SEEN, 0, (0, SEEN - FLAT_END)),
                      (CHUNK, FLAT_END, Q_ROWS, None), (FLAT_END, BAND, Q_ROWS, (0, BIAS_ROWS))]
            for r0, r1, l0, bias_rows in pieces:
                part = s[r0:r1, l0:l0 + Q_ROWS]
                if bias_rows is not None:
                    part = part + bias_ref[hp, bias_rows[0]:bias_rows[1], l0:l0 + Q_ROWS]
                if mask_missing_keys:
                    key = r0 + lax.broadcasted_iota(jnp.int32, part.shape, 0)
                    part = jnp.where(key >= PAD_ROWS - blk * Q_ROWS, part, NEG_INF)
                s_ref[u % 2, r0:r1, l0:l0 + Q_ROWS] = part

        def weights(u):
            slot = u % 2
            for r0, l0 in ((0, 0), (CHUNK, Q_ROWS)):
                s = s_ref[slot, r0:r0 + SEEN, l0:l0 + Q_ROWS]
                m = jnp.max(s, axis=0, keepdims=True)
                p_ref[slot, r0:r0 + SEEN, l0:l0 + Q_ROWS] = jnp.exp(s - m).astype(BF16)

        def values(u):
            sub, hp = units[u]
            blk = p * SUBS + sub
            vt2 = jnp.concatenate(
                [vtpad[blk + t, hp] for t in range(WIN_BLOCKS)], axis=1)
            o_t = jnp.dot(vt2, p_ref[u % 2], preferred_element_type=F32)
            o_t = o_t[0:PAIR] / o_t[PAIR:PAIR + 1]
            t0, t1 = o_t[:, :Q_ROWS], o_t[:, Q_ROWS:]
            head_a = jnp.where(first_chunk_lanes, t0, pltpu.roll(t1, CHUNK, 1))
            head_b = jnp.where(first_chunk_lanes, pltpu.roll(t0, CHUNK, 1), t1)
            own = jnp.where(first_head_rows, head_a, head_b)
            o_ref[sub * Q_ROWS:(sub + 1) * Q_ROWS, hp * PAIR:(hp + 1) * PAIR] = (
                own.T.astype(BF16))

        casts = [(k, r0) for k, src in enumerate(cast_in)
                 for r0 in range(0, src.shape[0], CAST_ROWS)]
        casts_per_unit = pl.cdiv(len(casts), len(units))

        def cast_slab(k, r0):
            src, dst = cast_in[k], cast_out[k]
            rows = slice(r0, r0 + CAST_ROWS)
            if k != paired_cast:
                dst[rows, :] = src[rows, :].astype(BF16)
                return
            half = src.shape[1] // 2
            for j in range(half // FFN_COLS):
                for part in range(2):
                    c_src = part * half + j * FFN_COLS
                    c_dst = (2 * j + part) * FFN_COLS
                    dst[rows, c_dst:c_dst + FFN_COLS] = (
                        src[rows, c_src:c_src + FFN_COLS].astype(BF16))

        scores(0)
        for u in range(len(units) + 1):
            if u + 1 < len(units):
                scores(u + 1)
            if u >= 1:
                values(u - 1)
            if u < len(units):
                weights(u)
                for k, r0 in casts[u * casts_per_unit:(u + 1) * casts_per_unit]:
                    cast_slab(k, r0)

    @pl.when(p < PAD_BLOCKS // SUBS)
    def _():
        step(True)

    @pl.when(p >= PAD_BLOCKS // SUBS)
    def _():
        step(False)


def _attention(qkv, bias_base, cast_weights, *, paired_cast, batch, seq):
    n = batch * seq
    rows = SUBS * Q_ROWS
    steps = seq // rows
    n_steps = batch * steps
    cast_blocks = []
    for w in cast_weights:
        assert w.ndim == 2 and w.shape[0] % (n_steps * CAST_ROWS) == 0, w.shape
        cast_blocks.append(pl.BlockSpec((w.shape[0] // n_steps, w.shape[1]),
                                        lambda b, p: (b * steps + p, 0)))
    outs = pl.pallas_call(
        functools.partial(_attn_body, n_cast=len(cast_weights), paired_cast=paired_cast),
        grid=(batch, steps),
        in_specs=[
            pl.BlockSpec((N_HEADS, BIAS_EXT), lambda b, p: (0, 0)),
            pl.BlockSpec((rows, ATTN_WIDTH), lambda b, p: (b * steps + p, 0)),
            pl.BlockSpec((seq, ATTN_WIDTH), lambda b, p: (b, 1)),
            pl.BlockSpec((seq, ATTN_WIDTH), lambda b, p: (b, 2)),
        ] + cast_blocks,
        out_specs=[pl.BlockSpec((rows, ATTN_WIDTH), lambda b, p: (b * steps + p, 0))]
        + cast_blocks,
        out_shape=[jax.ShapeDtypeStruct((n, ATTN_WIDTH), BF16)]
        + [jax.ShapeDtypeStruct(w.shape, BF16) for w in cast_weights],
        scratch_shapes=[
            pltpu.VMEM((PAD_ROWS + seq, ATTN_WIDTH), BF16),
            pltpu.VMEM((PAD_BLOCKS + seq // Q_ROWS, N_PAIRS, PAIR + ONES_ROWS, Q_ROWS), BF16),
            pltpu.VMEM((N_PAIRS, BIAS_ROWS, 2 * Q_ROWS), F32),
            pltpu.VMEM((2, BAND, 2 * Q_ROWS), F32),
            pltpu.VMEM((2, BAND, 2 * Q_ROWS), BF16),
        ],
        compiler_params=_compiler_params(("arbitrary", "arbitrary")),
        name="attn",
    )(bias_base, qkv, qkv, qkv, *cast_weights)
    return outs[0], outs[1:]


def _bias_base(rel_bias):
    far = rel_bias[:, -1:]
    head = jnp.broadcast_to(far, (N_HEADS, Q_ROWS))
    tail = jnp.broadcast_to(far, (N_HEADS, BIAS_EXT - Q_ROWS - 2 * MAX_REL - 1))
    return jnp.concatenate([head, rel_bias, tail], axis=1)


def _mix_body(ya_ref, u_ref, gates_ref, x_ref, pw_ref, ps_ref,
              pa_ref, pb_ref, wo_ref, o_ref, ext_ref, yb_ref, a_ref, *, tiles_per_seq):
    i = pl.program_id(0)
    tm, d = x_ref.shape
    tile_in_seq = i % tiles_per_seq
    n_groups = len(POOL_WINDOWS)
    a_cols = d // n_groups

    @pl.when(tile_in_seq == 0)
    def _():
        ext_ref[0:POOL_HALO, :] = jnp.zeros((POOL_HALO, POOL_WIDTH), F32)

    @pl.when(tile_in_seq != 0)
    def _():
        ext_ref[0:POOL_HALO, :] = ext_ref[tm:, :]

    ext_ref[POOL_HALO:, :] = u_ref[...]

    pos = tile_in_seq * tm + lax.broadcasted_iota(jnp.int32, (tm, 1), 0)
    for g, w in enumerate(POOL_WINDOWS):
        acols = slice(g * a_cols, (g + 1) * a_cols)
        a_ref[:, acols] = jnp.dot(ya_ref[...], pa_ref[:, acols],
                                  preferred_element_type=F32)
        cols = slice(g * POOL_GROUP, (g + 1) * POOL_GROUP)
        acc = ext_ref[:, cols]
        shift = 1
        while shift < w:
            acc = acc + pltpu.roll(acc, shift, 0)
            shift *= 2
        acc = acc[POOL_HALO:]
        cur = ext_ref[POOL_HALO:, cols]
        cnt = jnp.minimum(pos + 1, w).astype(F32)
        dev = (acc / cnt - cur).astype(BF16)
        y = jnp.dot(dev, pw_ref[g], preferred_element_type=F32)
        yb_ref[:, cols] = (y * ps_ref[:, cols]).astype(BF16)

    bb = jnp.dot(yb_ref[...], pb_ref[...], preferred_element_type=F32)
    gate_a = _sigmoid(gates_ref[:, :d].astype(F32))
    gate_b = _sigmoid(gates_ref[:, d:].astype(F32))
    merged = gate_a * a_ref[...] + gate_b * bb
    y = jnp.dot(merged.astype(BF16), wo_ref[...], preferred_element_type=F32)
    o_ref[...] = x_ref[...] + y


def _mix(ya, u, gates, x2d, pool_w, pool_scale, w_a, w_b, w_o, *, seq, tm):
    n, d = x2d.shape
    tiles_per_seq = seq // tm
    const = lambda *shape: pl.BlockSpec(shape, lambda i: (0,) * len(shape),
                                        pipeline_mode=pl.Buffered(1))
    return pl.pallas_call(
        functools.partial(_mix_body, tiles_per_seq=tiles_per_seq),
        grid=(n // tm,),
        in_specs=[
            pl.BlockSpec((tm, ATTN_WIDTH), lambda i: (i, 0)),
            pl.BlockSpec((tm, POOL_WIDTH), lambda i: (i, 0)),
            pl.BlockSpec((tm, 2 * d), lambda i: (i, 0)),
            pl.BlockSpec((tm, d), lambda i: (i, 0)),
            const(len(POOL_WINDOWS), POOL_GROUP, POOL_GROUP),
            const(1, POOL_WIDTH),
            const(ATTN_WIDTH, d),
            const(POOL_WIDTH, d),
            const(d, d),
        ],
        out_specs=pl.BlockSpec((tm, d), lambda i: (i, 0)),
        out_shape=jax.ShapeDtypeStruct((n, d), F32),
        scratch_shapes=[
            pltpu.VMEM((POOL_HALO + tm, POOL_WIDTH), F32),
            pltpu.VMEM((tm, POOL_WIDTH), BF16),
            pltpu.VMEM((tm, d), F32),
        ],
        compiler_params=_compiler_params(("arbitrary",)),
        name="mix",
    )(ya, u, gates, x2d, pool_w, pool_scale, w_a, w_b, w_o)


def _ffn_body(x_ref, g_ref, wgu_ref, wd_ref, gf_ref, o_ref, h_ref):
    j = pl.program_id(1)

    def slab(first):
        if first:
            x = x_ref[...]
            h = _rms(x, g_ref[...]).astype(BF16)
            h_ref[...] = h
        else:
            h = h_ref[...]
        tf = wd_ref.shape[0]
        gate = jnp.dot(h, wgu_ref[:, :tf], preferred_element_type=F32)
        up = jnp.dot(h, wgu_ref[:, tf:], preferred_element_type=F32)
        act = (gate * _sigmoid(gate) * up).astype(BF16)
        down = jnp.dot(act, wd_ref[...], preferred_element_type=F32)
        o_ref[...] = (x if first else o_ref[...]) + down

    @pl.when(j == 0)
    def _():
        slab(True)

    @pl.when(j > 0)
    def _():
        slab(False)

    @pl.when(j == pl.num_programs(1) - 1)
    def _():
        o_ref[...] = _rms(o_ref[...], gf_ref[...])


def _ffn(x2d, g, w_gate_up, w_down, g_final, *, tm, tf):
    n, d = x2d.shape
    d_ff = w_down.shape[0]
    n_f = d_ff // tf
    return pl.pallas_call(
        _ffn_body,
        grid=(n // tm, n_f),
        in_specs=[
            pl.BlockSpec((tm, d), lambda i, j: (i, 0)),
            pl.BlockSpec((1, d), lambda i, j: (0, 0)),
            pl.BlockSpec((d, 2 * tf), lambda i, j: (0, j)),
            pl.BlockSpec((tf, d), lambda i, j: (j, 0)),
            pl.BlockSpec((1, d), lambda i, j: (0, 0)),
        ],
        out_specs=pl.BlockSpec((tm, d), lambda i, j: (i, 0)),
        out_shape=jax.ShapeDtypeStruct((n, d), F32),
        scratch_shapes=[pltpu.VMEM((tm, d), BF16)],
        compiler_params=_compiler_params(("arbitrary", "arbitrary")),
        name="ffn",
    )(x2d, g, w_gate_up, w_down, g_final)


def _layer(x2d, norm_mix, w_in, rel_bias, pool_w, pool_scale, w_a, w_b, w_o,
           norm_ffn, w_gate_up, w_down, norm_out, *, batch, seq):
    d = x2d.shape[1]
    qkv, u, gates = _in_proj(x2d, norm_mix.reshape(1, d), w_in, tm=IN_PROJ_ROWS)
    pool_w2d = pool_w.reshape(-1, pool_w.shape[-1])
    ya, (pool_w2d, w_a, w_b, w_o, w_gate_up, w_down) = _attention(
        qkv, _bias_base(rel_bias), (pool_w2d, w_a, w_b, w_o, w_gate_up, w_down),
        paired_cast=4, batch=batch, seq=seq)
    x1 = _mix(ya, u, gates, x2d, pool_w2d.reshape(pool_w.shape), pool_scale.reshape(1, -1),
              w_a, w_b, w_o, seq=seq, tm=MIX_ROWS)
    return _ffn(x1, norm_ffn.reshape(1, d), w_gate_up, w_down,
                norm_out.reshape(1, d), tm=FFN_ROWS, tf=FFN_COLS)


def kernel(x, norm_mix, w_in, rel_bias, pool_w, pool_scale, w_branch_a, w_branch_b,
           w_out, norm_ffn, w_gate_up, w_down, norm_final):
    batch, seq, d = x.shape
    depth = w_in.shape[0]
    assert depth == 1, "the fused ffn kernel applies the final norm to the only layer"
    out = _layer(x.reshape(batch * seq, d), norm_mix[0], w_in[0], rel_bias[0],
                 pool_w[0], pool_scale[0], w_branch_a[0], w_branch_b[0], w_out[0],
                 norm_ffn[0], w_gate_up[0], w_down[0], norm_final,
                 batch=batch, seq=seq)
    return out.reshape(batch, seq, d)
```

```python
import functools
import math

import jax
import jax.numpy as jnp
from jax import lax
from jax.experimental import pallas as pl
from jax.experimental.pallas import tpu as pltpu

F32 = jnp.float32
BF16 = jnp.bfloat16

CHUNK = 64
LEFT_CHUNKS = 8
N_HEADS = 16
HEAD_DIM = 64
ATTN_WIDTH = N_HEADS * HEAD_DIM
MAX_REL = 128
POOL_WINDOWS = (2, 4, 8, 16)
POOL_GROUP = 256
POOL_WIDTH = POOL_GROUP * len(POOL_WINDOWS)
EPS = 1e-6
NEG_INF = -1e30

Q_ROWS = 2 * CHUNK
BAND = (LEFT_CHUNKS + 2) * CHUNK
PAD_ROWS = LEFT_CHUNKS * CHUNK
BIAS_EXT = BAND + Q_ROWS
POOL_HALO = 16

V7X_VMEM_BYTES = 64 * 1024 * 1024
VMEM_LIMIT_BYTES = V7X_VMEM_BYTES - 4 * 1024 * 1024
IN_PROJ_ROWS = 256
STAGE_ROWS = 128
MIX_ROWS = 512
FFN_ROWS = 1024
FFN_COLS = 512


def _rms(x, g):
    ms = jnp.mean(x * x, axis=-1, keepdims=True)
    return x * lax.rsqrt(ms + EPS) * g


def _sigmoid(x):
    return 0.5 * jnp.tanh(0.5 * x) + 0.5


def _compiler_params(semantics):
    return pltpu.CompilerParams(
        dimension_semantics=semantics, vmem_limit_bytes=VMEM_LIMIT_BYTES)


def _in_proj_body(x_ref, g_ref, w_hbm, qkv_ref, u_ref, gate_ref, w_ref, stage_ref, sem):
    n_chunks = w_ref.shape[0] // STAGE_ROWS

    def chunk_copy(c, slot):
        rows = pl.ds(pl.multiple_of(c * STAGE_ROWS, STAGE_ROWS), STAGE_ROWS)
        return pltpu.make_async_copy(w_hbm.at[rows], stage_ref.at[slot], sem.at[slot])

    @pl.when(pl.program_id(0) == 0)
    def _():
        chunk_copy(0, 0).start()

        def two_chunks(cc, carry):
            for slot in range(2):
                c = 2 * cc + slot

                @pl.when(c + 1 < n_chunks)
                def _():
                    chunk_copy(c + 1, 1 - slot).start()

                chunk_copy(c, slot).wait()
                rows = pl.ds(pl.multiple_of(c * STAGE_ROWS, STAGE_ROWS), STAGE_ROWS)
                w_ref[rows, :] = stage_ref[slot].astype(BF16)
            return carry

        lax.fori_loop(0, n_chunks // 2, two_chunks, 0)

    h = _rms(x_ref[...], g_ref[...]).astype(BF16)
    q_end = qkv_ref.shape[1]
    u_end = q_end + u_ref.shape[1]
    qkv_ref[...] = jnp.dot(h, w_ref[:, :q_end], preferred_element_type=F32).astype(BF16)
    u_ref[...] = jnp.dot(h, w_ref[:, q_end:u_end], preferred_element_type=F32)
    gate_ref[...] = jnp.dot(h, w_ref[:, u_end:], preferred_element_type=F32).astype(BF16)


def _in_proj(x2d, g, w, *, tm):
    n, d = x2d.shape
    in_width = w.shape[1]
    n_gate = in_width - 3 * ATTN_WIDTH - POOL_WIDTH
    assert d % (2 * STAGE_ROWS) == 0
    return pl.pallas_call(
        _in_proj_body,
        grid=(n // tm,),
        in_specs=[
            pl.BlockSpec((tm, d), lambda i: (i, 0)),
            pl.BlockSpec((1, d), lambda i: (0, 0)),
            pl.BlockSpec(memory_space=pl.ANY),
        ],
        out_specs=[
            pl.BlockSpec((tm, 3 * ATTN_WIDTH), lambda i: (i, 0)),
            pl.BlockSpec((tm, POOL_WIDTH), lambda i: (i, 0)),
            pl.BlockSpec((tm, n_gate), lambda i: (i, 0)),
        ],
        out_shape=[
            jax.ShapeDtypeStruct((n, 3 * ATTN_WIDTH), BF16),
            jax.ShapeDtypeStruct((n, POOL_WIDTH), F32),
            jax.ShapeDtypeStruct((n, n_gate), BF16),
        ],
        scratch_shapes=[
            pltpu.VMEM((d, in_width), BF16),
            pltpu.VMEM((2, STAGE_ROWS, in_width), F32),
            pltpu.SemaphoreType.DMA((2,)),
        ],
        compiler_params=_compiler_params(("arbitrary",)),
        name="in_proj",
    )(x2d, g, w)


PAIR = 2 * HEAD_DIM
N_PAIRS = N_HEADS // 2
WIN_BLOCKS = BAND // Q_ROWS
PAD_BLOCKS = PAD_ROWS // Q_ROWS
FLAT_END = PAD_ROWS - MAX_REL
BIAS_ROWS = BAND - FLAT_END
SEEN = BAND - CHUNK
ONES_ROWS = 16
SUBS = 4
CAST_ROWS = 16


def _attn_body(base_ref, q_ref, k_ref, v_ref, *rest, n_cast, paired_cast):
    cast_in = rest[:n_cast]
    o_ref = rest[n_cast]
    cast_out = rest[n_cast + 1:2 * n_cast + 1]
    kpad, vtpad, bias_ref, s_ref, p_ref = rest[2 * n_cast + 1:]
    b = pl.program_id(0)
    p = pl.program_id(1)

    @pl.when((b == 0) & (p == 0))
    def _():
        for hp in range(N_PAIRS):
            heads = []
            for a in range(2):
                h = 2 * hp + a
                base = jnp.broadcast_to(base_ref[h:h + 1, :], (BAND, BIAS_EXT))
                rot = pltpu.roll(base, 0, 1, stride=1, stride_axis=0)
                heads.append(rot[FLAT_END:, :Q_ROWS] - base[FLAT_END:, 0:1])
            bias_ref[hp] = jnp.concatenate(
                [heads[a][:, c * CHUNK:(c + 1) * CHUNK] for c in range(2) for a in range(2)],
                axis=1)
        kpad[0:PAD_ROWS, :] = jnp.zeros((PAD_ROWS, ATTN_WIDTH), BF16)
        vtpad[0:PAD_BLOCKS, :, 0:PAIR, :] = jnp.zeros(
            (PAD_BLOCKS, N_PAIRS, PAIR, Q_ROWS), BF16)
        vtpad[:, :, PAIR:, :] = jnp.ones(
            (vtpad.shape[0], N_PAIRS, ONES_ROWS, Q_ROWS), BF16)
        p_ref[:, SEEN:, :Q_ROWS] = jnp.zeros((2, CHUNK, Q_ROWS), BF16)
        p_ref[:, :CHUNK, Q_ROWS:] = jnp.zeros((2, CHUNK, Q_ROWS), BF16)

    @pl.when(p == 0)
    def _():
        kpad[PAD_ROWS:, :] = k_ref[...]
        for c in range(v_ref.shape[0] // Q_ROWS):
            blk_t = v_ref[c * Q_ROWS:(c + 1) * Q_ROWS, :].astype(F32).T.astype(BF16)
            for hp in range(N_PAIRS):
                vtpad[PAD_BLOCKS + c, hp, 0:PAIR, :] = blk_t[hp * PAIR:(hp + 1) * PAIR, :]

    lane = lax.broadcasted_iota(jnp.int32, (CHUNK, PAIR), 1)
    first_head = lane < HEAD_DIM
    row = lax.broadcasted_iota(jnp.int32, (PAIR, Q_ROWS), 0)
    first_head_rows = row < HEAD_DIM
    first_chunk_lanes = lax.broadcasted_iota(jnp.int32, (PAIR, Q_ROWS), 1) < CHUNK
    scale = jnp.asarray(1.0 / math.sqrt(HEAD_DIM), BF16)

    def step(mask_missing_keys):
        units = [(sub, hp) for sub in range(SUBS) for hp in range(N_PAIRS)]

        def scores(u):
            sub, hp = units[u]
            blk = p * SUBS + sub
            lanes = slice(hp * PAIR, (hp + 1) * PAIR)
            q2 = q_ref[sub * Q_ROWS:(sub + 1) * Q_ROWS, lanes] * scale
            zero = jnp.zeros((CHUNK, PAIR), BF16)
            qm = jnp.concatenate(
                [jnp.where(first_head if a == 0 else jnp.logical_not(first_head),
                           q2[c * CHUNK:(c + 1) * CHUNK], zero)
                 for c in range(2) for a in range(2)], axis=0)
            qm_t = qm.astype(F32).T.astype(BF16)
            k2 = kpad[pl.ds(pl.multiple_of(blk * Q_ROWS, Q_ROWS), BAND), lanes]
            s = jnp.dot(k2, qm_t, preferred_element_type=F32)
            pieces = [(0, FLAT_END, 0, None), (FLAT_END, SEEN, 0, (0, SEEN - FLAT_END)),
                      (CHUNK, FLAT_END, Q_ROWS, None), (FLAT_END, BAND, Q_ROWS, (0, BIAS_ROWS))]
            for r0, r1, l0, bias_rows in pieces:
                part = s[r0:r1, l0:l0 + Q_ROWS]
                if bias_rows is not None:
                    part = part + bias_ref[hp, bias_rows[0]:bias_rows[1], l0:l0 + Q_ROWS]
                if mask_missing_keys:
                    key = r0 + lax.broadcasted_iota(jnp.int32, part.shape, 0)
                    part = jnp.where(key >= PAD_ROWS - blk * Q_ROWS, part, NEG_INF)
                s_ref[u % 2, r0:r1, l0:l0 + Q_ROWS] = part

        def weights(u):
            slot = u % 2
            for r0, l0 in ((0, 0), (CHUNK, Q_ROWS)):
                s = s_ref[slot, r0:r0 + SEEN, l0:l0 + Q_ROWS]
                m = jnp.max(s, axis=0, keepdims=True)
                p_ref[slot, r0:r0 + SEEN, l0:l0 + Q_ROWS] = jnp.exp(s - m).astype(BF16)

        def values(u):
            sub, hp = units[u]
            blk = p * SUBS + sub
            vt2 = jnp.concatenate(
                [vtpad[blk + t, hp] for t in range(WIN_BLOCKS)], axis=1)
            o_t = jnp.dot(vt2, p_ref[u % 2], preferred_element_type=F32)
            o_t = o_t[0:PAIR] / o_t[PAIR:PAIR + 1]
            t0, t1 = o_t[:, :Q_ROWS], o_t[:, Q_ROWS:]
            head_a = jnp.where(first_chunk_lanes, t0, pltpu.roll(t1, CHUNK, 1))
            head_b = jnp.where(first_chunk_lanes, pltpu.roll(t0, CHUNK, 1), t1)
            own = jnp.where(first_head_rows, head_a, head_b)
            o_ref[sub * Q_ROWS:(sub + 1) * Q_ROWS, hp * PAIR:(hp + 1) * PAIR] = (
                own.T.astype(BF16))

        casts = [(k, r0) for k, src in enumerate(cast_in)
                 for r0 in range(0, src.shape[0], CAST_ROWS)]
        casts_per_unit = pl.cdiv(len(casts), len(units))

        def cast_slab(k, r0):
            src, dst = cast_in[k], cast_out[k]
            rows = slice(r0, r0 + CAST_ROWS)
            if k != paired_cast:
                dst[rows, :] = src[rows, :].astype(BF16)
                return
            half = src.shape[1] // 2
            for j in range(half // FFN_COLS):
                for part in range(2):
                    c_src = part * half + j * FFN_COLS
                    c_dst = (2 * j + part) * FFN_COLS
                    dst[rows, c_dst:c_dst + FFN_COLS] = (
                        src[rows, c_src:c_src + FFN_COLS].astype(BF16))

        scores(0)
        for u in range(len(units) + 1):
            if u + 1 < len(units):
                scores(u + 1)
            if u >= 1:
                values(u - 1)
            if u < len(units):
                weights(u)
                for k, r0 in casts[u * casts_per_unit:(u + 1) * casts_per_unit]:
                    cast_slab(k, r0)

    @pl.when(p < PAD_BLOCKS // SUBS)
    def _():
        step(True)

    @pl.when(p >= PAD_BLOCKS // SUBS)
    def _():
        step(False)


def _attention(qkv, bias_base, cast_weights, *, paired_cast, batch, seq):
    n = batch * seq
    rows = SUBS * Q_ROWS
    steps = seq // rows
    n_steps = batch * steps
    cast_blocks = []
    for w in cast_weights:
        assert w.ndim == 2 and w.shape[0] % (n_steps * CAST_ROWS) == 0, w.shape
        cast_blocks.append(pl.BlockSpec((w.shape[0] // n_steps, w.shape[1]),
                                        lambda b, p: (b * steps + p, 0)))
    outs = pl.pallas_call(
        functools.partial(_attn_body, n_cast=len(cast_weights), paired_cast=paired_cast),
        grid=(batch, steps),
        in_specs=[
            pl.BlockSpec((N_HEADS, BIAS_EXT), lambda b, p: (0, 0)),
            pl.BlockSpec((rows, ATTN_WIDTH), lambda b, p: (b * steps + p, 0)),
            pl.BlockSpec((seq, ATTN_WIDTH), lambda b, p: (b, 1)),
            pl.BlockSpec((seq, ATTN_WIDTH), lambda b, p: (b, 2)),
        ] + cast_blocks,
        out_specs=[pl.BlockSpec((rows, ATTN_WIDTH), lambda b, p: (b * steps + p, 0))]
        + cast_blocks,
        out_shape=[jax.ShapeDtypeStruct((n, ATTN_WIDTH), BF16)]
        + [jax.ShapeDtypeStruct(w.shape, BF16) for w in cast_weights],
        scratch_shapes=[
            pltpu.VMEM((PAD_ROWS + seq, ATTN_WIDTH), BF16),
            pltpu.VMEM((PAD_BLOCKS + seq // Q_ROWS, N_PAIRS, PAIR + ONES_ROWS, Q_ROWS), BF16),
            pltpu.VMEM((N_PAIRS, BIAS_ROWS, 2 * Q_ROWS), F32),
            pltpu.VMEM((2, BAND, 2 * Q_ROWS), F32),
            pltpu.VMEM((2, BAND, 2 * Q_ROWS), BF16),
        ],
        compiler_params=_compiler_params(("arbitrary", "arbitrary")),
        name="attn",
    )(bias_base, qkv, qkv, qkv, *cast_weights)
    return outs[0], outs[1:]


def _bias_base(rel_bias):
    far = rel_bias[:, -1:]
    head = jnp.broadcast_to(far, (N_HEADS, Q_ROWS))
    tail = jnp.broadcast_to(far, (N_HEADS, BIAS_EXT - Q_ROWS - 2 * MAX_REL - 1))
    return jnp.concatenate([head, rel_bias, tail], axis=1)


def _mix_body(ya_ref, u_ref, gates_ref, x_ref, pw_ref, ps_ref,
              pa_ref, pb_ref, wo_ref, o_ref, ext_ref, yb_ref, a_ref, *, tiles_per_seq):
    i = pl.program_id(0)
    tm, d = x_ref.shape
    tile_in_seq = i % tiles_per_seq
    n_groups = len(POOL_WINDOWS)
    a_cols = d // n_groups

    @pl.when(tile_in_seq == 0)
    def _():
        ext_ref[0:POOL_HALO, :] = jnp.zeros((POOL_HALO, POOL_WIDTH), F32)

    @pl.when(tile_in_seq != 0)
    def _():
        ext_ref[0:POOL_HALO, :] = ext_ref[tm:, :]

    ext_ref[POOL_HALO:, :] = u_ref[...]

    pos = tile_in_seq * tm + lax.broadcasted_iota(jnp.int32, (tm, 1), 0)
    for g, w in enumerate(POOL_WINDOWS):
        acols = slice(g * a_cols, (g + 1) * a_cols)
        a_ref[:, acols] = jnp.dot(ya_ref[...], pa_ref[:, acols],
                                  preferred_element_type=F32)
        cols = slice(g * POOL_GROUP, (g + 1) * POOL_GROUP)
        acc = ext_ref[:, cols]
        shift = 1
        while shift < w:
            acc = acc + pltpu.roll(acc, shift, 0)
            shift *= 2
        acc = acc[POOL_HALO:]
        cur = ext_ref[POOL_HALO:, cols]
        cnt = jnp.minimum(pos + 1, w).astype(F32)
        dev = (acc / cnt - cur).astype(BF16)
        y = jnp.dot(dev, pw_ref[g], preferred_element_type=F32)
        yb_ref[:, cols] = (y * ps_ref[:, cols]).astype(BF16)

    bb = jnp.dot(yb_ref[...], pb_ref[...], preferred_element_type=F32)
    gate_a = _sigmoid(gates_ref[:, :d].astype(F32))
    gate_b = _sigmoid(gates_ref[:, d:].astype(F32))
    merged = gate_a * a_ref[...] + gate_b * bb
    y = jnp.dot(merged.astype(BF16), wo_ref[...], preferred_element_type=F32)
    o_ref[...] = x_ref[...] + y


def _mix(ya, u, gates, x2d, pool_w, pool_scale, w_a, w_b, w_o, *, seq, tm):
    n, d = x2d.shape
    tiles_per_seq = seq // tm
    const = lambda *shape: pl.BlockSpec(shape, lambda i: (0,) * len(shape),
                                        pipeline_mode=pl.Buffered(1))
    return pl.pallas_call(
        functools.partial(_mix_body, tiles_per_seq=tiles_per_seq),
        grid=(n // tm,),
        in_specs=[
            pl.BlockSpec((tm, ATTN_WIDTH), lambda i: (i, 0)),
            pl.BlockSpec((tm, POOL_WIDTH), lambda i: (i, 0)),
            pl.BlockSpec((tm, 2 * d), lambda i: (i, 0)),
            pl.BlockSpec((tm, d), lambda i: (i, 0)),
            const(len(POOL_WINDOWS), POOL_GROUP, POOL_GROUP),
            const(1, POOL_WIDTH),
            const(ATTN_WIDTH, d),
            const(POOL_WIDTH, d),
            const(d, d),
        ],
        out_specs=pl.BlockSpec((tm, d), lambda i: (i, 0)),
        out_shape=jax.ShapeDtypeStruct((n, d), F32),
        scratch_shapes=[
            pltpu.VMEM((POOL_HALO + tm, POOL_WIDTH), F32),
            pltpu.VMEM((tm, POOL_WIDTH), BF16),
            pltpu.VMEM((tm, d), F32),
        ],
        compiler_params=_compiler_params(("arbitrary",)),
        name="mix",
    )(ya, u, gates, x2d, pool_w, pool_scale, w_a, w_b, w_o)


def _ffn_body(x_ref, g_ref, wgu_ref, wd_ref, gf_ref, o_ref, h_ref):
    j = pl.program_id(1)

    def slab(first, last=False):
        if first:
            x = x_ref[...]
            h = _rms(x, g_ref[...]).astype(BF16)
            h_ref[...] = h
        else:
            h = h_ref[...]
        tf = wd_ref.shape[0]
        gate = jnp.dot(h, wgu_ref[:, :tf], preferred_element_type=F32)
        up = jnp.dot(h, wgu_ref[:, tf:], preferred_element_type=F32)
        act = (gate * _sigmoid(gate) * up).astype(BF16)
        down = jnp.dot(act, wd_ref[...], preferred_element_type=F32)
        total = (x if first else o_ref[...]) + down
        o_ref[...] = _rms(total, gf_ref[...]) if last else total

    last_j = pl.num_programs(1) - 1

    @pl.when(j == 0)
    def _():
        slab(True)

    @pl.when((j > 0) & (j < last_j))
    def _():
        slab(False)

    @pl.when(j == last_j)
    def _():
        slab(False, last=True)


def _ffn(x2d, g, w_gate_up, w_down, g_final, *, tm, tf):
    n, d = x2d.shape
    d_ff = w_down.shape[0]
    n_f = d_ff // tf
    return pl.pallas_call(
        _ffn_body,
        grid=(n // tm, n_f),
        in_specs=[
            pl.BlockSpec((tm, d), lambda i, j: (i, 0)),
            pl.BlockSpec((1, d), lambda i, j: (0, 0)),
            pl.BlockSpec((d, 2 * tf), lambda i, j: (0, j)),
            pl.BlockSpec((tf, d), lambda i, j: (j, 0)),
            pl.BlockSpec((1, d), lambda i, j: (0, 0)),
        ],
        out_specs=pl.BlockSpec((tm, d), lambda i, j: (i, 0)),
        out_shape=jax.ShapeDtypeStruct((n, d), F32),
        scratch_shapes=[pltpu.VMEM((tm, d), BF16)],
        compiler_params=_compiler_params(("arbitrary", "arbitrary")),
        name="ffn",
    )(x2d, g, w_gate_up, w_down, g_final)


def _layer(x2d, norm_mix, w_in, rel_bias, pool_w, pool_scale, w_a, w_b, w_o,
           norm_ffn, w_gate_up, w_down, norm_out, *, batch, seq):
    d = x2d.shape[1]
    qkv, u, gates = _in_proj(x2d, norm_mix.reshape(1, d), w_in, tm=IN_PROJ_ROWS)
    pool_w2d = pool_w.reshape(-1, pool_w.shape[-1])
    ya, (pool_w2d, w_a, w_b, w_o, w_gate_up, w_down) = _attention(
        qkv, _bias_base(rel_bias), (pool_w2d, w_a, w_b, w_o, w_gate_up, w_down),
        paired_cast=4, batch=batch, seq=seq)
    x1 = _mix(ya, u, gates, x2d, pool_w2d.reshape(pool_w.shape), pool_scale.reshape(1, -1),
              w_a, w_b, w_o, seq=seq, tm=MIX_ROWS)
    return _ffn(x1, norm_ffn.reshape(1, d), w_gate_up, w_down,
                norm_out.reshape(1, d), tm=FFN_ROWS, tf=FFN_COLS)


def kernel(x, norm_mix, w_in, rel_bias, pool_w, pool_scale, w_branch_a, w_branch_b,
           w_out, norm_ffn, w_gate_up, w_down, norm_final):
    batch, seq, d = x.shape
    depth = w_in.shape[0]
    assert depth == 1, "the fused ffn kernel applies the final norm to the only layer"
    out = _layer(x.reshape(batch * seq, d), norm_mix[0], w_in[0], rel_bias[0],
                 pool_w[0], pool_scale[0], w_branch_a[0], w_branch_b[0], w_out[0],
                 norm_ffn[0], w_gate_up[0], w_down[0], norm_final,
                 batch=batch, seq=seq)
    return out.reshape(batch, seq, d)
```

```python
import functools
import math

import jax
import jax.numpy as jnp
from jax import lax
from jax.experimental import pallas as pl
from jax.experimental.pallas import tpu as pltpu

F32 = jnp.float32
BF16 = jnp.bfloat16

CHUNK = 64
LEFT_CHUNKS = 8
N_HEADS = 16
HEAD_DIM = 64
ATTN_WIDTH = N_HEADS * HEAD_DIM
MAX_REL = 128
POOL_WINDOWS = (2, 4, 8, 16)
POOL_GROUP = 256
POOL_WIDTH = POOL_GROUP * len(POOL_WINDOWS)
EPS = 1e-6
NEG_INF = -1e30

Q_ROWS = 2 * CHUNK
BAND = (LEFT_CHUNKS + 2) * CHUNK
PAD_ROWS = LEFT_CHUNKS * CHUNK
BIAS_EXT = BAND + Q_ROWS
POOL_HALO = 16

V7X_VMEM_BYTES = 64 * 1024 * 1024
VMEM_LIMIT_BYTES = V7X_VMEM_BYTES - 4 * 1024 * 1024
IN_PROJ_ROWS = 256
STAGE_ROWS = 128
MIX_ROWS = 256
FFN_ROWS = 1024
FFN_COLS = 512


def _rms(x, g):
    ms = jnp.mean(x * x, axis=-1, keepdims=True)
    return x * lax.rsqrt(ms + EPS) * g


def _sigmoid(x):
    return 0.5 * jnp.tanh(0.5 * x) + 0.5


def _compiler_params(semantics):
    return pltpu.CompilerParams(
        dimension_semantics=semantics, vmem_limit_bytes=VMEM_LIMIT_BYTES)


def _in_proj_body(x_ref, g_ref, w_hbm, qkv_ref, u_ref, gate_ref, w_ref, stage_ref, sem):
    n_chunks = w_ref.shape[0] // STAGE_ROWS

    def chunk_copy(c, slot):
        rows = pl.ds(pl.multiple_of(c * STAGE_ROWS, STAGE_ROWS), STAGE_ROWS)
        return pltpu.make_async_copy(w_hbm.at[rows], stage_ref.at[slot], sem.at[slot])

    @pl.when(pl.program_id(0) == 0)
    def _():
        chunk_copy(0, 0).start()

        def two_chunks(cc, carry):
            for slot in range(2):
                c = 2 * cc + slot

                @pl.when(c + 1 < n_chunks)
                def _():
                    chunk_copy(c + 1, 1 - slot).start()

                chunk_copy(c, slot).wait()
                rows = pl.ds(pl.multiple_of(c * STAGE_ROWS, STAGE_ROWS), STAGE_ROWS)
                w_ref[rows, :] = stage_ref[slot].astype(BF16)
            return carry

        lax.fori_loop(0, n_chunks // 2, two_chunks, 0)

    h = _rms(x_ref[...], g_ref[...]).astype(BF16)
    q_end = qkv_ref.shape[1]
    u_end = q_end + u_ref.shape[1]
    qkv_ref[...] = jnp.dot(h, w_ref[:, :q_end], preferred_element_type=F32).astype(BF16)
    u_ref[...] = jnp.dot(h, w_ref[:, q_end:u_end], preferred_element_type=F32)
    gate_ref[...] = jnp.dot(h, w_ref[:, u_end:], preferred_element_type=F32).astype(BF16)


def _in_proj(x2d, g, w, *, tm):
    n, d = x2d.shape
    in_width = w.shape[1]
    n_gate = in_width - 3 * ATTN_WIDTH - POOL_WIDTH
    assert d % (2 * STAGE_ROWS) == 0
    return pl.pallas_call(
        _in_proj_body,
        grid=(n // tm,),
        in_specs=[
            pl.BlockSpec((tm, d), lambda i: (i, 0)),
            pl.BlockSpec((1, d), lambda i: (0, 0)),
            pl.BlockSpec(memory_space=pl.ANY),
        ],
        out_specs=[
            pl.BlockSpec((tm, 3 * ATTN_WIDTH), lambda i: (i, 0)),
            pl.BlockSpec((tm, POOL_WIDTH), lambda i: (i, 0)),
            pl.BlockSpec((tm, n_gate), lambda i: (i, 0)),
        ],
        out_shape=[
            jax.ShapeDtypeStruct((n, 3 * ATTN_WIDTH), BF16),
            jax.ShapeDtypeStruct((n, POOL_WIDTH), F32),
            jax.ShapeDtypeStruct((n, n_gate), BF16),
        ],
        scratch_shapes=[
            pltpu.VMEM((d, in_width), BF16),
            pltpu.VMEM((2, STAGE_ROWS, in_width), F32),
            pltpu.SemaphoreType.DMA((2,)),
        ],
        compiler_params=_compiler_params(("arbitrary",)),
        name="in_proj",
    )(x2d, g, w)


PAIR = 2 * HEAD_DIM
N_PAIRS = N_HEADS // 2
WIN_BLOCKS = BAND // Q_ROWS
PAD_BLOCKS = PAD_ROWS // Q_ROWS
FLAT_END = PAD_ROWS - MAX_REL
BIAS_ROWS = BAND - FLAT_END
SEEN = BAND - CHUNK
ONES_ROWS = 16
SUBS = 4
CAST_ROWS = 16


def _attn_body(base_ref, q_ref, k_ref, v_ref, *rest, n_cast, paired_cast):
    cast_in = rest[:n_cast]
    o_ref = rest[n_cast]
    cast_out = rest[n_cast + 1:2 * n_cast + 1]
    kpad, vtpad, bias_ref, s_ref, p_ref = rest[2 * n_cast + 1:]
    b = pl.program_id(0)
    p = pl.program_id(1)

    @pl.when((b == 0) & (p == 0))
    def _():
        for hp in range(N_PAIRS):
            heads = []
            for a in range(2):
                h = 2 * hp + a
                base = jnp.broadcast_to(base_ref[h:h + 1, :], (BAND, BIAS_EXT))
                rot = pltpu.roll(base, 0, 1, stride=1, stride_axis=0)
                heads.append(rot[FLAT_END:, :Q_ROWS] - base[FLAT_END:, 0:1])
            bias_ref[hp] = jnp.concatenate(
                [heads[a][:, c * CHUNK:(c + 1) * CHUNK] for c in range(2) for a in range(2)],
                axis=1)
        kpad[0:PAD_ROWS, :] = jnp.zeros((PAD_ROWS, ATTN_WIDTH), BF16)
        vtpad[0:PAD_BLOCKS, :, 0:PAIR, :] = jnp.zeros(
            (PAD_BLOCKS, N_PAIRS, PAIR, Q_ROWS), BF16)
        vtpad[:, :, PAIR:, :] = jnp.ones(
            (vtpad.shape[0], N_PAIRS, ONES_ROWS, Q_ROWS), BF16)
        p_ref[:, SEEN:, :Q_ROWS] = jnp.zeros((2, CHUNK, Q_ROWS), BF16)
        p_ref[:, :CHUNK, Q_ROWS:] = jnp.zeros((2, CHUNK, Q_ROWS), BF16)

    @pl.when(p == 0)
    def _():
        kpad[PAD_ROWS:, :] = k_ref[...]
        for c in range(v_ref.shape[0] // Q_ROWS):
            blk_t = v_ref[c * Q_ROWS:(c + 1) * Q_ROWS, :].astype(F32).T.astype(BF16)
            for hp in range(N_PAIRS):
                vtpad[PAD_BLOCKS + c, hp, 0:PAIR, :] = blk_t[hp * PAIR:(hp + 1) * PAIR, :]

    lane = lax.broadcasted_iota(jnp.int32, (CHUNK, PAIR), 1)
    first_head = lane < HEAD_DIM
    row = lax.broadcasted_iota(jnp.int32, (PAIR, Q_ROWS), 0)
    first_head_rows = row < HEAD_DIM
    first_chunk_lanes = lax.broadcasted_iota(jnp.int32, (PAIR, Q_ROWS), 1) < CHUNK
    scale = jnp.asarray(1.0 / math.sqrt(HEAD_DIM), BF16)

    def step(mask_missing_keys):
        units = [(sub, hp) for sub in range(SUBS) for hp in range(N_PAIRS)]

        def scores(u):
            sub, hp = units[u]
            blk = p * SUBS + sub
            lanes = slice(hp * PAIR, (hp + 1) * PAIR)
            q2 = q_ref[sub * Q_ROWS:(sub + 1) * Q_ROWS, lanes] * scale
            zero = jnp.zeros((CHUNK, PAIR), BF16)
            qm = jnp.concatenate(
                [jnp.where(first_head if a == 0 else jnp.logical_not(first_head),
                           q2[c * CHUNK:(c + 1) * CHUNK], zero)
                 for c in range(2) for a in range(2)], axis=0)
            qm_t = qm.astype(F32).T.astype(BF16)
            k2 = kpad[pl.ds(pl.multiple_of(blk * Q_ROWS, Q_ROWS), BAND), lanes]
            s = jnp.dot(k2, qm_t, preferred_element_type=F32)
            pieces = [(0, FLAT_END, 0, None), (FLAT_END, SEEN, 0, (0, SEEN - FLAT_END)),
                      (CHUNK, FLAT_END, Q_ROWS, None), (FLAT_END, BAND, Q_ROWS, (0, BIAS_ROWS))]
            for r0, r1, l0, bias_rows in pieces:
                part = s[r0:r1, l0:l0 + Q_ROWS]
                if bias_rows is not None:
                    part = part + bias_ref[hp, bias_rows[0]:bias_rows[1], l0:l0 + Q_ROWS]
                if mask_missing_keys:
                    key = r0 + lax.broadcasted_iota(jnp.int32, part.shape, 0)
                    part = jnp.where(key >= PAD_ROWS - blk * Q_ROWS, part, NEG_INF)
                s_ref[u % 2, r0:r1, l0:l0 + Q_ROWS] = part

        def weights(u):
            slot = u % 2
            for r0, l0 in ((0, 0), (CHUNK, Q_ROWS)):
                s = s_ref[slot, r0:r0 + SEEN, l0:l0 + Q_ROWS]
                m = jnp.max(s, axis=0, keepdims=True)
                p_ref[slot, r0:r0 + SEEN, l0:l0 + Q_ROWS] = jnp.exp(s - m).astype(BF16)

        def values(u):
            sub, hp = units[u]
            blk = p * SUBS + sub
            vt2 = jnp.concatenate(
                [vtpad[blk + t, hp] for t in range(WIN_BLOCKS)], axis=1)
            o_t = jnp.dot(vt2, p_ref[u % 2], preferred_element_type=F32)
            o_t = o_t[0:PAIR] / o_t[PAIR:PAIR + 1]
            t0, t1 = o_t[:, :Q_ROWS], o_t[:, Q_ROWS:]
            head_a = jnp.where(first_chunk_lanes, t0, pltpu.roll(t1, CHUNK, 1))
            head_b = jnp.where(first_chunk_lanes, pltpu.roll(t0, CHUNK, 1), t1)
            own = jnp.where(first_head_rows, head_a, head_b)
            o_ref[sub * Q_ROWS:(sub + 1) * Q_ROWS, hp * PAIR:(hp + 1) * PAIR] = (
                own.T.astype(BF16))

        casts = [(k, r0) for k, src in enumerate(cast_in)
                 for r0 in range(0, src.shape[0], CAST_ROWS)]
        casts_per_unit = pl.cdiv(len(casts), len(units))

        def cast_slab(k, r0):
            src, dst = cast_in[k], cast_out[k]
            rows = slice(r0, r0 + CAST_ROWS)
            if k != paired_cast:
                dst[rows, :] = src[rows, :].astype(BF16)
                return
            half = src.shape[1] // 2
            for j in range(half // FFN_COLS):
                for part in range(2):
                    c_src = part * half + j * FFN_COLS
                    c_dst = (2 * j + part) * FFN_COLS
                    dst[rows, c_dst:c_dst + FFN_COLS] = (
                        src[rows, c_src:c_src + FFN_COLS].astype(BF16))

        scores(0)
        for u in range(len(units) + 1):
            if u + 1 < len(units):
                scores(u + 1)
            if u >= 1:
                values(u - 1)
            if u < len(units):
                weights(u)
                for k, r0 in casts[u * casts_per_unit:(u + 1) * casts_per_unit]:
                    cast_slab(k, r0)

    @pl.when(p < PAD_BLOCKS // SUBS)
    def _():
        step(True)

    @pl.when(p >= PAD_BLOCKS // SUBS)
    def _():
        step(False)


def _attention(qkv, bias_base, cast_weights, *, paired_cast, batch, seq):
    n = batch * seq
    rows = SUBS * Q_ROWS
    steps = seq // rows
    n_steps = batch * steps
    cast_blocks = []
    for w in cast_weights:
        assert w.ndim == 2 and w.shape[0] % (n_steps * CAST_ROWS) == 0, w.shape
        cast_blocks.append(pl.BlockSpec((w.shape[0] // n_steps, w.shape[1]),
                                        lambda b, p: (b * steps + p, 0)))
    outs = pl.pallas_call(
        functools.partial(_attn_body, n_cast=len(cast_weights), paired_cast=paired_cast),
        grid=(batch, steps),
        in_specs=[
            pl.BlockSpec((N_HEADS, BIAS_EXT), lambda b, p: (0, 0)),
            pl.BlockSpec((rows, ATTN_WIDTH), lambda b, p: (b * steps + p, 0)),
            pl.BlockSpec((seq, ATTN_WIDTH), lambda b, p: (b, 1)),
            pl.BlockSpec((seq, ATTN_WIDTH), lambda b, p: (b, 2)),
        ] + cast_blocks,
        out_specs=[pl.BlockSpec((rows, ATTN_WIDTH), lambda b, p: (b * steps + p, 0))]
        + cast_blocks,
        out_shape=[jax.ShapeDtypeStruct((n, ATTN_WIDTH), BF16)]
        + [jax.ShapeDtypeStruct(w.shape, BF16) for w in cast_weights],
        scratch_shapes=[
            pltpu.VMEM((PAD_ROWS + seq, ATTN_WIDTH), BF16),
            pltpu.VMEM((PAD_BLOCKS + seq // Q_ROWS, N_PAIRS, PAIR + ONES_ROWS, Q_ROWS), BF16),
            pltpu.VMEM((N_PAIRS, BIAS_ROWS, 2 * Q_ROWS), F32),
            pltpu.VMEM((2, BAND, 2 * Q_ROWS), F32),
            pltpu.VMEM((2, BAND, 2 * Q_ROWS), BF16),
        ],
        compiler_params=_compiler_params(("arbitrary", "arbitrary")),
        name="attn",
    )(bias_base, qkv, qkv, qkv, *cast_weights)
    return outs[0], outs[1:]


def _bias_base(rel_bias):
    far = rel_bias[:, -1:]
    head = jnp.broadcast_to(far, (N_HEADS, Q_ROWS))
    tail = jnp.broadcast_to(far, (N_HEADS, BIAS_EXT - Q_ROWS - 2 * MAX_REL - 1))
    return jnp.concatenate([head, rel_bias, tail], axis=1)


def _mix_body(ya_ref, u_ref, gates_ref, x_ref, pw_ref, ps_ref,
              pa_ref, pb_ref, wo_ref, o_ref, ext_ref, yb_ref, a_ref, *, tiles_per_seq):
    i = pl.program_id(0)
    tm, d = x_ref.shape
    tile_in_seq = i % tiles_per_seq
    n_groups = len(POOL_WINDOWS)
    a_cols = d // n_groups

    @pl.when(tile_in_seq == 0)
    def _():
        ext_ref[0:POOL_HALO, :] = jnp.zeros((POOL_HALO, POOL_WIDTH), F32)

    @pl.when(tile_in_seq != 0)
    def _():
        ext_ref[0:POOL_HALO, :] = ext_ref[tm:, :]

    ext_ref[POOL_HALO:, :] = u_ref[...]

    pos = tile_in_seq * tm + lax.broadcasted_iota(jnp.int32, (tm, 1), 0)
    for g, w in enumerate(POOL_WINDOWS):
        acols = slice(g * a_cols, (g + 1) * a_cols)
        a_ref[:, acols] = jnp.dot(ya_ref[...], pa_ref[:, acols],
                                  preferred_element_type=F32)
        cols = slice(g * POOL_GROUP, (g + 1) * POOL_GROUP)
        acc = ext_ref[:, cols]
        shift = 1
        while shift < w:
            acc = acc + pltpu.roll(acc, shift, 0)
            shift *= 2
        acc = acc[POOL_HALO:]
        cur = ext_ref[POOL_HALO:, cols]
        cnt = jnp.minimum(pos + 1, w).astype(F32)
        dev = (acc / cnt - cur).astype(BF16)
        y = jnp.dot(dev, pw_ref[g], preferred_element_type=F32)
        yb_ref[:, cols] = (y * ps_ref[:, cols]).astype(BF16)

    bb = jnp.dot(yb_ref[...], pb_ref[...], preferred_element_type=F32)
    gate_a = _sigmoid(gates_ref[:, :d].astype(F32))
    gate_b = _sigmoid(gates_ref[:, d:].astype(F32))
    merged = gate_a * a_ref[...] + gate_b * bb
    y = jnp.dot(merged.astype(BF16), wo_ref[...], preferred_element_type=F32)
    o_ref[...] = x_ref[...] + y


def _mix(ya, u, gates, x2d, pool_w, pool_scale, w_a, w_b, w_o, *, seq, tm):
    n, d = x2d.shape
    tiles_per_seq = seq // tm
    const = lambda *shape: pl.BlockSpec(shape, lambda i: (0,) * len(shape),
                                        pipeline_mode=pl.Buffered(1))
    return pl.pallas_call(
        functools.partial(_mix_body, tiles_per_seq=tiles_per_seq),
        grid=(n // tm,),
        in_specs=[
            pl.BlockSpec((tm, ATTN_WIDTH), lambda i: (i, 0)),
            pl.BlockSpec((tm, POOL_WIDTH), lambda i: (i, 0)),
            pl.BlockSpec((tm, 2 * d), lambda i: (i, 0)),
            pl.BlockSpec((tm, d), lambda i: (i, 0)),
            const(len(POOL_WINDOWS), POOL_GROUP, POOL_GROUP),
            const(1, POOL_WIDTH),
            const(ATTN_WIDTH, d),
            const(POOL_WIDTH, d),
            const(d, d),
        ],
        out_specs=pl.BlockSpec((tm, d), lambda i: (i, 0)),
        out_shape=jax.ShapeDtypeStruct((n, d), F32),
        scratch_shapes=[
            pltpu.VMEM((POOL_HALO + tm, POOL_WIDTH), F32),
            pltpu.VMEM((tm, POOL_WIDTH), BF16),
            pltpu.VMEM((tm, d), F32),
        ],
        compiler_params=_compiler_params(("arbitrary",)),
        name="mix",
    )(ya, u, gates, x2d, pool_w, pool_scale, w_a, w_b, w_o)


def _ffn_body(x_ref, g_ref, wgu_ref, wd_ref, gf_ref, o_ref, h_ref):
    j = pl.program_id(1)

    def slab(first, last=False):
        if first:
            x = x_ref[...]
            h = _rms(x, g_ref[...]).astype(BF16)
            h_ref[...] = h
        else:
            h = h_ref[...]
        tf = wd_ref.shape[0]
        gate = jnp.dot(h, wgu_ref[:, :tf], preferred_element_type=F32)
        up = jnp.dot(h, wgu_ref[:, tf:], preferred_element_type=F32)
        act = (gate * _sigmoid(gate) * up).astype(BF16)
        down = jnp.dot(act, wd_ref[...], preferred_element_type=F32)
        total = (x if first else o_ref[...]) + down
        o_ref[...] = _rms(total, gf_ref[...]) if last else total

    last_j = pl.num_programs(1) - 1

    @pl.when(j == 0)
    def _():
        slab(True)

    @pl.when((j > 0) & (j < last_j))
    def _():
        slab(False)

    @pl.when(j == last_j)
    def _():
        slab(False, last=True)


def _ffn(x2d, g, w_gate_up, w_down, g_final, *, tm, tf):
    n, d = x2d.shape
    d_ff = w_down.shape[0]
    n_f = d_ff // tf
    return pl.pallas_call(
        _ffn_body,
        grid=(n // tm, n_f),
        in_specs=[
            pl.BlockSpec((tm, d), lambda i, j: (i, 0)),
            pl.BlockSpec((1, d), lambda i, j: (0, 0)),
            pl.BlockSpec((d, 2 * tf), lambda i, j: (0, j)),
            pl.BlockSpec((tf, d), lambda i, j: (j, 0)),
            pl.BlockSpec((1, d), lambda i, j: (0, 0)),
        ],
        out_specs=pl.BlockSpec((tm, d), lambda i, j: (i, 0)),
        out_shape=jax.ShapeDtypeStruct((n, d), F32),
        scratch_shapes=[pltpu.VMEM((tm, d), BF16)],
        compiler_params=_compiler_params(("arbitrary", "arbitrary")),
        name="ffn",
    )(x2d, g, w_gate_up, w_down, g_final)


def _layer(x2d, norm_mix, w_in, rel_bias, pool_w, pool_scale, w_a, w_b, w_o,
           norm_ffn, w_gate_up, w_down, norm_out, *, batch, seq):
    d = x2d.shape[1]
    qkv, u, gates = _in_proj(x2d, norm_mix.reshape(1, d), w_in, tm=IN_PROJ_ROWS)
    pool_w2d = pool_w.reshape(-1, pool_w.shape[-1])
    ya, (pool_w2d, w_a, w_b, w_o, w_gate_up, w_down) = _attention(
        qkv, _bias_base(rel_bias), (pool_w2d, w_a, w_b, w_o, w_gate_up, w_down),
        paired_cast=4, batch=batch, seq=seq)
    x1 = _mix(ya, u, gates, x2d, pool_w2d.reshape(pool_w.shape), pool_scale.reshape(1, -1),
              w_a, w_b, w_o, seq=seq, tm=MIX_ROWS)
    return _ffn(x1, norm_ffn.reshape(1, d), w_gate_up, w_down,
                norm_out.reshape(1, d), tm=FFN_ROWS, tf=FFN_COLS)


def kernel(x, norm_mix, w_in, rel_bias, pool_w, pool_scale, w_branch_a, w_branch_b,
           w_out, norm_ffn, w_gate_up, w_down, norm_final):
    batch, seq, d = x.shape
    depth = w_in.shape[0]
    assert depth == 1, "the fused ffn kernel applies the final norm to the only layer"
    out = _layer(x.reshape(batch * seq, d), norm_mix[0], w_in[0], rel_bias[0],
                 pool_w[0], pool_scale[0], w_branch_a[0], w_branch_b[0], w_out[0],
                 norm_ffn[0], w_gate_up[0], w_down[0], norm_final,
                 batch=batch, seq=seq)
    return out.reshape(batch, seq, d)
```

```python
import functools
import math

import jax
import jax.numpy as jnp
from jax import lax
from jax.experimental import pallas as pl
from jax.experimental.pallas import tpu as pltpu

F32 = jnp.float32
BF16 = jnp.bfloat16

CHUNK = 64
LEFT_CHUNKS = 8
N_HEADS = 16
HEAD_DIM = 64
ATTN_WIDTH = N_HEADS * HEAD_DIM
MAX_REL = 128
POOL_WINDOWS = (2, 4, 8, 16)
POOL_GROUP = 256
POOL_WIDTH = POOL_GROUP * len(POOL_WINDOWS)
EPS = 1e-6
NEG_INF = -1e30
LOG2_E = math.log2(math.e)

Q_ROWS = 2 * CHUNK
BAND = (LEFT_CHUNKS + 2) * CHUNK
PAD_ROWS = LEFT_CHUNKS * CHUNK
BIAS_EXT = BAND + Q_ROWS
POOL_HALO = 16

V7X_VMEM_BYTES = 64 * 1024 * 1024
VMEM_LIMIT_BYTES = V7X_VMEM_BYTES - 4 * 1024 * 1024
IN_PROJ_ROWS = 256
STAGE_ROWS = 128
MIX_ROWS = 256
FFN_ROWS = 1024
FFN_COLS = 512


def _rms(x, g):
    ms = jnp.mean(x * x, axis=-1, keepdims=True)
    return x * lax.rsqrt(ms + EPS) * g


def _sigmoid(x):
    return 0.5 * jnp.tanh(0.5 * x) + 0.5


def _compiler_params(semantics):
    return pltpu.CompilerParams(
        dimension_semantics=semantics, vmem_limit_bytes=VMEM_LIMIT_BYTES)


def _in_proj_body(x_ref, g_ref, w_hbm, qkv_ref, u_ref, gate_ref, w_ref, stage_ref, sem):
    n_chunks = w_ref.shape[0] // STAGE_ROWS

    def chunk_copy(c, slot):
        rows = pl.ds(pl.multiple_of(c * STAGE_ROWS, STAGE_ROWS), STAGE_ROWS)
        return pltpu.make_async_copy(w_hbm.at[rows], stage_ref.at[slot], sem.at[slot])

    @pl.when(pl.program_id(0) == 0)
    def _():
        chunk_copy(0, 0).start()

        def two_chunks(cc, carry):
            for slot in range(2):
                c = 2 * cc + slot

                @pl.when(c + 1 < n_chunks)
                def _():
                    chunk_copy(c + 1, 1 - slot).start()

                chunk_copy(c, slot).wait()
                rows = pl.ds(pl.multiple_of(c * STAGE_ROWS, STAGE_ROWS), STAGE_ROWS)
                w_ref[rows, :] = stage_ref[slot].astype(BF16)
            return carry

        lax.fori_loop(0, n_chunks // 2, two_chunks, 0)

    h = _rms(x_ref[...], g_ref[...]).astype(BF16)
    q_end = qkv_ref.shape[1]
    u_end = q_end + u_ref.shape[1]
    qkv_ref[...] = jnp.dot(h, w_ref[:, :q_end], preferred_element_type=F32).astype(BF16)
    u_ref[...] = jnp.dot(h, w_ref[:, q_end:u_end], preferred_element_type=F32)
    gate_ref[...] = jnp.dot(h, w_ref[:, u_end:], preferred_element_type=F32).astype(BF16)


def _in_proj(x2d, g, w, *, tm):
    n, d = x2d.shape
    in_width = w.shape[1]
    n_gate = in_width - 3 * ATTN_WIDTH - POOL_WIDTH
    assert d % (2 * STAGE_ROWS) == 0
    return pl.pallas_call(
        _in_proj_body,
        grid=(n // tm,),
        in_specs=[
            pl.BlockSpec((tm, d), lambda i: (i, 0)),
            pl.BlockSpec((1, d), lambda i: (0, 0)),
            pl.BlockSpec(memory_space=pl.ANY),
        ],
        out_specs=[
            pl.BlockSpec((tm, 3 * ATTN_WIDTH), lambda i: (i, 0)),
            pl.BlockSpec((tm, POOL_WIDTH), lambda i: (i, 0)),
            pl.BlockSpec((tm, n_gate), lambda i: (i, 0)),
        ],
        out_shape=[
            jax.ShapeDtypeStruct((n, 3 * ATTN_WIDTH), BF16),
            jax.ShapeDtypeStruct((n, POOL_WIDTH), F32),
            jax.ShapeDtypeStruct((n, n_gate), BF16),
        ],
        scratch_shapes=[
            pltpu.VMEM((d, in_width), BF16),
            pltpu.VMEM((2, STAGE_ROWS, in_width), F32),
            pltpu.SemaphoreType.DMA((2,)),
        ],
        compiler_params=_compiler_params(("arbitrary",)),
        name="in_proj",
    )(x2d, g, w)


PAIR = 2 * HEAD_DIM
N_PAIRS = N_HEADS // 2
WIN_BLOCKS = BAND // Q_ROWS
PAD_BLOCKS = PAD_ROWS // Q_ROWS
FLAT_END = PAD_ROWS - MAX_REL
BIAS_ROWS = BAND - FLAT_END
SEEN = BAND - CHUNK
ONES_ROWS = 16
SUBS = 4
CAST_ROWS = 16


def _attn_body(base_ref, q_ref, k_ref, v_ref, *rest, n_cast, paired_cast):
    cast_in = rest[:n_cast]
    o_ref = rest[n_cast]
    cast_out = rest[n_cast + 1:2 * n_cast + 1]
    kpad, vtpad, bias_ref, s_ref, p_ref = rest[2 * n_cast + 1:]
    b = pl.program_id(0)
    p = pl.program_id(1)

    @pl.when((b == 0) & (p == 0))
    def _():
        for hp in range(N_PAIRS):
            heads = []
            for a in range(2):
                h = 2 * hp + a
                base = jnp.broadcast_to(base_ref[h:h + 1, :], (BAND, BIAS_EXT))
                rot = pltpu.roll(base, 0, 1, stride=1, stride_axis=0)
                heads.append((rot[FLAT_END:, :Q_ROWS] - base[FLAT_END:, 0:1]) * LOG2_E)
            bias_ref[hp] = jnp.concatenate(
                [heads[a][:, c * CHUNK:(c + 1) * CHUNK] for c in range(2) for a in range(2)],
                axis=1)
        kpad[0:PAD_ROWS, :] = jnp.zeros((PAD_ROWS, ATTN_WIDTH), BF16)
        vtpad[0:PAD_BLOCKS, :, 0:PAIR, :] = jnp.zeros(
            (PAD_BLOCKS, N_PAIRS, PAIR, Q_ROWS), BF16)
        vtpad[:, :, PAIR:, :] = jnp.ones(
            (vtpad.shape[0], N_PAIRS, ONES_ROWS, Q_ROWS), BF16)
        p_ref[:, SEEN:, :Q_ROWS] = jnp.zeros((2, CHUNK, Q_ROWS), BF16)
        p_ref[:, :CHUNK, Q_ROWS:] = jnp.zeros((2, CHUNK, Q_ROWS), BF16)

    @pl.when(p == 0)
    def _():
        kpad[PAD_ROWS:, :] = k_ref[...]
        for c in range(v_ref.shape[0] // Q_ROWS):
            blk_t = v_ref[c * Q_ROWS:(c + 1) * Q_ROWS, :].astype(F32).T.astype(BF16)
            for hp in range(N_PAIRS):
                vtpad[PAD_BLOCKS + c, hp, 0:PAIR, :] = blk_t[hp * PAIR:(hp + 1) * PAIR, :]

    lane = lax.broadcasted_iota(jnp.int32, (CHUNK, PAIR), 1)
    first_head = lane < HEAD_DIM
    row = lax.broadcasted_iota(jnp.int32, (PAIR, Q_ROWS), 0)
    first_head_rows = row < HEAD_DIM
    first_chunk_lanes = lax.broadcasted_iota(jnp.int32, (PAIR, Q_ROWS), 1) < CHUNK
    scale = LOG2_E / math.sqrt(HEAD_DIM)

    def step(mask_missing_keys):
        units = [(sub, hp) for sub in range(SUBS) for hp in range(N_PAIRS)]

        def scores(u):
            sub, hp = units[u]
            blk = p * SUBS + sub
            lanes = slice(hp * PAIR, (hp + 1) * PAIR)
            q2 = (q_ref[sub * Q_ROWS:(sub + 1) * Q_ROWS, lanes].astype(F32) * scale).astype(BF16)
            zero = jnp.zeros((CHUNK, PAIR), BF16)
            qm = jnp.concatenate(
                [jnp.where(first_head if a == 0 else jnp.logical_not(first_head),
                           q2[c * CHUNK:(c + 1) * CHUNK], zero)
                 for c in range(2) for a in range(2)], axis=0)
            qm_t = qm.astype(F32).T.astype(BF16)
            k2 = kpad[pl.ds(pl.multiple_of(blk * Q_ROWS, Q_ROWS), BAND), lanes]
            s = jnp.dot(k2, qm_t, preferred_element_type=F32)
            pieces = [(0, FLAT_END, 0, None), (FLAT_END, SEEN, 0, (0, SEEN - FLAT_END)),
                      (CHUNK, FLAT_END, Q_ROWS, None), (FLAT_END, BAND, Q_ROWS, (0, BIAS_ROWS))]
            for r0, r1, l0, bias_rows in pieces:
                part = s[r0:r1, l0:l0 + Q_ROWS]
                if bias_rows is not None:
                    part = part + bias_ref[hp, bias_rows[0]:bias_rows[1], l0:l0 + Q_ROWS]
                if mask_missing_keys:
                    key = r0 + lax.broadcasted_iota(jnp.int32, part.shape, 0)
                    part = jnp.where(key >= PAD_ROWS - blk * Q_ROWS, part, NEG_INF)
                s_ref[u % 2, r0:r1, l0:l0 + Q_ROWS] = part

        def weights(u):
            slot = u % 2
            for r0, l0 in ((0, 0), (CHUNK, Q_ROWS)):
                s = s_ref[slot, r0:r0 + SEEN, l0:l0 + Q_ROWS]
                m = jnp.max(s, axis=0, keepdims=True)
                p_ref[slot, r0:r0 + SEEN, l0:l0 + Q_ROWS] = jnp.exp2(s - m).astype(BF16)

        def values(u):
            sub, hp = units[u]
            blk = p * SUBS + sub
            vt2 = jnp.concatenate(
                [vtpad[blk + t, hp] for t in range(WIN_BLOCKS)], axis=1)
            o_t = jnp.dot(vt2, p_ref[u % 2], preferred_element_type=F32)
            o_t = o_t[0:PAIR] / o_t[PAIR:PAIR + 1]
            t0, t1 = o_t[:, :Q_ROWS], o_t[:, Q_ROWS:]
            head_a = jnp.where(first_chunk_lanes, t0, pltpu.roll(t1, CHUNK, 1))
            head_b = jnp.where(first_chunk_lanes, pltpu.roll(t0, CHUNK, 1), t1)
            own = jnp.where(first_head_rows, head_a, head_b)
            o_ref[sub * Q_ROWS:(sub + 1) * Q_ROWS, hp * PAIR:(hp + 1) * PAIR] = (
                own.T.astype(BF16))

        casts = [(k, r0) for k, src in enumerate(cast_in)
                 for r0 in range(0, src.shape[0], CAST_ROWS)]
        casts_per_unit = pl.cdiv(len(casts), len(units))

        def cast_slab(k, r0):
            src, dst = cast_in[k], cast_out[k]
            rows = slice(r0, r0 + CAST_ROWS)
            if k != paired_cast:
                dst[rows, :] = src[rows, :].astype(BF16)
                return
            half = src.shape[1] // 2
            for j in range(half // FFN_COLS):
                for part in range(2):
                    c_src = part * half + j * FFN_COLS
                    c_dst = (2 * j + part) * FFN_COLS
                    dst[rows, c_dst:c_dst + FFN_COLS] = (
                        src[rows, c_src:c_src + FFN_COLS].astype(BF16))

        scores(0)
        for u in range(len(units) + 1):
            if u + 1 < len(units):
                scores(u + 1)
            if u >= 1:
                values(u - 1)
            if u < len(units):
                weights(u)
                for k, r0 in casts[u * casts_per_unit:(u + 1) * casts_per_unit]:
                    cast_slab(k, r0)

    @pl.when(p < PAD_BLOCKS // SUBS)
    def _():
        step(True)

    @pl.when(p >= PAD_BLOCKS // SUBS)
    def _():
        step(False)


def _attention(qkv, bias_base, cast_weights, *, paired_cast, batch, seq):
    n = batch * seq
    rows = SUBS * Q_ROWS
    steps = seq // rows
    n_steps = batch * steps
    cast_blocks = []
    for w in cast_weights:
        assert w.ndim == 2 and w.shape[0] % (n_steps * CAST_ROWS) == 0, w.shape
        cast_blocks.append(pl.BlockSpec((w.shape[0] // n_steps, w.shape[1]),
                                        lambda b, p: (b * steps + p, 0)))
    outs = pl.pallas_call(
        functools.partial(_attn_body, n_cast=len(cast_weights), paired_cast=paired_cast),
        grid=(batch, steps),
        in_specs=[
            pl.BlockSpec((N_HEADS, BIAS_EXT), lambda b, p: (0, 0)),
            pl.BlockSpec((rows, ATTN_WIDTH), lambda b, p: (b * steps + p, 0)),
            pl.BlockSpec((seq, ATTN_WIDTH), lambda b, p: (b, 1)),
            pl.BlockSpec((seq, ATTN_WIDTH), lambda b, p: (b, 2)),
        ] + cast_blocks,
        out_specs=[pl.BlockSpec((rows, ATTN_WIDTH), lambda b, p: (b * steps + p, 0))]
        + cast_blocks,
        out_shape=[jax.ShapeDtypeStruct((n, ATTN_WIDTH), BF16)]
        + [jax.ShapeDtypeStruct(w.shape, BF16) for w in cast_weights],
        scratch_shapes=[
            pltpu.VMEM((PAD_ROWS + seq, ATTN_WIDTH), BF16),
            pltpu.VMEM((PAD_BLOCKS + seq // Q_ROWS, N_PAIRS, PAIR + ONES_ROWS, Q_ROWS), BF16),
            pltpu.VMEM((N_PAIRS, BIAS_ROWS, 2 * Q_ROWS), F32),
            pltpu.VMEM((2, BAND, 2 * Q_ROWS), F32),
            pltpu.VMEM((2, BAND, 2 * Q_ROWS), BF16),
        ],
        compiler_params=_compiler_params(("arbitrary", "arbitrary")),
        name="attn",
    )(bias_base, qkv, qkv, qkv, *cast_weights)
    return outs[0], outs[1:]


def _bias_base(rel_bias):
    far = rel_bias[:, -1:]
    head = jnp.broadcast_to(far, (N_HEADS, Q_ROWS))
    tail = jnp.broadcast_to(far, (N_HEADS, BIAS_EXT - Q_ROWS - 2 * MAX_REL - 1))
    return jnp.concatenate([head, rel_bias, tail], axis=1)


def _mix_body(ya_ref, u_ref, gates_ref, x_ref, pw_ref, ps_ref,
              pa_ref, pb_ref, wo_ref, o_ref, ext_ref, yb_ref, a_ref, *, tiles_per_seq):
    i = pl.program_id(0)
    tm, d = x_ref.shape
    tile_in_seq = i % tiles_per_seq
    n_groups = len(POOL_WINDOWS)
    a_cols = d // n_groups

    @pl.when(tile_in_seq == 0)
    def _():
        ext_ref[0:POOL_HALO, :] = jnp.zeros((POOL_HALO, POOL_WIDTH), F32)

    @pl.when(tile_in_seq != 0)
    def _():
        ext_ref[0:POOL_HALO, :] = ext_ref[tm:, :]

    ext_ref[POOL_HALO:, :] = u_ref[...]

    pos = tile_in_seq * tm + lax.broadcasted_iota(jnp.int32, (tm, 1), 0)
    for g, w in enumerate(POOL_WINDOWS):
        acols = slice(g * a_cols, (g + 1) * a_cols)
        a_ref[:, acols] = jnp.dot(ya_ref[...], pa_ref[:, acols],
                                  preferred_element_type=F32)
        cols = slice(g * POOL_GROUP, (g + 1) * POOL_GROUP)
        acc = ext_ref[:, cols]
        shift = 1
        while shift < w:
            acc = acc + pltpu.roll(acc, shift, 0)
            shift *= 2
        acc = acc[POOL_HALO:]
        cur = ext_ref[POOL_HALO:, cols]
        cnt = jnp.minimum(pos + 1, w).astype(F32)
        dev = (acc / cnt - cur).astype(BF16)
        y = jnp.dot(dev, pw_ref[g], preferred_element_type=F32)
        yb_ref[:, cols] = (y * ps_ref[:, cols]).astype(BF16)

    bb = jnp.dot(yb_ref[...], pb_ref[...], preferred_element_type=F32)
    gate_a = _sigmoid(gates_ref[:, :d].astype(F32))
    gate_b = _sigmoid(gates_ref[:, d:].astype(F32))
    merged = gate_a * a_ref[...] + gate_b * bb
    y = jnp.dot(merged.astype(BF16), wo_ref[...], preferred_element_type=F32)
    o_ref[...] = x_ref[...] + y


def _mix(ya, u, gates, x2d, pool_w, pool_scale, w_a, w_b, w_o, *, seq, tm):
    n, d = x2d.shape
    tiles_per_seq = seq // tm
    const = lambda *shape: pl.BlockSpec(shape, lambda i: (0,) * len(shape),
                                        pipeline_mode=pl.Buffered(1))
    return pl.pallas_call(
        functools.partial(_mix_body, tiles_per_seq=tiles_per_seq),
        grid=(n // tm,),
        in_specs=[
            pl.BlockSpec((tm, ATTN_WIDTH), lambda i: (i, 0)),
            pl.BlockSpec((tm, POOL_WIDTH), lambda i: (i, 0)),
            pl.BlockSpec((tm, 2 * d), lambda i: (i, 0)),
            pl.BlockSpec((tm, d), lambda i: (i, 0)),
            const(len(POOL_WINDOWS), POOL_GROUP, POOL_GROUP),
            const(1, POOL_WIDTH),
            const(ATTN_WIDTH, d),
            const(POOL_WIDTH, d),
            const(d, d),
        ],
        out_specs=pl.BlockSpec((tm, d), lambda i: (i, 0)),
        out_shape=jax.ShapeDtypeStruct((n, d), F32),
        scratch_shapes=[
            pltpu.VMEM((POOL_HALO + tm, POOL_WIDTH), F32),
            pltpu.VMEM((tm, POOL_WIDTH), BF16),
            pltpu.VMEM((tm, d), F32),
        ],
        compiler_params=_compiler_params(("arbitrary",)),
        name="mix",
    )(ya, u, gates, x2d, pool_w, pool_scale, w_a, w_b, w_o)


def _ffn_body(x_ref, g_ref, wgu_ref, wd_ref, gf_ref, o_ref, h_ref):
    j = pl.program_id(1)

    def slab(first, last=False):
        if first:
            x = x_ref[...]
            h = _rms(x, g_ref[...]).astype(BF16)
            h_ref[...] = h
        else:
            h = h_ref[...]
        tf = wd_ref.shape[0]
        gate = jnp.dot(h, wgu_ref[:, :tf], preferred_element_type=F32)
        up = jnp.dot(h, wgu_ref[:, tf:], preferred_element_type=F32)
        act = (gate * _sigmoid(gate) * up).astype(BF16)
        down = jnp.dot(act, wd_ref[...], preferred_element_type=F32)
        total = (x if first else o_ref[...]) + down
        o_ref[...] = _rms(total, gf_ref[...]) if last else total

    last_j = pl.num_programs(1) - 1

    @pl.when(j == 0)
    def _():
        slab(True)

    @pl.when((j > 0) & (j < last_j))
    def _():
        slab(False)

    @pl.when(j == last_j)
    def _():
        slab(False, last=True)


def _ffn(x2d, g, w_gate_up, w_down, g_final, *, tm, tf):
    n, d = x2d.shape
    d_ff = w_down.shape[0]
    n_f = d_ff // tf
    return pl.pallas_call(
        _ffn_body,
        grid=(n // tm, n_f),
        in_specs=[
            pl.BlockSpec((tm, d), lambda i, j: (i, 0)),
            pl.BlockSpec((1, d), lambda i, j: (0, 0)),
            pl.BlockSpec((d, 2 * tf), lambda i, j: (0, j)),
            pl.BlockSpec((tf, d), lambda i, j: (j, 0)),
            pl.BlockSpec((1, d), lambda i, j: (0, 0)),
        ],
        out_specs=pl.BlockSpec((tm, d), lambda i, j: (i, 0)),
        out_shape=jax.ShapeDtypeStruct((n, d), F32),
        scratch_shapes=[pltpu.VMEM((tm, d), BF16)],
        compiler_params=_compiler_params(("arbitrary", "arbitrary")),
        name="ffn",
    )(x2d, g, w_gate_up, w_down, g_final)


def _layer(x2d, norm_mix, w_in, rel_bias, pool_w, pool_scale, w_a, w_b, w_o,
           norm_ffn, w_gate_up, w_down, norm_out, *, batch, seq):
    d = x2d.shape[1]
    qkv, u, gates = _in_proj(x2d, norm_mix.reshape(1, d), w_in, tm=IN_PROJ_ROWS)
    pool_w2d = pool_w.reshape(-1, pool_w.shape[-1])
    ya, (pool_w2d, w_a, w_b, w_o, w_gate_up, w_down) = _attention(
        qkv, _bias_base(rel_bias), (pool_w2d, w_a, w_b, w_o, w_gate_up, w_down),
        paired_cast=4, batch=batch, seq=seq)
    x1 = _mix(ya, u, gates, x2d, pool_w2d.reshape(pool_w.shape), pool_scale.reshape(1, -1),
              w_a, w_b, w_o, seq=seq, tm=MIX_ROWS)
    return _ffn(x1, norm_ffn.reshape(1, d), w_gate_up, w_down,
                norm_out.reshape(1, d), tm=FFN_ROWS, tf=FFN_COLS)


def kernel(x, norm_mix, w_in, rel_bias, pool_w, pool_scale, w_branch_a, w_branch_b,
           w_out, norm_ffn, w_gate_up, w_down, norm_final):
    batch, seq, d = x.shape
    depth = w_in.shape[0]
    assert depth == 1, "the fused ffn kernel applies the final norm to the only layer"
    out = _layer(x.reshape(batch * seq, d), norm_mix[0], w_in[0], rel_bias[0],
                 pool_w[0], pool_scale[0], w_branch_a[0], w_branch_b[0], w_out[0],
                 norm_ffn[0], w_gate_up[0], w_down[0], norm_final,
                 batch=batch, seq=seq)
    return out.reshape(batch, seq, d)
```

```python
import functools
import math

import jax
import jax.numpy as jnp
from jax import lax
from jax.experimental import pallas as pl
from jax.experimental.pallas import tpu as pltpu

F32 = jnp.float32
BF16 = jnp.bfloat16

CHUNK = 64
LEFT_CHUNKS = 8
N_HEADS = 16
HEAD_DIM = 64
ATTN_WIDTH = N_HEADS * HEAD_DIM
MAX_REL = 128
POOL_WINDOWS = (2, 4, 8, 16)
POOL_GROUP = 256
POOL_WIDTH = POOL_GROUP * len(POOL_WINDOWS)
EPS = 1e-6
NEG_INF = -1e30
LOG2_E = math.log2(math.e)

Q_ROWS = 2 * CHUNK
BAND = (LEFT_CHUNKS + 2) * CHUNK
PAD_ROWS = LEFT_CHUNKS * CHUNK
BIAS_EXT = BAND + Q_ROWS
POOL_HALO = 16

V7X_VMEM_BYTES = 64 * 1024 * 1024
VMEM_LIMIT_BYTES = V7X_VMEM_BYTES - 4 * 1024 * 1024
IN_PROJ_ROWS = 256
STAGE_ROWS = 128
MIX_ROWS = 256
FFN_ROWS = 1024
FFN_COLS = 512


def _rms(x, g):
    ms = jnp.mean(x * x, axis=-1, keepdims=True)
    return x * lax.rsqrt(ms + EPS) * g


def _compiler_params(semantics):
    return pltpu.CompilerParams(
        dimension_semantics=semantics, vmem_limit_bytes=VMEM_LIMIT_BYTES)


def _in_proj_body(x_ref, g_ref, w_hbm, qkv_ref, u_ref, gate_ref, w_ref, stage_ref, sem):
    n_chunks = w_ref.shape[0] // STAGE_ROWS

    def chunk_copy(c, slot):
        rows = pl.ds(pl.multiple_of(c * STAGE_ROWS, STAGE_ROWS), STAGE_ROWS)
        return pltpu.make_async_copy(w_hbm.at[rows], stage_ref.at[slot], sem.at[slot])

    @pl.when(pl.program_id(0) == 0)
    def _():
        chunk_copy(0, 0).start()

        def two_chunks(cc, carry):
            for slot in range(2):
                c = 2 * cc + slot

                @pl.when(c + 1 < n_chunks)
                def _():
                    chunk_copy(c + 1, 1 - slot).start()

                chunk_copy(c, slot).wait()
                rows = pl.ds(pl.multiple_of(c * STAGE_ROWS, STAGE_ROWS), STAGE_ROWS)
                gate0 = w_ref.shape[1] - gate_ref.shape[1]
                w_ref[rows, :gate0] = stage_ref[slot, :, :gate0].astype(BF16)
                w_ref[rows, gate0:] = (0.5 * stage_ref[slot, :, gate0:]).astype(BF16)
            return carry

        lax.fori_loop(0, n_chunks // 2, two_chunks, 0)

    h = _rms(x_ref[...], g_ref[...]).astype(BF16)
    q_end = qkv_ref.shape[1]
    u_end = q_end + u_ref.shape[1]
    qkv_ref[...] = jnp.dot(h, w_ref[:, :q_end], preferred_element_type=F32).astype(BF16)
    u_ref[...] = jnp.dot(h, w_ref[:, q_end:u_end], preferred_element_type=F32)
    gate_ref[...] = jnp.dot(h, w_ref[:, u_end:], preferred_element_type=F32).astype(BF16)


def _in_proj(x2d, g, w, *, tm):
    n, d = x2d.shape
    in_width = w.shape[1]
    n_gate = in_width - 3 * ATTN_WIDTH - POOL_WIDTH
    assert d % (2 * STAGE_ROWS) == 0
    return pl.pallas_call(
        _in_proj_body,
        grid=(n // tm,),
        in_specs=[
            pl.BlockSpec((tm, d), lambda i: (i, 0)),
            pl.BlockSpec((1, d), lambda i: (0, 0)),
            pl.BlockSpec(memory_space=pl.ANY),
        ],
        out_specs=[
            pl.BlockSpec((tm, 3 * ATTN_WIDTH), lambda i: (i, 0)),
            pl.BlockSpec((tm, POOL_WIDTH), lambda i: (i, 0)),
            pl.BlockSpec((tm, n_gate), lambda i: (i, 0)),
        ],
        out_shape=[
            jax.ShapeDtypeStruct((n, 3 * ATTN_WIDTH), BF16),
            jax.ShapeDtypeStruct((n, POOL_WIDTH), F32),
            jax.ShapeDtypeStruct((n, n_gate), BF16),
        ],
        scratch_shapes=[
            pltpu.VMEM((d, in_width), BF16),
            pltpu.VMEM((2, STAGE_ROWS, in_width), F32),
            pltpu.SemaphoreType.DMA((2,)),
        ],
        compiler_params=_compiler_params(("arbitrary",)),
        name="in_proj",
    )(x2d, g, w)


PAIR = 2 * HEAD_DIM
N_PAIRS = N_HEADS // 2
WIN_BLOCKS = BAND // Q_ROWS
PAD_BLOCKS = PAD_ROWS // Q_ROWS
FLAT_END = PAD_ROWS - MAX_REL
BIAS_ROWS = BAND - FLAT_END
SEEN = BAND - CHUNK
ONES_ROWS = 16
SUBS = 4
CAST_ROWS = 16


def _attn_body(base_ref, q_ref, k_ref, v_ref, *rest, n_cast, paired_cast):
    cast_in = rest[:n_cast]
    o_ref = rest[n_cast]
    cast_out = rest[n_cast + 1:2 * n_cast + 1]
    kpad, vtpad, bias_ref, s_ref, p_ref = rest[2 * n_cast + 1:]
    b = pl.program_id(0)
    p = pl.program_id(1)

    @pl.when((b == 0) & (p == 0))
    def _():
        for hp in range(N_PAIRS):
            heads = []
            for a in range(2):
                h = 2 * hp + a
                base = jnp.broadcast_to(base_ref[h:h + 1, :], (BAND, BIAS_EXT))
                rot = pltpu.roll(base, 0, 1, stride=1, stride_axis=0)
                heads.append((rot[FLAT_END:, :Q_ROWS] - base[FLAT_END:, 0:1]) * LOG2_E)
            bias_ref[hp] = jnp.concatenate(
                [heads[a][:, c * CHUNK:(c + 1) * CHUNK] for c in range(2) for a in range(2)],
                axis=1)
        kpad[0:PAD_ROWS, :] = jnp.zeros((PAD_ROWS, ATTN_WIDTH), BF16)
        vtpad[0:PAD_BLOCKS, :, 0:PAIR, :] = jnp.zeros(
            (PAD_BLOCKS, N_PAIRS, PAIR, Q_ROWS), BF16)
        vtpad[:, :, PAIR:, :] = jnp.ones(
            (vtpad.shape[0], N_PAIRS, ONES_ROWS, Q_ROWS), BF16)
        p_ref[:, SEEN:, :Q_ROWS] = jnp.zeros((2, CHUNK, Q_ROWS), BF16)
        p_ref[:, :CHUNK, Q_ROWS:] = jnp.zeros((2, CHUNK, Q_ROWS), BF16)

    @pl.when(p == 0)
    def _():
        kpad[PAD_ROWS:, :] = k_ref[...]
        for c in range(v_ref.shape[0] // Q_ROWS):
            blk_t = v_ref[c * Q_ROWS:(c + 1) * Q_ROWS, :].astype(F32).T.astype(BF16)
            for hp in range(N_PAIRS):
                vtpad[PAD_BLOCKS + c, hp, 0:PAIR, :] = blk_t[hp * PAIR:(hp + 1) * PAIR, :]

    lane = lax.broadcasted_iota(jnp.int32, (CHUNK, PAIR), 1)
    first_head = lane < HEAD_DIM
    row = lax.broadcasted_iota(jnp.int32, (PAIR, Q_ROWS), 0)
    first_head_rows = row < HEAD_DIM
    first_chunk_lanes = lax.broadcasted_iota(jnp.int32, (PAIR, Q_ROWS), 1) < CHUNK
    scale = LOG2_E / math.sqrt(HEAD_DIM)

    def step(mask_missing_keys):
        units = [(sub, hp) for sub in range(SUBS) for hp in range(N_PAIRS)]

        def scores(u):
            sub, hp = units[u]
            blk = p * SUBS + sub
            lanes = slice(hp * PAIR, (hp + 1) * PAIR)
            q2 = (q_ref[sub * Q_ROWS:(sub + 1) * Q_ROWS, lanes].astype(F32) * scale).astype(BF16)
            zero = jnp.zeros((CHUNK, PAIR), BF16)
            qm = jnp.concatenate(
                [jnp.where(first_head if a == 0 else jnp.logical_not(first_head),
                           q2[c * CHUNK:(c + 1) * CHUNK], zero)
                 for c in range(2) for a in range(2)], axis=0)
            qm_t = qm.astype(F32).T.astype(BF16)
            k2 = kpad[pl.ds(pl.multiple_of(blk * Q_ROWS, Q_ROWS), BAND), lanes]
            s = jnp.dot(k2, qm_t, preferred_element_type=F32)
            pieces = [(0, FLAT_END, 0, None), (FLAT_END, SEEN, 0, (0, SEEN - FLAT_END)),
                      (CHUNK, FLAT_END, Q_ROWS, None), (FLAT_END, BAND, Q_ROWS, (0, BIAS_ROWS))]
            for r0, r1, l0, bias_rows in pieces:
                part = s[r0:r1, l0:l0 + Q_ROWS]
                if bias_rows is not None:
                    part = part + bias_ref[hp, bias_rows[0]:bias_rows[1], l0:l0 + Q_ROWS]
                if mask_missing_keys:
                    key = r0 + lax.broadcasted_iota(jnp.int32, part.shape, 0)
                    part = jnp.where(key >= PAD_ROWS - blk * Q_ROWS, part, NEG_INF)
                s_ref[u % 2, r0:r1, l0:l0 + Q_ROWS] = part

        def weights(u):
            slot = u % 2
            for r0, l0 in ((0, 0), (CHUNK, Q_ROWS)):
                s = s_ref[slot, r0:r0 + SEEN, l0:l0 + Q_ROWS]
                m = jnp.max(s, axis=0, keepdims=True)
                p_ref[slot, r0:r0 + SEEN, l0:l0 + Q_ROWS] = jnp.exp2(s - m).astype(BF16)

        def values(u):
            sub, hp = units[u]
            blk = p * SUBS + sub
            vt2 = jnp.concatenate(
                [vtpad[blk + t, hp] for t in range(WIN_BLOCKS)], axis=1)
            o_t = jnp.dot(vt2, p_ref[u % 2], preferred_element_type=F32)
            o_t = o_t[0:PAIR] / o_t[PAIR:PAIR + 1]
            t0, t1 = o_t[:, :Q_ROWS], o_t[:, Q_ROWS:]
            head_a = jnp.where(first_chunk_lanes, t0, pltpu.roll(t1, CHUNK, 1))
            head_b = jnp.where(first_chunk_lanes, pltpu.roll(t0, CHUNK, 1), t1)
            own = jnp.where(first_head_rows, head_a, head_b)
            o_ref[sub * Q_ROWS:(sub + 1) * Q_ROWS, hp * PAIR:(hp + 1) * PAIR] = (
                own.T.astype(BF16))

        casts = [(k, r0) for k, src in enumerate(cast_in)
                 for r0 in range(0, src.shape[0], CAST_ROWS)]
        casts_per_unit = pl.cdiv(len(casts), len(units))

        def cast_slab(k, r0):
            src, dst = cast_in[k], cast_out[k]
            rows = slice(r0, r0 + CAST_ROWS)
            if k != paired_cast:
                dst[rows, :] = src[rows, :].astype(BF16)
                return
            half = src.shape[1] // 2
            for j in range(half // FFN_COLS):
                for part in range(2):
                    c_src = part * half + j * FFN_COLS
                    c_dst = (2 * j + part) * FFN_COLS
                    w = src[rows, c_src:c_src + FFN_COLS]
                    dst[rows, c_dst:c_dst + FFN_COLS] = (0.5 * w if part == 0 else w).astype(BF16)

        scores(0)
        for u in range(len(units) + 1):
            if u + 1 < len(units):
                scores(u + 1)
            if u >= 1:
                values(u - 1)
            if u < len(units):
                weights(u)
                for k, r0 in casts[u * casts_per_unit:(u + 1) * casts_per_unit]:
                    cast_slab(k, r0)

    @pl.when(p < PAD_BLOCKS // SUBS)
    def _():
        step(True)

    @pl.when(p >= PAD_BLOCKS // SUBS)
    def _():
        step(False)


def _attention(qkv, bias_base, cast_weights, *, paired_cast, batch, seq):
    n = batch * seq
    rows = SUBS * Q_ROWS
    steps = seq // rows
    n_steps = batch * steps
    cast_blocks = []
    for w in cast_weights:
        assert w.ndim == 2 and w.shape[0] % (n_steps * CAST_ROWS) == 0, w.shape
        cast_blocks.append(pl.BlockSpec((w.shape[0] // n_steps, w.shape[1]),
                                        lambda b, p: (b * steps + p, 0)))
    outs = pl.pallas_call(
        functools.partial(_attn_body, n_cast=len(cast_weights), paired_cast=paired_cast),
        grid=(batch, steps),
        in_specs=[
            pl.BlockSpec((N_HEADS, BIAS_EXT), lambda b, p: (0, 0)),
            pl.BlockSpec((rows, ATTN_WIDTH), lambda b, p: (b * steps + p, 0)),
            pl.BlockSpec((seq, ATTN_WIDTH), lambda b, p: (b, 1)),
            pl.BlockSpec((seq, ATTN_WIDTH), lambda b, p: (b, 2)),
        ] + cast_blocks,
        out_specs=[pl.BlockSpec((rows, ATTN_WIDTH), lambda b, p: (b * steps + p, 0))]
        + cast_blocks,
        out_shape=[jax.ShapeDtypeStruct((n, ATTN_WIDTH), BF16)]
        + [jax.ShapeDtypeStruct(w.shape, BF16) for w in cast_weights],
        scratch_shapes=[
            pltpu.VMEM((PAD_ROWS + seq, ATTN_WIDTH), BF16),
            pltpu.VMEM((PAD_BLOCKS + seq // Q_ROWS, N_PAIRS, PAIR + ONES_ROWS, Q_ROWS), BF16),
            pltpu.VMEM((N_PAIRS, BIAS_ROWS, 2 * Q_ROWS), F32),
            pltpu.VMEM((2, BAND, 2 * Q_ROWS), F32),
            pltpu.VMEM((2, BAND, 2 * Q_ROWS), BF16),
        ],
        compiler_params=_compiler_params(("arbitrary", "arbitrary")),
        name="attn",
    )(bias_base, qkv, qkv, qkv, *cast_weights)
    return outs[0], outs[1:]


def _bias_base(rel_bias):
    far = rel_bias[:, -1:]
    head = jnp.broadcast_to(far, (N_HEADS, Q_ROWS))
    tail = jnp.broadcast_to(far, (N_HEADS, BIAS_EXT - Q_ROWS - 2 * MAX_REL - 1))
    return jnp.concatenate([head, rel_bias, tail], axis=1)


def _mix_body(ya_ref, u_ref, gates_ref, x_ref, pw_ref, ps_ref,
              pa_ref, pb_ref, wo_ref, o_ref, ext_ref, yb_ref, a_ref, *, tiles_per_seq):
    i = pl.program_id(0)
    tm, d = x_ref.shape
    tile_in_seq = i % tiles_per_seq
    n_groups = len(POOL_WINDOWS)
    a_cols = d // n_groups

    @pl.when(tile_in_seq == 0)
    def _():
        ext_ref[0:POOL_HALO, :] = jnp.zeros((POOL_HALO, POOL_WIDTH), F32)

    @pl.when(tile_in_seq != 0)
    def _():
        ext_ref[0:POOL_HALO, :] = ext_ref[tm:, :]

    ext_ref[POOL_HALO:, :] = u_ref[...]

    pos = tile_in_seq * tm + lax.broadcasted_iota(jnp.int32, (tm, 1), 0)
    for g, w in enumerate(POOL_WINDOWS):
        acols = slice(g * a_cols, (g + 1) * a_cols)
        a_ref[:, acols] = jnp.dot(ya_ref[...], pa_ref[:, acols],
                                  preferred_element_type=F32)
        cols = slice(g * POOL_GROUP, (g + 1) * POOL_GROUP)
        acc = ext_ref[:, cols]
        shift = 1
        while shift < w:
            acc = acc + pltpu.roll(acc, shift, 0)
            shift *= 2
        acc = acc[POOL_HALO:]
        cur = ext_ref[POOL_HALO:, cols]
        cnt = jnp.minimum(pos + 1, w).astype(F32)
        dev = (acc / cnt - cur).astype(BF16)
        y = jnp.dot(dev, pw_ref[g], preferred_element_type=F32)
        yb_ref[:, cols] = (y * ps_ref[:, cols]).astype(BF16)

    bb = jnp.dot(yb_ref[...], pb_ref[...], preferred_element_type=F32)
    a = a_ref[...]
    tanh_a = jnp.tanh(gates_ref[:, :d].astype(F32))
    tanh_b = jnp.tanh(gates_ref[:, d:].astype(F32))
    merged = 0.5 * ((a + bb) + (tanh_a * a + tanh_b * bb))
    y = jnp.dot(merged.astype(BF16), wo_ref[...], preferred_element_type=F32)
    o_ref[...] = x_ref[...] + y


def _mix(ya, u, gates, x2d, pool_w, pool_scale, w_a, w_b, w_o, *, seq, tm):
    n, d = x2d.shape
    tiles_per_seq = seq // tm
    const = lambda *shape: pl.BlockSpec(shape, lambda i: (0,) * len(shape),
                                        pipeline_mode=pl.Buffered(1))
    return pl.pallas_call(
        functools.partial(_mix_body, tiles_per_seq=tiles_per_seq),
        grid=(n // tm,),
        in_specs=[
            pl.BlockSpec((tm, ATTN_WIDTH), lambda i: (i, 0)),
            pl.BlockSpec((tm, POOL_WIDTH), lambda i: (i, 0)),
            pl.BlockSpec((tm, 2 * d), lambda i: (i, 0)),
            pl.BlockSpec((tm, d), lambda i: (i, 0)),
            const(len(POOL_WINDOWS), POOL_GROUP, POOL_GROUP),
            const(1, POOL_WIDTH),
            const(ATTN_WIDTH, d),
            const(POOL_WIDTH, d),
            const(d, d),
        ],
        out_specs=pl.BlockSpec((tm, d), lambda i: (i, 0)),
        out_shape=jax.ShapeDtypeStruct((n, d), F32),
        scratch_shapes=[
            pltpu.VMEM((POOL_HALO + tm, POOL_WIDTH), F32),
            pltpu.VMEM((tm, POOL_WIDTH), BF16),
            pltpu.VMEM((tm, d), F32),
        ],
        compiler_params=_compiler_params(("arbitrary",)),
        name="mix",
    )(ya, u, gates, x2d, pool_w, pool_scale, w_a, w_b, w_o)


def _ffn_body(x_ref, g_ref, wgu_ref, wd_ref, gf_ref, o_ref, h_ref):
    j = pl.program_id(1)

    def slab(first, last=False):
        if first:
            x = x_ref[...]
            h = _rms(x, g_ref[...]).astype(BF16)
            h_ref[...] = h
        else:
            h = h_ref[...]
        tf = wd_ref.shape[0]
        half_gate = jnp.dot(h, wgu_ref[:, :tf], preferred_element_type=F32)
        up = jnp.dot(h, wgu_ref[:, tf:], preferred_element_type=F32)
        act = ((half_gate * up) * (1.0 + jnp.tanh(half_gate))).astype(BF16)
        down = jnp.dot(act, wd_ref[...], preferred_element_type=F32)
        total = (x if first else o_ref[...]) + down
        o_ref[...] = _rms(total, gf_ref[...]) if last else total

    last_j = pl.num_programs(1) - 1

    @pl.when(j == 0)
    def _():
        slab(True)

    @pl.when((j > 0) & (j < last_j))
    def _():
        slab(False)

    @pl.when(j == last_j)
    def _():
        slab(False, last=True)


def _ffn(x2d, g, w_gate_up, w_down, g_final, *, tm, tf):
    n, d = x2d.shape
    d_ff = w_down.shape[0]
    n_f = d_ff // tf
    return pl.pallas_call(
        _ffn_body,
        grid=(n // tm, n_f),
        in_specs=[
            pl.BlockSpec((tm, d), lambda i, j: (i, 0)),
            pl.BlockSpec((1, d), lambda i, j: (0, 0)),
            pl.BlockSpec((d, 2 * tf), lambda i, j: (0, j)),
            pl.BlockSpec((tf, d), lambda i, j: (j, 0)),
            pl.BlockSpec((1, d), lambda i, j: (0, 0)),
        ],
        out_specs=pl.BlockSpec((tm, d), lambda i, j: (i, 0)),
        out_shape=jax.ShapeDtypeStruct((n, d), F32),
        scratch_shapes=[pltpu.VMEM((tm, d), BF16)],
        compiler_params=_compiler_params(("arbitrary", "arbitrary")),
        name="ffn",
    )(x2d, g, w_gate_up, w_down, g_final)


def _layer(x2d, norm_mix, w_in, rel_bias, pool_w, pool_scale, w_a, w_b, w_o,
           norm_ffn, w_gate_up, w_down, norm_out, *, batch, seq):
    d = x2d.shape[1]
    qkv, u, gates = _in_proj(x2d, norm_mix.reshape(1, d), w_in, tm=IN_PROJ_ROWS)
    pool_w2d = pool_w.reshape(-1, pool_w.shape[-1])
    ya, (pool_w2d, w_a, w_b, w_o, w_gate_up, w_down) = _attention(
        qkv, _bias_base(rel_bias), (pool_w2d, w_a, w_b, w_o, w_gate_up, w_down),
        paired_cast=4, batch=batch, seq=seq)
    x1 = _mix(ya, u, gates, x2d, pool_w2d.reshape(pool_w.shape), pool_scale.reshape(1, -1),
              w_a, w_b, w_o, seq=seq, tm=MIX_ROWS)
    return _ffn(x1, norm_ffn.reshape(1, d), w_gate_up, w_down,
                norm_out.reshape(1, d), tm=FFN_ROWS, tf=FFN_COLS)


def kernel(x, norm_mix, w_in, rel_bias, pool_w, pool_scale, w_branch_a, w_branch_b,
           w_out, norm_ffn, w_gate_up, w_down, norm_final):
    batch, seq, d = x.shape
    depth = w_in.shape[0]
    assert depth == 1, "the fused ffn kernel applies the final norm to the only layer"
    out = _layer(x.reshape(batch * seq, d), norm_mix[0], w_in[0], rel_bias[0],
                 pool_w[0], pool_scale[0], w_branch_a[0], w_branch_b[0], w_out[0],
                 norm_ffn[0], w_gate_up[0], w_down[0], norm_final,
                 batch=batch, seq=seq)
    return out.reshape(batch, seq, d)
```

```python
import functools
import math

import jax
import jax.numpy as jnp
from jax import lax
from jax.experimental import pallas as pl
from jax.experimental.pallas import tpu as pltpu

F32 = jnp.float32
BF16 = jnp.bfloat16

CHUNK = 64
LEFT_CHUNKS = 8
N_HEADS = 16
HEAD_DIM = 64
ATTN_WIDTH = N_HEADS * HEAD_DIM
MAX_REL = 128
POOL_WINDOWS = (2, 4, 8, 16)
POOL_GROUP = 256
POOL_WIDTH = POOL_GROUP * len(POOL_WINDOWS)
EPS = 1e-6
NEG_INF = -1e30
LOG2_E = math.log2(math.e)

Q_ROWS = 2 * CHUNK
BAND = (LEFT_CHUNKS + 2) * CHUNK
PAD_ROWS = LEFT_CHUNKS * CHUNK
BIAS_EXT = BAND + Q_ROWS
POOL_HALO = 16

V7X_VMEM_BYTES = 64 * 1024 * 1024
VMEM_LIMIT_BYTES = V7X_VMEM_BYTES - 4 * 1024 * 1024
IN_PROJ_ROWS = 256
STAGE_ROWS = 128
MIX_ROWS = 256
FFN_ROWS = 1024
FFN_COLS = 512


def _rms(x, g):
    ms = jnp.mean(x * x, axis=-1, keepdims=True)
    return x * lax.rsqrt(ms + EPS) * g


def _sigmoid(x):
    return 0.5 * jnp.tanh(0.5 * x) + 0.5


def _compiler_params(semantics):
    return pltpu.CompilerParams(
        dimension_semantics=semantics, vmem_limit_bytes=VMEM_LIMIT_BYTES)


def _in_proj_body(x_ref, g_ref, w_hbm, qkv_ref, u_ref, gate_ref, w_ref, stage_ref, sem):
    n_chunks = w_ref.shape[0] // STAGE_ROWS

    def chunk_copy(c, slot):
        rows = pl.ds(pl.multiple_of(c * STAGE_ROWS, STAGE_ROWS), STAGE_ROWS)
        return pltpu.make_async_copy(w_hbm.at[rows], stage_ref.at[slot], sem.at[slot])

    @pl.when(pl.program_id(0) == 0)
    def _():
        chunk_copy(0, 0).start()

        def two_chunks(cc, carry):
            for slot in range(2):
                c = 2 * cc + slot

                @pl.when(c + 1 < n_chunks)
                def _():
                    chunk_copy(c + 1, 1 - slot).start()

                chunk_copy(c, slot).wait()
                rows = pl.ds(pl.multiple_of(c * STAGE_ROWS, STAGE_ROWS), STAGE_ROWS)
                w_ref[rows, :] = stage_ref[slot].astype(BF16)
            return carry

        lax.fori_loop(0, n_chunks // 2, two_chunks, 0)

    h = _rms(x_ref[...], g_ref[...]).astype(BF16)
    q_end = qkv_ref.shape[1]
    u_end = q_end + u_ref.shape[1]
    qkv_ref[...] = jnp.dot(h, w_ref[:, :q_end], preferred_element_type=F32).astype(BF16)
    u_ref[...] = jnp.dot(h, w_ref[:, q_end:u_end], preferred_element_type=F32)
    gate_ref[...] = jnp.dot(h, w_ref[:, u_end:], preferred_element_type=F32).astype(BF16)


def _in_proj(x2d, g, w, *, tm):
    n, d = x2d.shape
    in_width = w.shape[1]
    n_gate = in_width - 3 * ATTN_WIDTH - POOL_WIDTH
    assert d % (2 * STAGE_ROWS) == 0
    return pl.pallas_call(
        _in_proj_body,
        grid=(n // tm,),
        in_specs=[
            pl.BlockSpec((tm, d), lambda i: (i, 0)),
            pl.BlockSpec((1, d), lambda i: (0, 0)),
            pl.BlockSpec(memory_space=pl.ANY),
        ],
        out_specs=[
            pl.BlockSpec((tm, 3 * ATTN_WIDTH), lambda i: (i, 0)),
            pl.BlockSpec((tm, POOL_WIDTH), lambda i: (i, 0)),
            pl.BlockSpec((tm, n_gate), lambda i: (i, 0)),
        ],
        out_shape=[
            jax.ShapeDtypeStruct((n, 3 * ATTN_WIDTH), BF16),
            jax.ShapeDtypeStruct((n, POOL_WIDTH), F32),
            jax.ShapeDtypeStruct((n, n_gate), BF16),
        ],
        scratch_shapes=[
            pltpu.VMEM((d, in_width), BF16),
            pltpu.VMEM((2, STAGE_ROWS, in_width), F32),
            pltpu.SemaphoreType.DMA((2,)),
        ],
        compiler_params=_compiler_params(("arbitrary",)),
        name="in_proj",
    )(x2d, g, w)


PAIR = 2 * HEAD_DIM
N_PAIRS = N_HEADS // 2
WIN_BLOCKS = BAND // Q_ROWS
PAD_BLOCKS = PAD_ROWS // Q_ROWS
FLAT_END = PAD_ROWS - MAX_REL
BIAS_ROWS = BAND - FLAT_END
SEEN = BAND - CHUNK
ONES_ROWS = 16
SUBS = 4
CAST_ROWS = 16


def _attn_body(base_ref, q_ref, k_ref, v_ref, *rest, n_cast, paired_cast):
    cast_in = rest[:n_cast]
    o_ref = rest[n_cast]
    cast_out = rest[n_cast + 1:2 * n_cast + 1]
    kpad, vtpad, bias_ref, s_ref, p_ref = rest[2 * n_cast + 1:]
    b = pl.program_id(0)
    p = pl.program_id(1)

    @pl.when((b == 0) & (p == 0))
    def _():
        for hp in range(N_PAIRS):
            heads = []
            for a in range(2):
                h = 2 * hp + a
                base = jnp.broadcast_to(base_ref[h:h + 1, :], (BAND, BIAS_EXT))
                rot = pltpu.roll(base, 0, 1, stride=1, stride_axis=0)
                heads.append((rot[FLAT_END:, :Q_ROWS] - base[FLAT_END:, 0:1]) * LOG2_E)
            bias_ref[hp] = jnp.concatenate(
                [heads[a][:, c * CHUNK:(c + 1) * CHUNK] for c in range(2) for a in range(2)],
                axis=1)
        kpad[0:PAD_ROWS, :] = jnp.zeros((PAD_ROWS, ATTN_WIDTH), BF16)
        vtpad[0:PAD_BLOCKS, :, 0:PAIR, :] = jnp.zeros(
            (PAD_BLOCKS, N_PAIRS, PAIR, Q_ROWS), BF16)
        vtpad[:, :, PAIR:, :] = jnp.ones(
            (vtpad.shape[0], N_PAIRS, ONES_ROWS, Q_ROWS), BF16)
        p_ref[:, SEEN:, :Q_ROWS] = jnp.zeros((2, CHUNK, Q_ROWS), BF16)
        p_ref[:, :CHUNK, Q_ROWS:] = jnp.zeros((2, CHUNK, Q_ROWS), BF16)

    @pl.when(p == 0)
    def _():
        kpad[PAD_ROWS:, :] = k_ref[...]
        for c in range(v_ref.shape[0] // Q_ROWS):
            blk_t = v_ref[c * Q_ROWS:(c + 1) * Q_ROWS, :].astype(F32).T.astype(BF16)
            for hp in range(N_PAIRS):
                vtpad[PAD_BLOCKS + c, hp, 0:PAIR, :] = blk_t[hp * PAIR:(hp + 1) * PAIR, :]

    lane = lax.broadcasted_iota(jnp.int32, (CHUNK, PAIR), 1)
    first_head = lane < HEAD_DIM
    row = lax.broadcasted_iota(jnp.int32, (PAIR, Q_ROWS), 0)
    first_head_rows = row < HEAD_DIM
    first_chunk_lanes = lax.broadcasted_iota(jnp.int32, (PAIR, Q_ROWS), 1) < CHUNK
    scale = LOG2_E / math.sqrt(HEAD_DIM)

    def step(mask_missing_keys):
        units = [(sub, hp) for sub in range(SUBS) for hp in range(N_PAIRS)]

        def scores(u):
            sub, hp = units[u]
            blk = p * SUBS + sub
            lanes = slice(hp * PAIR, (hp + 1) * PAIR)
            q2 = q_ref[sub * Q_ROWS:(sub + 1) * Q_ROWS, lanes].astype(F32) * scale
            zero = jnp.zeros((CHUNK, PAIR), F32)
            qm = jnp.concatenate(
                [jnp.where(first_head if a == 0 else jnp.logical_not(first_head),
                           q2[c * CHUNK:(c + 1) * CHUNK], zero)
                 for c in range(2) for a in range(2)], axis=0)
            qm_t = qm.T.astype(BF16)
            k2 = kpad[pl.ds(pl.multiple_of(blk * Q_ROWS, Q_ROWS), BAND), lanes]
            s = jnp.dot(k2, qm_t, preferred_element_type=F32)
            pieces = [(0, FLAT_END, 0, None), (FLAT_END, SEEN, 0, (0, SEEN - FLAT_END)),
                      (CHUNK, FLAT_END, Q_ROWS, None), (FLAT_END, BAND, Q_ROWS, (0, BIAS_ROWS))]
            for r0, r1, l0, bias_rows in pieces:
                part = s[r0:r1, l0:l0 + Q_ROWS]
                if bias_rows is not None:
                    part = part + bias_ref[hp, bias_rows[0]:bias_rows[1], l0:l0 + Q_ROWS]
                if mask_missing_keys:
                    key = r0 + lax.broadcasted_iota(jnp.int32, part.shape, 0)
                    part = jnp.where(key >= PAD_ROWS - blk * Q_ROWS, part, NEG_INF)
                s_ref[u % 2, r0:r1, l0:l0 + Q_ROWS] = part

        def weights(u):
            slot = u % 2
            for r0, l0 in ((0, 0), (CHUNK, Q_ROWS)):
                s = s_ref[slot, r0:r0 + SEEN, l0:l0 + Q_ROWS]
                m = jnp.max(s, axis=0, keepdims=True)
                p_ref[slot, r0:r0 + SEEN, l0:l0 + Q_ROWS] = jnp.exp2(s - m).astype(BF16)

        def values(u):
            sub, hp = units[u]
            blk = p * SUBS + sub
            vt2 = jnp.concatenate(
                [vtpad[blk + t, hp] for t in range(WIN_BLOCKS)], axis=1)
            o_t = jnp.dot(vt2, p_ref[u % 2], preferred_element_type=F32)
            o_t = o_t[0:PAIR] / o_t[PAIR:PAIR + 1]
            t0, t1 = o_t[:, :Q_ROWS], o_t[:, Q_ROWS:]
            head_a = jnp.where(first_chunk_lanes, t0, pltpu.roll(t1, CHUNK, 1))
            head_b = jnp.where(first_chunk_lanes, pltpu.roll(t0, CHUNK, 1), t1)
            own = jnp.where(first_head_rows, head_a, head_b)
            o_ref[sub * Q_ROWS:(sub + 1) * Q_ROWS, hp * PAIR:(hp + 1) * PAIR] = (
                own.T.astype(BF16))

        casts = [(k, r0) for k, src in enumerate(cast_in)
                 for r0 in range(0, src.shape[0], CAST_ROWS)]
        casts_per_unit = pl.cdiv(len(casts), len(units))

        def cast_slab(k, r0):
            src, dst = cast_in[k], cast_out[k]
            rows = slice(r0, r0 + CAST_ROWS)
            if k != paired_cast:
                dst[rows, :] = src[rows, :].astype(BF16)
                return
            half = src.shape[1] // 2
            for j in range(half // FFN_COLS):
                for part in range(2):
                    c_src = part * half + j * FFN_COLS
                    c_dst = (2 * j + part) * FFN_COLS
                    dst[rows, c_dst:c_dst + FFN_COLS] = (
                        src[rows, c_src:c_src + FFN_COLS].astype(BF16))

        scores(0)
        for u in range(len(units) + 1):
            if u + 1 < len(units):
                scores(u + 1)
            if u >= 1:
                values(u - 1)
            if u < len(units):
                weights(u)
                for k, r0 in casts[u * casts_per_unit:(u + 1) * casts_per_unit]:
                    cast_slab(k, r0)

    @pl.when(p < PAD_BLOCKS // SUBS)
    def _():
        step(True)

    @pl.when(p >= PAD_BLOCKS // SUBS)
    def _():
        step(False)


def _attention(qkv, bias_base, cast_weights, *, paired_cast, batch, seq):
    n = batch * seq
    rows = SUBS * Q_ROWS
    steps = seq // rows
    n_steps = batch * steps
    cast_blocks = []
    for w in cast_weights:
        assert w.ndim == 2 and w.shape[0] % (n_steps * CAST_ROWS) == 0, w.shape
        cast_blocks.append(pl.BlockSpec((w.shape[0] // n_steps, w.shape[1]),
                                        lambda b, p: (b * steps + p, 0)))
    outs = pl.pallas_call(
        functools.partial(_attn_body, n_cast=len(cast_weights), paired_cast=paired_cast),
        grid=(batch, steps),
        in_specs=[
            pl.BlockSpec((N_HEADS, BIAS_EXT), lambda b, p: (0, 0)),
            pl.BlockSpec((rows, ATTN_WIDTH), lambda b, p: (b * steps + p, 0)),
            pl.BlockSpec((seq, ATTN_WIDTH), lambda b, p: (b, 1)),
            pl.BlockSpec((seq, ATTN_WIDTH), lambda b, p: (b, 2)),
        ] + cast_blocks,
        out_specs=[pl.BlockSpec((rows, ATTN_WIDTH), lambda b, p: (b * steps + p, 0))]
        + cast_blocks,
        out_shape=[jax.ShapeDtypeStruct((n, ATTN_WIDTH), BF16)]
        + [jax.ShapeDtypeStruct(w.shape, BF16) for w in cast_weights],
        scratch_shapes=[
            pltpu.VMEM((PAD_ROWS + seq, ATTN_WIDTH), BF16),
            pltpu.VMEM((PAD_BLOCKS + seq // Q_ROWS, N_PAIRS, PAIR + ONES_ROWS, Q_ROWS), BF16),
            pltpu.VMEM((N_PAIRS, BIAS_ROWS, 2 * Q_ROWS), F32),
            pltpu.VMEM((2, BAND, 2 * Q_ROWS), F32),
            pltpu.VMEM((2, BAND, 2 * Q_ROWS), BF16),
        ],
        compiler_params=_compiler_params(("arbitrary", "arbitrary")),
        name="attn",
    )(bias_base, qkv, qkv, qkv, *cast_weights)
    return outs[0], outs[1:]


def _bias_base(rel_bias):
    far = rel_bias[:, -1:]
    head = jnp.broadcast_to(far, (N_HEADS, Q_ROWS))
    tail = jnp.broadcast_to(far, (N_HEADS, BIAS_EXT - Q_ROWS - 2 * MAX_REL - 1))
    return jnp.concatenate([head, rel_bias, tail], axis=1)


def _mix_body(ya_ref, u_ref, gates_ref, x_ref, pw_ref, ps_ref,
              pa_ref, pb_ref, wo_ref, o_ref, ext_ref, yb_ref, a_ref, *, tiles_per_seq):
    i = pl.program_id(0)
    tm, d = x_ref.shape
    tile_in_seq = i % tiles_per_seq
    n_groups = len(POOL_WINDOWS)
    a_cols = d // n_groups

    @pl.when(tile_in_seq == 0)
    def _():
        ext_ref[0:POOL_HALO, :] = jnp.zeros((POOL_HALO, POOL_WIDTH), F32)

    @pl.when(tile_in_seq != 0)
    def _():
        ext_ref[0:POOL_HALO, :] = ext_ref[tm:, :]

    ext_ref[POOL_HALO:, :] = u_ref[...]

    pos = tile_in_seq * tm + lax.broadcasted_iota(jnp.int32, (tm, 1), 0)
    for g, w in enumerate(POOL_WINDOWS):
        acols = slice(g * a_cols, (g + 1) * a_cols)
        a_ref[:, acols] = jnp.dot(ya_ref[...], pa_ref[:, acols],
                                  preferred_element_type=F32)
        cols = slice(g * POOL_GROUP, (g + 1) * POOL_GROUP)
        acc = ext_ref[:, cols]
        shift = 1
        while shift < w:
            acc = acc + pltpu.roll(acc, shift, 0)
            shift *= 2
        acc = acc[POOL_HALO:]
        cur = ext_ref[POOL_HALO:, cols]
        cnt = jnp.minimum(pos + 1, w).astype(F32)
        dev = (acc / cnt - cur).astype(BF16)
        y = jnp.dot(dev, pw_ref[g], preferred_element_type=F32)
        yb_ref[:, cols] = (y * ps_ref[:, cols]).astype(BF16)

    bb = jnp.dot(yb_ref[...], pb_ref[...], preferred_element_type=F32)
    gate_a = _sigmoid(gates_ref[:, :d].astype(F32))
    gate_b = _sigmoid(gates_ref[:, d:].astype(F32))
    merged = gate_a * a_ref[...] + gate_b * bb
    y = jnp.dot(merged.astype(BF16), wo_ref[...], preferred_element_type=F32)
    o_ref[...] = x_ref[...] + y


def _mix(ya, u, gates, x2d, pool_w, pool_scale, w_a, w_b, w_o, *, seq, tm):
    n, d = x2d.shape
    tiles_per_seq = seq // tm
    const = lambda *shape: pl.BlockSpec(shape, lambda i: (0,) * len(shape),
                                        pipeline_mode=pl.Buffered(1))
    return pl.pallas_call(
        functools.partial(_mix_body, tiles_per_seq=tiles_per_seq),
        grid=(n // tm,),
        in_specs=[
            pl.BlockSpec((tm, ATTN_WIDTH), lambda i: (i, 0)),
            pl.BlockSpec((tm, POOL_WIDTH), lambda i: (i, 0)),
            pl.BlockSpec((tm, 2 * d), lambda i: (i, 0)),
            pl.BlockSpec((tm, d), lambda i: (i, 0)),
            const(len(POOL_WINDOWS), POOL_GROUP, POOL_GROUP),
            const(1, POOL_WIDTH),
            const(ATTN_WIDTH, d),
            const(POOL_WIDTH, d),
            const(d, d),
        ],
        out_specs=pl.BlockSpec((tm, d), lambda i: (i, 0)),
        out_shape=jax.ShapeDtypeStruct((n, d), F32),
        scratch_shapes=[
            pltpu.VMEM((POOL_HALO + tm, POOL_WIDTH), F32),
            pltpu.VMEM((tm, POOL_WIDTH), BF16),
            pltpu.VMEM((tm, d), F32),
        ],
        compiler_params=_compiler_params(("arbitrary",)),
        name="mix",
    )(ya, u, gates, x2d, pool_w, pool_scale, w_a, w_b, w_o)


def _ffn_body(x_ref, g_ref, wgu_ref, wd_ref, gf_ref, o_ref, h_ref):
    j = pl.program_id(1)

    def slab(first, last=False):
        if first:
            x = x_ref[...]
            h = _rms(x, g_ref[...]).astype(BF16)
            h_ref[...] = h
        else:
            h = h_ref[...]
        tf = wd_ref.shape[0]
        gate = jnp.dot(h, wgu_ref[:, :tf], preferred_element_type=F32)
        up = jnp.dot(h, wgu_ref[:, tf:], preferred_element_type=F32)
        act = (gate * _sigmoid(gate) * up).astype(BF16)
        down = jnp.dot(act, wd_ref[...], preferred_element_type=F32)
        total = (x if first else o_ref[...]) + down
        o_ref[...] = _rms(total, gf_ref[...]) if last else total

    last_j = pl.num_programs(1) - 1

    @pl.when(j == 0)
    def _():
        slab(True)

    @pl.when((j > 0) & (j < last_j))
    def _():
        slab(False)

    @pl.when(j == last_j)
    def _():
        slab(False, last=True)


def _ffn(x2d, g, w_gate_up, w_down, g_final, *, tm, tf):
    n, d = x2d.shape
    d_ff = w_down.shape[0]
    n_f = d_ff // tf
    return pl.pallas_call(
        _ffn_body,
        grid=(n // tm, n_f),
        in_specs=[
            pl.BlockSpec((tm, d), lambda i, j: (i, 0)),
            pl.BlockSpec((1, d), lambda i, j: (0, 0)),
            pl.BlockSpec((d, 2 * tf), lambda i, j: (0, j)),
            pl.BlockSpec((tf, d), lambda i, j: (j, 0)),
            pl.BlockSpec((1, d), lambda i, j: (0, 0)),
        ],
        out_specs=pl.BlockSpec((tm, d), lambda i, j: (i, 0)),
        out_shape=jax.ShapeDtypeStruct((n, d), F32),
        scratch_shapes=[pltpu.VMEM((tm, d), BF16)],
        compiler_params=_compiler_params(("arbitrary", "arbitrary")),
        name="ffn",
    )(x2d, g, w_gate_up, w_down, g_final)


def _layer(x2d, norm_mix, w_in, rel_bias, pool_w, pool_scale, w_a, w_b, w_o,
           norm_ffn, w_gate_up, w_down, norm_out, *, batch, seq):
    d = x2d.shape[1]
    qkv, u, gates = _in_proj(x2d, norm_mix.reshape(1, d), w_in, tm=IN_PROJ_ROWS)
    pool_w2d = pool_w.reshape(-1, pool_w.shape[-1])
    ya, (pool_w2d, w_a, w_b, w_o, w_gate_up, w_down) = _attention(
        qkv, _bias_base(rel_bias), (pool_w2d, w_a, w_b, w_o, w_gate_up, w_down),
        paired_cast=4, batch=batch, seq=seq)
    x1 = _mix(ya, u, gates, x2d, pool_w2d.reshape(pool_w.shape), pool_scale.reshape(1, -1),
              w_a, w_b, w_o, seq=seq, tm=MIX_ROWS)
    return _ffn(x1, norm_ffn.reshape(1, d), w_gate_up, w_down,
                norm_out.reshape(1, d), tm=FFN_ROWS, tf=FFN_COLS)


def kernel(x, norm_mix, w_in, rel_bias, pool_w, pool_scale, w_branch_a, w_branch_b,
           w_out, norm_ffn, w_gate_up, w_down, norm_final):
    batch, seq, d = x.shape
    depth = w_in.shape[0]
    assert depth == 1, "the fused ffn kernel applies the final norm to the only layer"
    out = _layer(x.reshape(batch * seq, d), norm_mix[0], w_in[0], rel_bias[0],
                 pool_w[0], pool_scale[0], w_branch_a[0], w_branch_b[0], w_out[0],
                 norm_ffn[0], w_gate_up[0], w_down[0], norm_final,
                 batch=batch, seq=seq)
    return out.reshape(batch, seq, d)
```

```python
import functools
import math

import jax
import jax.numpy as jnp
from jax import lax
from jax.experimental import pallas as pl
from jax.experimental.pallas import tpu as pltpu

F32 = jnp.float32
BF16 = jnp.bfloat16

CHUNK = 64
LEFT_CHUNKS = 8
N_HEADS = 16
HEAD_DIM = 64
ATTN_WIDTH = N_HEADS * HEAD_DIM
MAX_REL = 128
POOL_WINDOWS = (2, 4, 8, 16)
POOL_GROUP = 256
POOL_WIDTH = POOL_GROUP * len(POOL_WINDOWS)
EPS = 1e-6
NEG_INF = -1e30
LOG2_E = math.log2(math.e)

Q_ROWS = 2 * CHUNK
BAND = (LEFT_CHUNKS + 2) * CHUNK
PAD_ROWS = LEFT_CHUNKS * CHUNK
BIAS_EXT = BAND + Q_ROWS
POOL_HALO = 16

V7X_VMEM_BYTES = 64 * 1024 * 1024
VMEM_LIMIT_BYTES = V7X_VMEM_BYTES - 4 * 1024 * 1024
IN_PROJ_ROWS = 256
STAGE_ROWS = 128
MIX_ROWS = 256
FFN_ROWS = 1024
FFN_COLS = 512


def _rms(x, g):
    ms = jnp.mean(x * x, axis=-1, keepdims=True)
    return x * lax.rsqrt(ms + EPS) * g


def _sigmoid(x):
    return 0.5 * jnp.tanh(0.5 * x) + 0.5


def _compiler_params(semantics):
    return pltpu.CompilerParams(
        dimension_semantics=semantics, vmem_limit_bytes=VMEM_LIMIT_BYTES)


def _in_proj_body(x_ref, g_ref, w_hbm, qkv_ref, u_ref, gate_ref, w_ref, stage_ref, sem):
    n_chunks = w_ref.shape[0] // STAGE_ROWS

    def chunk_copy(c, slot):
        rows = pl.ds(pl.multiple_of(c * STAGE_ROWS, STAGE_ROWS), STAGE_ROWS)
        return pltpu.make_async_copy(w_hbm.at[rows], stage_ref.at[slot], sem.at[slot])

    @pl.when(pl.program_id(0) == 0)
    def _():
        chunk_copy(0, 0).start()

        def two_chunks(cc, carry):
            for slot in range(2):
                c = 2 * cc + slot

                @pl.when(c + 1 < n_chunks)
                def _():
                    chunk_copy(c + 1, 1 - slot).start()

                chunk_copy(c, slot).wait()
                rows = pl.ds(pl.multiple_of(c * STAGE_ROWS, STAGE_ROWS), STAGE_ROWS)
                gate0 = w_ref.shape[1] - gate_ref.shape[1]
                w_ref[rows, :gate0] = stage_ref[slot, :, :gate0].astype(BF16)
                w_ref[rows, gate0:] = (0.5 * stage_ref[slot, :, gate0:]).astype(BF16)
            return carry

        lax.fori_loop(0, n_chunks // 2, two_chunks, 0)

    h = _rms(x_ref[...], g_ref[...]).astype(BF16)
    q_end = qkv_ref.shape[1]
    u_end = q_end + u_ref.shape[1]
    qkv_ref[...] = jnp.dot(h, w_ref[:, :q_end], preferred_element_type=F32).astype(BF16)
    u_ref[...] = jnp.dot(h, w_ref[:, q_end:u_end], preferred_element_type=F32)
    gate_ref[...] = jnp.dot(h, w_ref[:, u_end:], preferred_element_type=F32).astype(BF16)


def _in_proj(x2d, g, w, *, tm):
    n, d = x2d.shape
    in_width = w.shape[1]
    n_gate = in_width - 3 * ATTN_WIDTH - POOL_WIDTH
    assert d % (2 * STAGE_ROWS) == 0
    return pl.pallas_call(
        _in_proj_body,
        grid=(n // tm,),
        in_specs=[
            pl.BlockSpec((tm, d), lambda i: (i, 0)),
            pl.BlockSpec((1, d), lambda i: (0, 0)),
            pl.BlockSpec(memory_space=pl.ANY),
        ],
        out_specs=[
            pl.BlockSpec((tm, 3 * ATTN_WIDTH), lambda i: (i, 0)),
            pl.BlockSpec((tm, POOL_WIDTH), lambda i: (i, 0)),
            pl.BlockSpec((tm, n_gate), lambda i: (i, 0)),
        ],
        out_shape=[
            jax.ShapeDtypeStruct((n, 3 * ATTN_WIDTH), BF16),
            jax.ShapeDtypeStruct((n, POOL_WIDTH), F32),
            jax.ShapeDtypeStruct((n, n_gate), BF16),
        ],
        scratch_shapes=[
            pltpu.VMEM((d, in_width), BF16),
            pltpu.VMEM((2, STAGE_ROWS, in_width), F32),
            pltpu.SemaphoreType.DMA((2,)),
        ],
        compiler_params=_compiler_params(("arbitrary",)),
        name="in_proj",
    )(x2d, g, w)


PAIR = 2 * HEAD_DIM
N_PAIRS = N_HEADS // 2
WIN_BLOCKS = BAND // Q_ROWS
PAD_BLOCKS = PAD_ROWS // Q_ROWS
FLAT_END = PAD_ROWS - MAX_REL
BIAS_ROWS = BAND - FLAT_END
SEEN = BAND - CHUNK
ONES_ROWS = 16
SUBS = 4
CAST_ROWS = 16


def _attn_body(base_ref, q_ref, k_ref, v_ref, *rest, n_cast, paired_cast):
    cast_in = rest[:n_cast]
    o_ref = rest[n_cast]
    cast_out = rest[n_cast + 1:2 * n_cast + 1]
    kpad, vtpad, bias_ref, s_ref, p_ref = rest[2 * n_cast + 1:]
    b = pl.program_id(0)
    p = pl.program_id(1)

    @pl.when((b == 0) & (p == 0))
    def _():
        for hp in range(N_PAIRS):
            heads = []
            for a in range(2):
                h = 2 * hp + a
                base = jnp.broadcast_to(base_ref[h:h + 1, :], (BAND, BIAS_EXT))
                rot = pltpu.roll(base, 0, 1, stride=1, stride_axis=0)
                heads.append((rot[FLAT_END:, :Q_ROWS] - base[FLAT_END:, 0:1]) * LOG2_E)
            bias_ref[hp] = jnp.concatenate(
                [heads[a][:, c * CHUNK:(c + 1) * CHUNK] for c in range(2) for a in range(2)],
                axis=1)
        kpad[0:PAD_ROWS, :] = jnp.zeros((PAD_ROWS, ATTN_WIDTH), BF16)
        vtpad[0:PAD_BLOCKS, :, 0:PAIR, :] = jnp.zeros(
            (PAD_BLOCKS, N_PAIRS, PAIR, Q_ROWS), BF16)
        vtpad[:, :, PAIR:, :] = jnp.ones(
            (vtpad.shape[0], N_PAIRS, ONES_ROWS, Q_ROWS), BF16)
        p_ref[:, SEEN:, :Q_ROWS] = jnp.zeros((2, CHUNK, Q_ROWS), BF16)
        p_ref[:, :CHUNK, Q_ROWS:] = jnp.zeros((2, CHUNK, Q_ROWS), BF16)

    @pl.when(p == 0)
    def _():
        kpad[PAD_ROWS:, :] = k_ref[...]
        for c in range(v_ref.shape[0] // Q_ROWS):
            blk_t = v_ref[c * Q_ROWS:(c + 1) * Q_ROWS, :].astype(F32).T.astype(BF16)
            for hp in range(N_PAIRS):
                vtpad[PAD_BLOCKS + c, hp, 0:PAIR, :] = blk_t[hp * PAIR:(hp + 1) * PAIR, :]

    lane = lax.broadcasted_iota(jnp.int32, (CHUNK, PAIR), 1)
    first_head = lane < HEAD_DIM
    row = lax.broadcasted_iota(jnp.int32, (PAIR, Q_ROWS), 0)
    first_head_rows = row < HEAD_DIM
    first_chunk_lanes = lax.broadcasted_iota(jnp.int32, (PAIR, Q_ROWS), 1) < CHUNK
    scale = LOG2_E / math.sqrt(HEAD_DIM)

    def step(mask_missing_keys):
        units = [(sub, hp) for sub in range(SUBS) for hp in range(N_PAIRS)]

        def scores(u):
            sub, hp = units[u]
            blk = p * SUBS + sub
            lanes = slice(hp * PAIR, (hp + 1) * PAIR)
            q2 = q_ref[sub * Q_ROWS:(sub + 1) * Q_ROWS, lanes].astype(F32) * scale
            zero = jnp.zeros((CHUNK, PAIR), F32)
            qm = jnp.concatenate(
                [jnp.where(first_head if a == 0 else jnp.logical_not(first_head),
                           q2[c * CHUNK:(c + 1) * CHUNK], zero)
                 for c in range(2) for a in range(2)], axis=0)
            qm_t = qm.T.astype(BF16)
            k2 = kpad[pl.ds(pl.multiple_of(blk * Q_ROWS, Q_ROWS), BAND), lanes]
            s = jnp.dot(k2, qm_t, preferred_element_type=F32)
            pieces = [(0, FLAT_END, 0, None), (FLAT_END, SEEN, 0, (0, SEEN - FLAT_END)),
                      (CHUNK, FLAT_END, Q_ROWS, None), (FLAT_END, BAND, Q_ROWS, (0, BIAS_ROWS))]
            for r0, r1, l0, bias_rows in pieces:
                part = s[r0:r1, l0:l0 + Q_ROWS]
                if bias_rows is not None:
                    part = part + bias_ref[hp, bias_rows[0]:bias_rows[1], l0:l0 + Q_ROWS]
                if mask_missing_keys:
                    key = r0 + lax.broadcasted_iota(jnp.int32, part.shape, 0)
                    part = jnp.where(key >= PAD_ROWS - blk * Q_ROWS, part, NEG_INF)
                s_ref[u % 2, r0:r1, l0:l0 + Q_ROWS] = part

        def weights(u):
            slot = u % 2
            for r0, l0 in ((0, 0), (CHUNK, Q_ROWS)):
                s = s_ref[slot, r0:r0 + SEEN, l0:l0 + Q_ROWS]
                m = jnp.max(s, axis=0, keepdims=True)
                p_ref[slot, r0:r0 + SEEN, l0:l0 + Q_ROWS] = jnp.exp2(s - m).astype(BF16)

        def values(u):
            sub, hp = units[u]
            blk = p * SUBS + sub
            vt2 = jnp.concatenate(
                [vtpad[blk + t, hp] for t in range(WIN_BLOCKS)], axis=1)
            o_t = jnp.dot(vt2, p_ref[u % 2], preferred_element_type=F32)
            o_t = o_t[0:PAIR] / o_t[PAIR:PAIR + 1]
            t0, t1 = o_t[:, :Q_ROWS], o_t[:, Q_ROWS:]
            head_a = jnp.where(first_chunk_lanes, t0, pltpu.roll(t1, CHUNK, 1))
            head_b = jnp.where(first_chunk_lanes, pltpu.roll(t0, CHUNK, 1), t1)
            own = jnp.where(first_head_rows, head_a, head_b)
            o_ref[sub * Q_ROWS:(sub + 1) * Q_ROWS, hp * PAIR:(hp + 1) * PAIR] = (
                own.T.astype(BF16))

        casts = [(k, r0) for k, src in enumerate(cast_in)
                 for r0 in range(0, src.shape[0], CAST_ROWS)]
        casts_per_unit = pl.cdiv(len(casts), len(units))

        def cast_slab(k, r0):
            src, dst = cast_in[k], cast_out[k]
            rows = slice(r0, r0 + CAST_ROWS)
            if k != paired_cast:
                dst[rows, :] = src[rows, :].astype(BF16)
                return
            half = src.shape[1] // 2
            for j in range(half // FFN_COLS):
                for part in range(2):
                    c_src = part * half + j * FFN_COLS
                    c_dst = (2 * j + part) * FFN_COLS
                    dst[rows, c_dst:c_dst + FFN_COLS] = (
                        src[rows, c_src:c_src + FFN_COLS].astype(BF16))

        scores(0)
        for u in range(len(units) + 1):
            if u + 1 < len(units):
                scores(u + 1)
            if u >= 1:
                values(u - 1)
            if u < len(units):
                weights(u)
                for k, r0 in casts[u * casts_per_unit:(u + 1) * casts_per_unit]:
                    cast_slab(k, r0)

    @pl.when(p < PAD_BLOCKS // SUBS)
    def _():
        step(True)

    @pl.when(p >= PAD_BLOCKS // SUBS)
    def _():
        step(False)


def _attention(qkv, bias_base, cast_weights, *, paired_cast, batch, seq):
    n = batch * seq
    rows = SUBS * Q_ROWS
    steps = seq // rows
    n_steps = batch * steps
    cast_blocks = []
    for w in cast_weights:
        assert w.ndim == 2 and w.shape[0] % (n_steps * CAST_ROWS) == 0, w.shape
        cast_blocks.append(pl.BlockSpec((w.shape[0] // n_steps, w.shape[1]),
                                        lambda b, p: (b * steps + p, 0)))
    outs = pl.pallas_call(
        functools.partial(_attn_body, n_cast=len(cast_weights), paired_cast=paired_cast),
        grid=(batch, steps),
        in_specs=[
            pl.BlockSpec((N_HEADS, BIAS_EXT), lambda b, p: (0, 0)),
            pl.BlockSpec((rows, ATTN_WIDTH), lambda b, p: (b * steps + p, 0)),
            pl.BlockSpec((seq, ATTN_WIDTH), lambda b, p: (b, 1)),
            pl.BlockSpec((seq, ATTN_WIDTH), lambda b, p: (b, 2)),
        ] + cast_blocks,
        out_specs=[pl.BlockSpec((rows, ATTN_WIDTH), lambda b, p: (b * steps + p, 0))]
        + cast_blocks,
        out_shape=[jax.ShapeDtypeStruct((n, ATTN_WIDTH), BF16)]
        + [jax.ShapeDtypeStruct(w.shape, BF16) for w in cast_weights],
        scratch_shapes=[
            pltpu.VMEM((PAD_ROWS + seq, ATTN_WIDTH), BF16),
            pltpu.VMEM((PAD_BLOCKS + seq // Q_ROWS, N_PAIRS, PAIR + ONES_ROWS, Q_ROWS), BF16),
            pltpu.VMEM((N_PAIRS, BIAS_ROWS, 2 * Q_ROWS), F32),
            pltpu.VMEM((2, BAND, 2 * Q_ROWS), F32),
            pltpu.VMEM((2, BAND, 2 * Q_ROWS), BF16),
        ],
        compiler_params=_compiler_params(("arbitrary", "arbitrary")),
        name="attn",
    )(bias_base, qkv, qkv, qkv, *cast_weights)
    return outs[0], outs[1:]


def _bias_base(rel_bias):
    far = rel_bias[:, -1:]
    head = jnp.broadcast_to(far, (N_HEADS, Q_ROWS))
    tail = jnp.broadcast_to(far, (N_HEADS, BIAS_EXT - Q_ROWS - 2 * MAX_REL - 1))
    return jnp.concatenate([head, rel_bias, tail], axis=1)


def _mix_body(ya_ref, u_ref, gates_ref, x_ref, pw_ref, ps_ref,
              pa_ref, pb_ref, wo_ref, o_ref, ext_ref, yb_ref, a_ref, *, tiles_per_seq):
    i = pl.program_id(0)
    tm, d = x_ref.shape
    tile_in_seq = i % tiles_per_seq
    n_groups = len(POOL_WINDOWS)
    a_cols = d // n_groups

    @pl.when(tile_in_seq == 0)
    def _():
        ext_ref[0:POOL_HALO, :] = jnp.zeros((POOL_HALO, POOL_WIDTH), F32)

    @pl.when(tile_in_seq != 0)
    def _():
        ext_ref[0:POOL_HALO, :] = ext_ref[tm:, :]

    ext_ref[POOL_HALO:, :] = u_ref[...]

    pos = tile_in_seq * tm + lax.broadcasted_iota(jnp.int32, (tm, 1), 0)
    for g, w in enumerate(POOL_WINDOWS):
        acols = slice(g * a_cols, (g + 1) * a_cols)
        a_ref[:, acols] = jnp.dot(ya_ref[...], pa_ref[:, acols],
                                  preferred_element_type=F32)
        cols = slice(g * POOL_GROUP, (g + 1) * POOL_GROUP)
        acc = ext_ref[:, cols]
        shift = 1
        while shift < w:
            acc = acc + pltpu.roll(acc, shift, 0)
            shift *= 2
        acc = acc[POOL_HALO:]
        cur = ext_ref[POOL_HALO:, cols]
        cnt = jnp.minimum(pos + 1, w).astype(F32)
        dev = (acc / cnt - cur).astype(BF16)
        y = jnp.dot(dev, pw_ref[g], preferred_element_type=F32)
        yb_ref[:, cols] = (y * ps_ref[:, cols]).astype(BF16)

    bb = jnp.dot(yb_ref[...], pb_ref[...], preferred_element_type=F32)
    tanh_a = jnp.tanh(gates_ref[:, :d].astype(F32))
    tanh_b = jnp.tanh(gates_ref[:, d:].astype(F32))
    merged = 0.5 * ((1.0 + tanh_a) * a_ref[...] + (1.0 + tanh_b) * bb)
    y = jnp.dot(merged.astype(BF16), wo_ref[...], preferred_element_type=F32)
    o_ref[...] = x_ref[...] + y


def _mix(ya, u, gates, x2d, pool_w, pool_scale, w_a, w_b, w_o, *, seq, tm):
    n, d = x2d.shape
    tiles_per_seq = seq // tm
    const = lambda *shape: pl.BlockSpec(shape, lambda i: (0,) * len(shape),
                                        pipeline_mode=pl.Buffered(1))
    return pl.pallas_call(
        functools.partial(_mix_body, tiles_per_seq=tiles_per_seq),
        grid=(n // tm,),
        in_specs=[
            pl.BlockSpec((tm, ATTN_WIDTH), lambda i: (i, 0)),
            pl.BlockSpec((tm, POOL_WIDTH), lambda i: (i, 0)),
            pl.BlockSpec((tm, 2 * d), lambda i: (i, 0)),
            pl.BlockSpec((tm, d), lambda i: (i, 0)),
            const(len(POOL_WINDOWS), POOL_GROUP, POOL_GROUP),
            const(1, POOL_WIDTH),
            const(ATTN_WIDTH, d),
            const(POOL_WIDTH, d),
            const(d, d),
        ],
        out_specs=pl.BlockSpec((tm, d), lambda i: (i, 0)),
        out_shape=jax.ShapeDtypeStruct((n, d), F32),
        scratch_shapes=[
            pltpu.VMEM((POOL_HALO + tm, POOL_WIDTH), F32),
            pltpu.VMEM((tm, POOL_WIDTH), BF16),
            pltpu.VMEM((tm, d), F32),
        ],
        compiler_params=_compiler_params(("arbitrary",)),
        name="mix",
    )(ya, u, gates, x2d, pool_w, pool_scale, w_a, w_b, w_o)


def _ffn_body(x_ref, g_ref, wgu_ref, wd_ref, gf_ref, o_ref, h_ref):
    j = pl.program_id(1)

    def slab(first, last=False):
        if first:
            x = x_ref[...]
            h = _rms(x, g_ref[...]).astype(BF16)
            h_ref[...] = h
        else:
            h = h_ref[...]
        tf = wd_ref.shape[0]
        gate = jnp.dot(h, wgu_ref[:, :tf], preferred_element_type=F32)
        up = jnp.dot(h, wgu_ref[:, tf:], preferred_element_type=F32)
        act = (gate * _sigmoid(gate) * up).astype(BF16)
        down = jnp.dot(act, wd_ref[...], preferred_element_type=F32)
        total = (x if first else o_ref[...]) + down
        o_ref[...] = _rms(total, gf_ref[...]) if last else total

    last_j = pl.num_programs(1) - 1

    @pl.when(j == 0)
    def _():
        slab(True)

    @pl.when((j > 0) & (j < last_j))
    def _():
        slab(False)

    @pl.when(j == last_j)
    def _():
        slab(False, last=True)


def _ffn(x2d, g, w_gate_up, w_down, g_final, *, tm, tf):
    n, d = x2d.shape
    d_ff = w_down.shape[0]
    n_f = d_ff // tf
    return pl.pallas_call(
        _ffn_body,
        grid=(n // tm, n_f),
        in_specs=[
            pl.BlockSpec((tm, d), lambda i, j: (i, 0)),
            pl.BlockSpec((1, d), lambda i, j: (0, 0)),
            pl.BlockSpec((d, 2 * tf), lambda i, j: (0, j)),
            pl.BlockSpec((tf, d), lambda i, j: (j, 0)),
            pl.BlockSpec((1, d), lambda i, j: (0, 0)),
        ],
        out_specs=pl.BlockSpec((tm, d), lambda i, j: (i, 0)),
        out_shape=jax.ShapeDtypeStruct((n, d), F32),
        scratch_shapes=[pltpu.VMEM((tm, d), BF16)],
        compiler_params=_compiler_params(("arbitrary", "arbitrary")),
        name="ffn",
    )(x2d, g, w_gate_up, w_down, g_final)


def _layer(x2d, norm_mix, w_in, rel_bias, pool_w, pool_scale, w_a, w_b, w_o,
           norm_ffn, w_gate_up, w_down, norm_out, *, batch, seq):
    d = x2d.shape[1]
    qkv, u, gates = _in_proj(x2d, norm_mix.reshape(1, d), w_in, tm=IN_PROJ_ROWS)
    pool_w2d = pool_w.reshape(-1, pool_w.shape[-1])
    ya, (pool_w2d, w_a, w_b, w_o, w_gate_up, w_down) = _attention(
        qkv, _bias_base(rel_bias), (pool_w2d, w_a, w_b, w_o, w_gate_up, w_down),
        paired_cast=4, batch=batch, seq=seq)
    x1 = _mix(ya, u, gates, x2d, pool_w2d.reshape(pool_w.shape), pool_scale.reshape(1, -1),
              w_a, w_b, w_o, seq=seq, tm=MIX_ROWS)
    return _ffn(x1, norm_ffn.reshape(1, d), w_gate_up, w_down,
                norm_out.reshape(1, d), tm=FFN_ROWS, tf=FFN_COLS)


def kernel(x, norm_mix, w_in, rel_bias, pool_w, pool_scale, w_branch_a, w_branch_b,
           w_out, norm_ffn, w_gate_up, w_down, norm_final):
    batch, seq, d = x.shape
    depth = w_in.shape[0]
    assert depth == 1, "the fused ffn kernel applies the final norm to the only layer"
    out = _layer(x.reshape(batch * seq, d), norm_mix[0], w_in[0], rel_bias[0],
                 pool_w[0], pool_scale[0], w_branch_a[0], w_branch_b[0], w_out[0],
                 norm_ffn[0], w_gate_up[0], w_down[0], norm_final,
                 batch=batch, seq=seq)
    return out.reshape(batch, seq, d)
```

```python
import functools
import math

import jax
import jax.numpy as jnp
from jax import lax
from jax.experimental import pallas as pl
from jax.experimental.pallas import tpu as pltpu

F32 = jnp.float32
BF16 = jnp.bfloat16

CHUNK = 64
LEFT_CHUNKS = 8
N_HEADS = 16
HEAD_DIM = 64
ATTN_WIDTH = N_HEADS * HEAD_DIM
MAX_REL = 128
POOL_WINDOWS = (2, 4, 8, 16)
POOL_GROUP = 256
POOL_WIDTH = POOL_GROUP * len(POOL_WINDOWS)
EPS = 1e-6
NEG_INF = -1e30
LOG2_E = math.log2(math.e)

Q_ROWS = 2 * CHUNK
BAND = (LEFT_CHUNKS + 2) * CHUNK
PAD_ROWS = LEFT_CHUNKS * CHUNK
BIAS_EXT = BAND + Q_ROWS
POOL_HALO = 16
PAIR = 2 * HEAD_DIM
N_PAIRS = N_HEADS // 2
ONES_ROWS = 16

V7X_VMEM_BYTES = 64 * 1024 * 1024
VMEM_LIMIT_BYTES = V7X_VMEM_BYTES - 4 * 1024 * 1024
IN_PROJ_ROWS = 256
STAGE_ROWS = 128
MIX_ROWS = 256
FFN_ROWS = 1024
FFN_COLS = 512


def _rms(x, g):
    ms = jnp.mean(x * x, axis=-1, keepdims=True)
    return x * lax.rsqrt(ms + EPS) * g


def _sigmoid(x):
    return 0.5 * jnp.tanh(0.5 * x) + 0.5


def _compiler_params(semantics):
    return pltpu.CompilerParams(
        dimension_semantics=semantics, vmem_limit_bytes=VMEM_LIMIT_BYTES)


def _in_proj_body(x_ref, g_ref, w_hbm, qt_ref, k_ref, vt_ref, dev_ref, gate_ref, w_ref,
                  stage_ref, ext_ref, sem, *, tiles_per_seq):
    n_chunks = w_ref.shape[0] // STAGE_ROWS

    def chunk_copy(c, slot):
        rows = pl.ds(pl.multiple_of(c * STAGE_ROWS, STAGE_ROWS), STAGE_ROWS)
        return pltpu.make_async_copy(w_hbm.at[rows], stage_ref.at[slot], sem.at[slot])

    @pl.when(pl.program_id(0) == 0)
    def _():
        chunk_copy(0, 0).start()

        def two_chunks(cc, carry):
            for slot in range(2):
                c = 2 * cc + slot

                @pl.when(c + 1 < n_chunks)
                def _():
                    chunk_copy(c + 1, 1 - slot).start()

                chunk_copy(c, slot).wait()
                rows = pl.ds(pl.multiple_of(c * STAGE_ROWS, STAGE_ROWS), STAGE_ROWS)
                gate0 = w_ref.shape[1] - gate_ref.shape[1]
                w_ref[rows, :gate0] = stage_ref[slot, :, :gate0].astype(BF16)
                w_ref[rows, gate0:] = (0.5 * stage_ref[slot, :, gate0:]).astype(BF16)
            return carry

        lax.fori_loop(0, n_chunks // 2, two_chunks, 0)

    h = _rms(x_ref[...], g_ref[...]).astype(BF16)
    qk_end = 2 * ATTN_WIDTH
    q_end = qk_end + ATTN_WIDTH
    u_end = q_end + dev_ref.shape[1]
    q = jnp.dot(h, w_ref[:, :ATTN_WIDTH], preferred_element_type=F32) * (
        LOG2_E / math.sqrt(HEAD_DIM))
    lane = lax.broadcasted_iota(jnp.int32, (CHUNK, PAIR), 1)
    zero = jnp.zeros((CHUNK, PAIR), F32)
    for blk in range(qt_ref.shape[0]):
        for hp in range(N_PAIRS):
            q2 = q[blk * Q_ROWS:(blk + 1) * Q_ROWS, hp * PAIR:(hp + 1) * PAIR]
            qm = jnp.concatenate(
                [jnp.where((lane < HEAD_DIM) == (a == 0), q2[c * CHUNK:(c + 1) * CHUNK], zero)
                 for c in range(2) for a in range(2)], axis=0)
            qt_ref[blk, hp] = qm.T.astype(BF16)
    k_ref[...] = jnp.dot(h, w_ref[:, ATTN_WIDTH:qk_end],
                         preferred_element_type=F32).astype(BF16)
    v = jnp.dot(h, w_ref[:, qk_end:q_end], preferred_element_type=F32)
    for blk in range(vt_ref.shape[0]):
        blk_t = v[blk * Q_ROWS:(blk + 1) * Q_ROWS, :].T.astype(BF16)
        for hp in range(N_PAIRS):
            vt_ref[blk, hp, 0:PAIR, :] = blk_t[hp * PAIR:(hp + 1) * PAIR, :]
    vt_ref[:, :, PAIR:, :] = jnp.ones((vt_ref.shape[0], N_PAIRS, ONES_ROWS, Q_ROWS), BF16)
    _pool_deviation(jnp.dot(h, w_ref[:, q_end:u_end], preferred_element_type=F32),
                    ext_ref, dev_ref, pl.program_id(0) % tiles_per_seq)
    gate_ref[...] = jnp.dot(h, w_ref[:, u_end:], preferred_element_type=F32).astype(BF16)


def _pool_deviation(u, ext_ref, dev_ref, tile_in_seq):
    tm = u.shape[0]

    @pl.when(tile_in_seq == 0)
    def _():
        ext_ref[0:POOL_HALO, :] = jnp.zeros((POOL_HALO, POOL_WIDTH), F32)

    @pl.when(tile_in_seq != 0)
    def _():
        ext_ref[0:POOL_HALO, :] = ext_ref[tm:, :]

    ext_ref[POOL_HALO:, :] = u
    pos = tile_in_seq * tm + lax.broadcasted_iota(jnp.int32, (tm, 1), 0)
    for g, w in enumerate(POOL_WINDOWS):
        cols = slice(g * POOL_GROUP, (g + 1) * POOL_GROUP)
        acc = ext_ref[:, cols]
        shift = 1
        while shift < w:
            acc = acc + pltpu.roll(acc, shift, 0)
            shift *= 2
        cnt = jnp.minimum(pos + 1, w).astype(F32)
        dev_ref[:, cols] = (acc[POOL_HALO:] / cnt - ext_ref[POOL_HALO:, cols]).astype(BF16)


def _in_proj(x2d, g, w, *, tm, seq):
    n, d = x2d.shape
    in_width = w.shape[1]
    n_gate = in_width - 3 * ATTN_WIDTH - POOL_WIDTH
    assert d % (2 * STAGE_ROWS) == 0
    return pl.pallas_call(
        functools.partial(_in_proj_body, tiles_per_seq=seq // tm),
        grid=(n // tm,),
        in_specs=[
            pl.BlockSpec((tm, d), lambda i: (i, 0)),
            pl.BlockSpec((1, d), lambda i: (0, 0)),
            pl.BlockSpec(memory_space=pl.ANY),
        ],
        out_specs=[
            pl.BlockSpec((tm // Q_ROWS, N_PAIRS, PAIR, 2 * Q_ROWS), lambda i: (i, 0, 0, 0)),
            pl.BlockSpec((tm, ATTN_WIDTH), lambda i: (i, 0)),
            pl.BlockSpec((tm // Q_ROWS, N_PAIRS, PAIR + ONES_ROWS, Q_ROWS),
                         lambda i: (i, 0, 0, 0)),
            pl.BlockSpec((tm, POOL_WIDTH), lambda i: (i, 0)),
            pl.BlockSpec((tm, n_gate), lambda i: (i, 0)),
        ],
        out_shape=[
            jax.ShapeDtypeStruct((n // Q_ROWS, N_PAIRS, PAIR, 2 * Q_ROWS), BF16),
            jax.ShapeDtypeStruct((n, ATTN_WIDTH), BF16),
            jax.ShapeDtypeStruct((n // Q_ROWS, N_PAIRS, PAIR + ONES_ROWS, Q_ROWS), BF16),
            jax.ShapeDtypeStruct((n, POOL_WIDTH), BF16),
            jax.ShapeDtypeStruct((n, n_gate), BF16),
        ],
        scratch_shapes=[
            pltpu.VMEM((d, in_width), BF16),
            pltpu.VMEM((2, STAGE_ROWS, in_width), F32),
            pltpu.VMEM((POOL_HALO + tm, POOL_WIDTH), F32),
            pltpu.SemaphoreType.DMA((2,)),
        ],
        compiler_params=_compiler_params(("arbitrary",)),
        name="in_proj",
    )(x2d, g, w)


WIN_BLOCKS = BAND // Q_ROWS
PAD_BLOCKS = PAD_ROWS // Q_ROWS
FLAT_END = PAD_ROWS - MAX_REL
BIAS_ROWS = BAND - FLAT_END
SEEN = BAND - CHUNK
SUBS = 4
CAST_ROWS = 16


def _attn_body(base_ref, q_ref, k_ref, v_ref, *rest, n_cast, paired_cast):
    cast_in = rest[:n_cast]
    o_ref = rest[n_cast]
    cast_out = rest[n_cast + 1:2 * n_cast + 1]
    bias_ref, s_ref, p_ref = rest[2 * n_cast + 1:]
    b = pl.program_id(0)
    p = pl.program_id(1)

    @pl.when((b == 0) & (p == 0))
    def _():
        for hp in range(N_PAIRS):
            heads = []
            for a in range(2):
                h = 2 * hp + a
                base = jnp.broadcast_to(base_ref[h:h + 1, :], (BAND, BIAS_EXT))
                rot = pltpu.roll(base, 0, 1, stride=1, stride_axis=0)
                heads.append((rot[FLAT_END:, :Q_ROWS] - base[FLAT_END:, 0:1]) * LOG2_E)
            bias_ref[hp] = jnp.concatenate(
                [heads[a][:, c * CHUNK:(c + 1) * CHUNK] for c in range(2) for a in range(2)],
                axis=1)
        p_ref[:, SEEN:, :Q_ROWS] = jnp.zeros((2, CHUNK, Q_ROWS), BF16)
        p_ref[:, :CHUNK, Q_ROWS:] = jnp.zeros((2, CHUNK, Q_ROWS), BF16)

    row = lax.broadcasted_iota(jnp.int32, (PAIR, Q_ROWS), 0)
    first_head_rows = row < HEAD_DIM
    first_chunk_lanes = lax.broadcasted_iota(jnp.int32, (PAIR, Q_ROWS), 1) < CHUNK

    def step(mask_missing_keys):
        units = [(sub, hp) for sub in range(SUBS) for hp in range(N_PAIRS)]

        def key_block(sub, t):
            if mask_missing_keys:
                return max(sub + t - PAD_BLOCKS, 0)
            return p * SUBS + sub + t - PAD_BLOCKS

        def scores(u):
            sub, hp = units[u]
            blk = p * SUBS + sub
            lanes = slice(hp * PAIR, (hp + 1) * PAIR)
            qm_t = q_ref[sub, hp]
            if mask_missing_keys:
                k2 = jnp.concatenate(
                    [k_ref[key_block(sub, t) * Q_ROWS:(key_block(sub, t) + 1) * Q_ROWS, lanes]
                     for t in range(WIN_BLOCKS)], axis=0)
            else:
                first = pl.multiple_of(key_block(sub, 0) * Q_ROWS, Q_ROWS)
                k2 = k_ref[pl.ds(first, BAND), lanes]
            s = jnp.dot(k2, qm_t, preferred_element_type=F32)
            pieces = [(0, FLAT_END, 0, None), (FLAT_END, SEEN, 0, (0, SEEN - FLAT_END)),
                      (CHUNK, FLAT_END, Q_ROWS, None), (FLAT_END, BAND, Q_ROWS, (0, BIAS_ROWS))]
            for r0, r1, l0, bias_rows in pieces:
                part = s[r0:r1, l0:l0 + Q_ROWS]
                if bias_rows is not None:
                    part = part + bias_ref[hp, bias_rows[0]:bias_rows[1], l0:l0 + Q_ROWS]
                if mask_missing_keys:
                    key = r0 + lax.broadcasted_iota(jnp.int32, part.shape, 0)
                    part = jnp.where(key >= PAD_ROWS - blk * Q_ROWS, part, NEG_INF)
                s_ref[u % 2, r0:r1, l0:l0 + Q_ROWS] = part

        def weights(u):
            slot = u % 2
            for r0, l0 in ((0, 0), (CHUNK, Q_ROWS)):
                s = s_ref[slot, r0:r0 + SEEN, l0:l0 + Q_ROWS]
                m = jnp.max(s, axis=0, keepdims=True)
                p_ref[slot, r0:r0 + SEEN, l0:l0 + Q_ROWS] = jnp.exp2(s - m).astype(BF16)

        def values(u):
            sub, hp = units[u]
            vt2 = jnp.concatenate(
                [v_ref[key_block(sub, t), hp] for t in range(WIN_BLOCKS)], axis=1)
            o_t = jnp.dot(vt2, p_ref[u % 2], preferred_element_type=F32)
            o_t = o_t[0:PAIR] / o_t[PAIR:PAIR + 1]
            t0, t1 = o_t[:, :Q_ROWS], o_t[:, Q_ROWS:]
            head_a = jnp.where(first_chunk_lanes, t0, pltpu.roll(t1, CHUNK, 1))
            head_b = jnp.where(first_chunk_lanes, pltpu.roll(t0, CHUNK, 1), t1)
            own = jnp.where(first_head_rows, head_a, head_b)
            o_ref[sub * Q_ROWS:(sub + 1) * Q_ROWS, hp * PAIR:(hp + 1) * PAIR] = (
                own.T.astype(BF16))

        casts = [(k, r0) for k, src in enumerate(cast_in)
                 for r0 in range(0, src.shape[0], CAST_ROWS)]
        casts_per_unit = pl.cdiv(len(casts), len(units))

        def cast_slab(k, r0):
            src, dst = cast_in[k], cast_out[k]
            rows = slice(r0, r0 + CAST_ROWS)
            if k != paired_cast:
                dst[rows, :] = src[rows, :].astype(BF16)
                return
            half = src.shape[1] // 2
            for j in range(half // FFN_COLS):
                for part in range(2):
                    c_src = part * half + j * FFN_COLS
                    c_dst = (2 * j + part) * FFN_COLS
                    dst[rows, c_dst:c_dst + FFN_COLS] = (
                        src[rows, c_src:c_src + FFN_COLS].astype(BF16))

        scores(0)
        for u in range(len(units) + 1):
            if u + 1 < len(units):
                scores(u + 1)
            if u >= 1:
                values(u - 1)
            if u < len(units):
                weights(u)
                for k, r0 in casts[u * casts_per_unit:(u + 1) * casts_per_unit]:
                    cast_slab(k, r0)

    @pl.when(p < PAD_BLOCKS // SUBS)
    def _():
        step(True)

    @pl.when(p >= PAD_BLOCKS // SUBS)
    def _():
        step(False)


def _attention(qt, k, vt, bias_base, cast_weights, *, paired_cast, batch, seq):
    n = batch * seq
    rows = SUBS * Q_ROWS
    steps = seq // rows
    n_steps = batch * steps
    cast_blocks = []
    for w in cast_weights:
        assert w.ndim == 2 and w.shape[0] % (n_steps * CAST_ROWS) == 0, w.shape
        cast_blocks.append(pl.BlockSpec((w.shape[0] // n_steps, w.shape[1]),
                                        lambda b, p: (b * steps + p, 0)))
    outs = pl.pallas_call(
        functools.partial(_attn_body, n_cast=len(cast_weights), paired_cast=paired_cast),
        grid=(batch, steps),
        in_specs=[
            pl.BlockSpec((N_HEADS, BIAS_EXT), lambda b, p: (0, 0)),
            pl.BlockSpec((SUBS, N_PAIRS, PAIR, 2 * Q_ROWS), lambda b, p: (b * steps + p, 0, 0, 0)),
            pl.BlockSpec((seq, ATTN_WIDTH), lambda b, p: (b, 0)),
            pl.BlockSpec((seq // Q_ROWS, N_PAIRS, PAIR + ONES_ROWS, Q_ROWS),
                         lambda b, p: (b, 0, 0, 0)),
        ] + cast_blocks,
        out_specs=[pl.BlockSpec((rows, ATTN_WIDTH), lambda b, p: (b * steps + p, 0))]
        + cast_blocks,
        out_shape=[jax.ShapeDtypeStruct((n, ATTN_WIDTH), BF16)]
        + [jax.ShapeDtypeStruct(w.shape, BF16) for w in cast_weights],
        scratch_shapes=[
            pltpu.VMEM((N_PAIRS, BIAS_ROWS, 2 * Q_ROWS), F32),
            pltpu.VMEM((2, BAND, 2 * Q_ROWS), F32),
            pltpu.VMEM((2, BAND, 2 * Q_ROWS), BF16),
        ],
        compiler_params=_compiler_params(("arbitrary", "arbitrary")),
        name="attn",
    )(bias_base, qt, k, vt, *cast_weights)
    return outs[0], outs[1:]


def _bias_base(rel_bias):
    far = rel_bias[:, -1:]
    head = jnp.broadcast_to(far, (N_HEADS, Q_ROWS))
    tail = jnp.broadcast_to(far, (N_HEADS, BIAS_EXT - Q_ROWS - 2 * MAX_REL - 1))
    return jnp.concatenate([head, rel_bias, tail], axis=1)


def _mix_body(ya_ref, dev_ref, gates_ref, x_ref, pw_ref, ps_ref,
              pa_ref, pb_ref, wo_ref, o_ref, yb_ref, a_ref):
    tm, d = x_ref.shape
    n_groups = len(POOL_WINDOWS)
    a_cols = d // n_groups

    for g in range(n_groups):
        acols = slice(g * a_cols, (g + 1) * a_cols)
        a_ref[:, acols] = jnp.dot(ya_ref[...], pa_ref[:, acols],
                                  preferred_element_type=F32)
        cols = slice(g * POOL_GROUP, (g + 1) * POOL_GROUP)
        y = jnp.dot(dev_ref[:, cols], pw_ref[g], preferred_element_type=F32)
        yb_ref[:, cols] = (y * ps_ref[:, cols]).astype(BF16)

    bb = jnp.dot(yb_ref[...], pb_ref[...], preferred_element_type=F32)
    tanh_a = jnp.tanh(gates_ref[:, :d].astype(F32))
    tanh_b = jnp.tanh(gates_ref[:, d:].astype(F32))
    merged = 0.5 * ((1.0 + tanh_a) * a_ref[...] + (1.0 + tanh_b) * bb)
    y = jnp.dot(merged.astype(BF16), wo_ref[...], preferred_element_type=F32)
    o_ref[...] = x_ref[...] + y


def _mix(ya, dev, gates, x2d, pool_w, pool_scale, w_a, w_b, w_o, *, tm):
    n, d = x2d.shape
    const = lambda *shape: pl.BlockSpec(shape, lambda i: (0,) * len(shape),
                                        pipeline_mode=pl.Buffered(1))
    return pl.pallas_call(
        _mix_body,
        grid=(n // tm,),
        in_specs=[
            pl.BlockSpec((tm, ATTN_WIDTH), lambda i: (i, 0)),
            pl.BlockSpec((tm, POOL_WIDTH), lambda i: (i, 0)),
            pl.BlockSpec((tm, 2 * d), lambda i: (i, 0)),
            pl.BlockSpec((tm, d), lambda i: (i, 0)),
            const(len(POOL_WINDOWS), POOL_GROUP, POOL_GROUP),
            const(1, POOL_WIDTH),
            const(ATTN_WIDTH, d),
            const(POOL_WIDTH, d),
            const(d, d),
        ],
        out_specs=pl.BlockSpec((tm, d), lambda i: (i, 0)),
        out_shape=jax.ShapeDtypeStruct((n, d), F32),
        scratch_shapes=[
            pltpu.VMEM((tm, POOL_WIDTH), BF16),
            pltpu.VMEM((tm, d), F32),
        ],
        compiler_params=_compiler_params(("arbitrary",)),
        name="mix",
    )(ya, dev, gates, x2d, pool_w, pool_scale, w_a, w_b, w_o)


def _ffn_body(x_ref, g_ref, wgu_ref, wd_ref, gf_ref, o_ref, h_ref):
    j = pl.program_id(1)

    def slab(first, last=False):
        if first:
            x = x_ref[...]
            h = _rms(x, g_ref[...]).astype(BF16)
            h_ref[...] = h
        else:
            h = h_ref[...]
        tf = wd_ref.shape[0]
        gate = jnp.dot(h, wgu_ref[:, :tf], preferred_element_type=F32)
        up = jnp.dot(h, wgu_ref[:, tf:], preferred_element_type=F32)
        act = (gate * _sigmoid(gate) * up).astype(BF16)
        down = jnp.dot(act, wd_ref[...], preferred_element_type=F32)
        total = (x if first else o_ref[...]) + down
        o_ref[...] = _rms(total, gf_ref[...]) if last else total

    last_j = pl.num_programs(1) - 1

    @pl.when(j == 0)
    def _():
        slab(True)

    @pl.when((j > 0) & (j < last_j))
    def _():
        slab(False)

    @pl.when(j == last_j)
    def _():
        slab(False, last=True)


def _ffn(x2d, g, w_gate_up, w_down, g_final, *, tm, tf):
    n, d = x2d.shape
    d_ff = w_down.shape[0]
    n_f = d_ff // tf
    return pl.pallas_call(
        _ffn_body,
        grid=(n // tm, n_f),
        in_specs=[
            pl.BlockSpec((tm, d), lambda i, j: (i, 0)),
            pl.BlockSpec((1, d), lambda i, j: (0, 0)),
            pl.BlockSpec((d, 2 * tf), lambda i, j: (0, j)),
            pl.BlockSpec((tf, d), lambda i, j: (j, 0)),
            pl.BlockSpec((1, d), lambda i, j: (0, 0)),
        ],
        out_specs=pl.BlockSpec((tm, d), lambda i, j: (i, 0)),
        out_shape=jax.ShapeDtypeStruct((n, d), F32),
        scratch_shapes=[pltpu.VMEM((tm, d), BF16)],
        compiler_params=_compiler_params(("arbitrary", "arbitrary")),
        name="ffn",
    )(x2d, g, w_gate_up, w_down, g_final)


def _layer(x2d, norm_mix, w_in, rel_bias, pool_w, pool_scale, w_a, w_b, w_o,
           norm_ffn, w_gate_up, w_down, norm_out, *, batch, seq):
    d = x2d.shape[1]
    qt, k, vt, dev, gates = _in_proj(x2d, norm_mix.reshape(1, d), w_in, tm=IN_PROJ_ROWS,
                                     seq=seq)
    pool_w2d = pool_w.reshape(-1, pool_w.shape[-1])
    ya, (pool_w2d, w_a, w_b, w_o, w_gate_up, w_down) = _attention(
        qt, k, vt, _bias_base(rel_bias), (pool_w2d, w_a, w_b, w_o, w_gate_up, w_down),
        paired_cast=4, batch=batch, seq=seq)
    x1 = _mix(ya, dev, gates, x2d, pool_w2d.reshape(pool_w.shape), pool_scale.reshape(1, -1),
              w_a, w_b, w_o, tm=MIX_ROWS)
    return _ffn(x1, norm_ffn.reshape(1, d), w_gate_up, w_down,
                norm_out.reshape(1, d), tm=FFN_ROWS, tf=FFN_COLS)


def kernel(x, norm_mix, w_in, rel_bias, pool_w, pool_scale, w_branch_a, w_branch_b,
           w_out, norm_ffn, w_gate_up, w_down, norm_final):
    batch, seq, d = x.shape
    depth = w_in.shape[0]
    assert depth == 1, "the fused ffn kernel applies the final norm to the only layer"
    out = _layer(x.reshape(batch * seq, d), norm_mix[0], w_in[0], rel_bias[0],
                 pool_w[0], pool_scale[0], w_branch_a[0], w_branch_b[0], w_out[0],
                 norm_ffn[0], w_gate_up[0], w_down[0], norm_final,
                 batch=batch, seq=seq)
    return out.reshape(batch, seq, d)
```
